```python
import math
import jax
import jax.numpy as jnp
from jax import lax
import numpy as np

D_MODEL = 1024
BATCH = 16
SEQ = 2048
DEPTH = 2

HEAD_DIM = 64
N_SLOTS = D_MODEL // HEAD_DIM
A_HEADS = 8
A_LATENT = 128
IDX_HEADS = 8
IDX_DIM = 64
A_TOPK_MAX = 256
B_HEADS = 8
B_KV_HEADS = 2
B_WINDOW = 128
C_HEADS = 16
C_KV_HEADS = 2
CMP_BLOCK = 32
CMP_STRIDE = 16
CMP_HIDDEN = 128
SLC_BLOCK = 64
SLC_COUNT = 8
C_WINDOW = 512
MEM_LEN = 256
X_HEADS = 4
X_HEAD_DIM = 128
D_FF = 2816
N_EXPERTS = 8
TOP_K = 2
D_FF_EXPERT = 3584
REL_BUCKETS = 32
REL_MAX_DIST = 128
Q_BLOCK = 128
SLC_Q_BLOCK = 32
LN_EPS = 1e-5
NEG_INF = -1e30
FORCE_BONUS = 1e6
ALPHA = (2 * DEPTH) ** 0.25
BETA = (8 * DEPTH) ** -0.25
N_EVEN = (DEPTH + 1) // 2
N_ODD = DEPTH // 2
EVEN_WIDTHS = (A_HEADS * HEAD_DIM, A_LATENT, IDX_HEADS * IDX_DIM, IDX_DIM, IDX_HEADS,
               B_HEADS * HEAD_DIM, B_KV_HEADS * HEAD_DIM, B_KV_HEADS * HEAD_DIM)
EVEN_COLS = sum(EVEN_WIDTHS)
ODD_WIDTHS = (C_HEADS * HEAD_DIM,) + (C_KV_HEADS * HEAD_DIM,) * 6 + (3 * C_HEADS,)
ODD_COLS = sum(ODD_WIDTHS)

kernel_name = "hybrid_dsa_swa_nsa_moe_deepnorm"


def split_cols(a, widths):
    cuts = [int(c) for c in np.cumsum(widths)[:-1]]
    return jnp.split(a, cuts, axis=-1)


def layer_norm(x, g, b):
    xf = x.astype(jnp.float32)
    mu = jnp.mean(xf, -1, keepdims=True)
    var = jnp.mean(jnp.square(xf - mu), -1, keepdims=True)
    return ((xf - mu) * lax.rsqrt(var + LN_EPS)).astype(x.dtype) * g + b


def rms_norm(x, g):
    xf = x.astype(jnp.float32)
    y = xf * lax.rsqrt(jnp.mean(jnp.square(xf), -1, keepdims=True) + LN_EPS)
    return y.astype(x.dtype) * g


def rel_bucket(dist):
    n = jnp.maximum(dist, 0)
    max_exact = REL_BUCKETS // 2
    nf = jnp.maximum(n, 1).astype(jnp.float32)
    log_b = max_exact + (jnp.log(nf / max_exact) / math.log(REL_MAX_DIST / max_exact)
                         * (REL_BUCKETS - max_exact)).astype(jnp.int32)
    return jnp.where(n < max_exact, n, jnp.minimum(log_b, REL_BUCKETS - 1))


def banded_attention(q, k, v, rel_tbl, window, sinks=None):
    B, S, H, D = q.shape
    kvh = k.shape[2]
    g = H // kvh
    pad = -(-window // Q_BLOCK) * Q_BLOCK
    span = pad + Q_BLOCK
    kp = jnp.pad(k, ((0, 0), (pad, 0), (0, 0), (0, 0)))
    vp = jnp.pad(v, ((0, 0), (pad, 0), (0, 0), (0, 0)))
    dist = np.arange(Q_BLOCK)[:, None] + pad - np.arange(span)[None, :]
    band = jnp.asarray((dist >= 0) & (dist < window))
    bias = rel_tbl.astype(jnp.float32)[rel_bucket(jnp.asarray(dist))]
    bias = bias.transpose(2, 0, 1).reshape(kvh, g, Q_BLOCK, span)
    key_off = jnp.arange(span) - pad

    def block(n):
        t0 = n * Q_BLOCK
        qb = lax.dynamic_slice_in_dim(q, t0, Q_BLOCK, 1).reshape(B, Q_BLOCK, kvh, g, D)
        kb = lax.dynamic_slice_in_dim(kp, t0, span, 1)
        vb = lax.dynamic_slice_in_dim(vp, t0, span, 1)
        logits = jnp.einsum('bqkgd,bskd->bkgqs', qb, kb).astype(jnp.float32) * D ** -0.5 + bias
        ok = band & (t0 + key_off >= 0)[None, :]
        logits = jnp.where(ok, logits, NEG_INF)
        if sinks is None:
            p = jax.nn.softmax(logits, axis=-1)
        else:
            s = sinks.astype(jnp.float32).reshape(kvh, g, 1, 1)
            m = jnp.maximum(jnp.max(logits, -1, keepdims=True), s)
            e = jnp.exp(logits - m)
            p = e / (jnp.sum(e, -1, keepdims=True) + jnp.exp(s - m))
        o = jnp.einsum('bkgqs,bskd->bqkgd', p.astype(v.dtype), vb)
        return o.reshape(B, Q_BLOCK, H, D)

    out = lax.map(block, jnp.arange(S // Q_BLOCK))
    return jnp.moveaxis(out, 0, 1).reshape(B, S, H, D)


def dsa_attention(q_a, c_kv, q_i, k_i, w_i, ckv_gain, w_uk, w_uv, rel_tbl):
    B, S, _ = q_a.shape
    topk = min(A_TOPK_MAX, S // 4)
    q_lat = jnp.einsum('bshd,hdc->bshc', q_a.reshape(B, S, A_HEADS, HEAD_DIM), w_uk)
    c = rms_norm(c_kv, ckv_gain)
    qi = q_i.reshape(B, S, IDX_HEADS, IDX_DIM)
    wi = w_i * IDX_HEADS ** -0.5
    tbl = rel_tbl.astype(jnp.float32)
    kpos = jnp.arange(S)

    def block(n):
        t0 = n * Q_BLOCK
        qpos = t0 + jnp.arange(Q_BLOCK)
        qi_b = lax.dynamic_slice_in_dim(qi, t0, Q_BLOCK, 1)
        wi_b = lax.dynamic_slice_in_dim(wi, t0, Q_BLOCK, 1)
        ql_b = lax.dynamic_slice_in_dim(q_lat, t0, Q_BLOCK, 1)
        act = jax.nn.relu(jnp.einsum('bqhd,bsd->bqhs', qi_b, k_i) * IDX_DIM ** -0.5)
        score = jnp.einsum('bqh,bqhs->bqs', wi_b, act).astype(jnp.float32)
        score = jnp.where(kpos[None, :] <= qpos[:, None], score, NEG_INF)
        _, sel = lax.top_k(score, topk)
        c_sel = jax.vmap(lambda cb, sb: cb[sb])(c, sel)
        dist = qpos[None, :, None] - sel
        bias = tbl[rel_bucket(dist)].transpose(0, 1, 3, 2)
        lg = jnp.einsum('bqhc,bqkc->bqhk', ql_b, c_sel).astype(jnp.float32) * HEAD_DIM ** -0.5 + bias
        lg = jnp.where((dist >= 0)[:, :, None, :], lg, NEG_INF)
        p = jax.nn.softmax(lg, axis=-1).astype(c.dtype)
        o_lat = jnp.einsum('bqhk,bqkc->bqhc', p, c_sel)
        return jnp.einsum('bqhc,hcd->bqhd', o_lat, w_uv)

    out = lax.map(block, jnp.arange(S // Q_BLOCK))
    return jnp.moveaxis(out, 0, 1).reshape(B, S, A_HEADS, HEAD_DIM)


def compress_blocks(kv, pe, w1, w2):
    B, S, G, D = kv.shape
    n_cmp = (S - CMP_BLOCK) // CMP_STRIDE + 1
    idx = np.arange(n_cmp)[:, None] * CMP_STRIDE + np.arange(CMP_BLOCK)[None, :]
    blk = kv[:, idx] + pe[None, None, :, None, :]
    blk = blk.transpose(0, 1, 3, 2, 4).reshape(B, n_cmp, G, CMP_BLOCK * D)
    return jax.nn.gelu(blk @ w1) @ w2


def nsa_attention(q, k_cmp, v_cmp, k_slc, v_slc, k_win, v_win, gate_logits,
                  pe_k, w1_k, w2_k, pe_v, w1_v, w2_v, rel_tbl):
    B, S, H, D = q.shape
    G = C_KV_HEADS
    hpg = H // G
    qg = q.reshape(B, S, G, hpg, D)
    scale = D ** -0.5
    tbl = rel_tbl.astype(jnp.float32)
    pos = np.arange(S)

    n_cmp = (S - CMP_BLOCK) // CMP_STRIDE + 1
    kc = compress_blocks(k_cmp, pe_k, w1_k, w2_k)
    vc = compress_blocks(v_cmp, pe_v, w1_v, w2_v)
    cmp_end = np.arange(n_cmp) * CMP_STRIDE + CMP_BLOCK - 1
    dist_c = pos[:, None] - cmp_end[None, :]
    valid_c = jnp.asarray(dist_c >= 0)
    bias_c = tbl[rel_bucket(jnp.asarray(dist_c))].transpose(2, 0, 1).reshape(G, hpg, S, n_cmp)
    lg = jnp.einsum('bsghd,bngd->bghsn', qg, kc).astype(jnp.float32) * scale + bias_c
    p_cmp = jax.nn.softmax(jnp.where(valid_c, lg, NEG_INF), axis=-1) * valid_c
    o_cmp = jnp.einsum('bghsn,bngd->bsghd', p_cmp.astype(vc.dtype), vc).reshape(B, S, H, D)

    n_slc = S // SLC_BLOCK
    n_sel = min(SLC_COUNT, n_slc)
    cs = np.arange(n_cmp)[:, None] * CMP_STRIDE
    ss = np.arange(n_slc)[None, :] * SLC_BLOCK
    overlap = jnp.asarray(((cs < ss + SLC_BLOCK) & (cs + CMP_BLOCK > ss)).astype(np.float32))
    p_slc = jnp.einsum('bghsn,nj->bgsj', p_cmp, overlap)
    blk = pos // SLC_BLOCK
    jj = np.arange(n_slc)
    forced = (jj[None, :] == 0) | (jj[None, :] == blk[:, None]) | (jj[None, :] == blk[:, None] - 1)
    admissible = jj[None, :] * SLC_BLOCK <= pos[:, None]
    score = jnp.where(jnp.asarray(admissible),
                      p_slc + FORCE_BONUS * jnp.asarray(forced.astype(np.float32)), NEG_INF)
    _, sel = lax.top_k(score, n_sel)

    kb = k_slc.reshape(B, n_slc, SLC_BLOCK, G, D).transpose(0, 3, 1, 2, 4)
    vb = v_slc.reshape(B, n_slc, SLC_BLOCK, G, D).transpose(0, 3, 1, 2, 4)
    bi = jnp.arange(B)[:, None, None, None]
    gi = jnp.arange(G)[None, :, None, None]
    tbl_g = tbl.reshape(REL_BUCKETS, G, hpg).transpose(1, 0, 2)
    in_blk = jnp.arange(SLC_BLOCK)
    n_keys = n_sel * SLC_BLOCK

    def slc_block(n):
        t0 = n * SLC_Q_BLOCK
        qpos = t0 + jnp.arange(SLC_Q_BLOCK)
        qb = lax.dynamic_slice_in_dim(qg, t0, SLC_Q_BLOCK, 1)
        sb = lax.dynamic_slice_in_dim(sel, t0, SLC_Q_BLOCK, 2)
        ks = kb[bi, gi, sb].reshape(B, G, SLC_Q_BLOCK, n_keys, D)
        vs = vb[bi, gi, sb].reshape(B, G, SLC_Q_BLOCK, n_keys, D)
        kpos = (sb[..., None] * SLC_BLOCK + in_blk).reshape(B, G, SLC_Q_BLOCK, n_keys)
        dist = qpos[None, None, :, None] - kpos
        bias = tbl_g[gi, rel_bucket(dist)].transpose(0, 1, 4, 2, 3)
        lg2 = jnp.einsum('bqghd,bgqsd->bghqs', qb, ks).astype(jnp.float32) * scale + bias
        lg2 = jnp.where((dist >= 0)[:, :, None], lg2, NEG_INF)
        p = jax.nn.softmax(lg2, axis=-1).astype(vs.dtype)
        o = jnp.einsum('bghqs,bgqsd->bqghd', p, vs)
        return o.reshape(B, SLC_Q_BLOCK, H, D)

    o_slc = lax.map(slc_block, jnp.arange(S // SLC_Q_BLOCK))
    o_slc = jnp.moveaxis(o_slc, 0, 1).reshape(B, S, H, D)

    o_win = banded_attention(q, k_win, v_win, rel_tbl, C_WINDOW)

    gt = jax.nn.sigmoid(gate_logits).reshape(B, S, H, 3)
    return gt[..., 0:1] * o_cmp + gt[..., 1:2] * o_slc + gt[..., 2:3] * o_win


def even_mixer(x, w_in, ckv_gain, w_uk, w_uv, sinks, w_out, rel_table):
    B, S, _ = x.shape
    q_a, c_kv, q_i, k_i, w_i, q_b, k_b, v_b = split_cols(x @ w_in, EVEN_WIDTHS)
    o_a = dsa_attention(q_a, c_kv, q_i, k_i, w_i, ckv_gain, w_uk, w_uv, rel_table[:, :A_HEADS])
    o_b = banded_attention(q_b.reshape(B, S, B_HEADS, HEAD_DIM),
                           k_b.reshape(B, S, B_KV_HEADS, HEAD_DIM),
                           v_b.reshape(B, S, B_KV_HEADS, HEAD_DIM),
                           rel_table[:, A_HEADS:A_HEADS + B_HEADS], B_WINDOW, sinks)
    o = jnp.concatenate([o_a.reshape(B, S, -1), o_b.reshape(B, S, -1)], axis=-1)
    return o @ w_out


def odd_mixer(x, w_in, pe_k, w1_k, w2_k, pe_v, w1_v, w2_v, w_out, rel_table):
    B, S, _ = x.shape
    q, kc, vc, ks, vs, kw, vw, gl = split_cols(x @ w_in, ODD_WIDTHS)
    r = lambda a, h: a.reshape(B, S, h, HEAD_DIM)
    o = nsa_attention(r(q, C_HEADS), r(kc, C_KV_HEADS), r(vc, C_KV_HEADS), r(ks, C_KV_HEADS),
                      r(vs, C_KV_HEADS), r(kw, C_KV_HEADS), r(vw, C_KV_HEADS), gl,
                      pe_k, w1_k, w2_k, pe_v, w1_v, w2_v, rel_table[:, :C_HEADS])
    return o.reshape(B, S, C_HEADS * HEAD_DIM) @ w_out


def cross_attention(x, mem, w_q, w_k, w_v, w_o):
    B, S, _ = x.shape
    M = mem.shape[1]
    q = (x @ w_q).reshape(B, S, X_HEADS, X_HEAD_DIM)
    k = (mem @ w_k).reshape(B, M, X_HEADS, X_HEAD_DIM)
    v = (mem @ w_v).reshape(B, M, X_HEADS, X_HEAD_DIM)
    lg = jnp.einsum('bshd,bmhd->bhsm', q, k).astype(jnp.float32) * X_HEAD_DIM ** -0.5
    p = jax.nn.softmax(lg, axis=-1).astype(v.dtype)
    o = jnp.einsum('bhsm,bmhd->bshd', p, v).reshape(B, S, X_HEADS * X_HEAD_DIM)
    return o @ w_o


def dense_swiglu(x, w_gate, w_up, w_down):
    return (jax.nn.silu(x @ w_gate) * (x @ w_up)) @ w_down


def moe_swiglu(x, w_router, w_gate, w_up, w_down):
    logits = (x @ w_router).astype(jnp.float32)
    top_val, top_idx = lax.top_k(logits, TOP_K)
    top_w = jax.nn.softmax(top_val, axis=-1)
    gate = jnp.einsum('bsk,bske->bse', top_w,
                      jax.nn.one_hot(top_idx, N_EXPERTS, dtype=jnp.float32)).astype(x.dtype)
    y = jnp.zeros_like(x)
    for e in range(N_EXPERTS):
        h = jax.nn.silu(x @ w_gate[e]) * (x @ w_up[e])
        y = y + gate[..., e:e + 1] * (h @ w_down[e])
    return y


def setup_inputs(seed: int = 0) -> dict:
    key = jax.random.key(seed)
    ks = iter(jax.random.split(key, 32))

    def nrm(shape, scale):
        return jax.random.normal(next(ks), shape, jnp.float32) * scale

    D = D_MODEL
    XW = X_HEADS * X_HEAD_DIM
    CW = CMP_BLOCK * HEAD_DIM
    return {
        'x': nrm((BATCH, SEQ, D), 1.0),
        'mem': nrm((BATCH, MEM_LEN, D), 1.0),
        'rel_table': nrm((REL_BUCKETS, N_SLOTS), 0.2),
        'ev_w_in': nrm((N_EVEN, D, EVEN_COLS), D ** -0.5),
        'ev_ckv_gain': 1.0 + nrm((N_EVEN, A_LATENT), 0.02),
        'ev_w_uk': nrm((N_EVEN, A_HEADS, HEAD_DIM, A_LATENT), HEAD_DIM ** -0.5),
        'ev_w_uv': nrm((N_EVEN, A_HEADS, A_LATENT, HEAD_DIM), A_LATENT ** -0.5),
        'ev_sinks': nrm((N_EVEN, B_HEADS), 0.5),
        'ev_w_out': nrm((N_EVEN, N_SLOTS * HEAD_DIM, D), BETA * (N_SLOTS * HEAD_DIM) ** -0.5),
        'od_w_in': nrm((N_ODD, D, ODD_COLS), D ** -0.5),
        'od_pe_k': nrm((N_ODD, CMP_BLOCK, HEAD_DIM), 0.1),
        'od_w1_k': nrm((N_ODD, CW, CMP_HIDDEN), CW ** -0.5),
        'od_w2_k': nrm((N_ODD, CMP_HIDDEN, HEAD_DIM), CMP_HIDDEN ** -0.5),
        'od_pe_v': nrm((N_ODD, CMP_BLOCK, HEAD_DIM), 0.1),
        'od_w1_v': nrm((N_ODD, CW, CMP_HIDDEN), CW ** -0.5),
        'od_w2_v': nrm((N_ODD, CMP_HIDDEN, HEAD_DIM), CMP_HIDDEN ** -0.5),
        'od_w_out': nrm((N_ODD, C_HEADS * HEAD_DIM, D), BETA * (C_HEADS * HEAD_DIM) ** -0.5),
        'xa_w_q': nrm((DEPTH, D, XW), D ** -0.5),
        'xa_w_k': nrm((DEPTH, D, XW), D ** -0.5),
        'xa_w_v': nrm((DEPTH, D, XW), D ** -0.5),
        'xa_w_o': nrm((DEPTH, XW, D), BETA * XW ** -0.5),
        'ff_w_gate': nrm((N_EVEN, D, D_FF), D ** -0.5),
        'ff_w_up': nrm((N_EVEN, D, D_FF), D ** -0.5),
        'ff_w_down': nrm((N_EVEN, D_FF, D), BETA * D_FF ** -0.5),
        'moe_w_router': nrm((N_ODD, D, N_EXPERTS), D ** -0.5),
        'moe_w_gate': nrm((N_ODD, N_EXPERTS, D, D_FF_EXPERT), D ** -0.5),
        'moe_w_up': nrm((N_ODD, N_EXPERTS, D, D_FF_EXPERT), D ** -0.5),
        'moe_w_down': nrm((N_ODD, N_EXPERTS, D_FF_EXPERT, D), BETA * D_FF_EXPERT ** -0.5),
        'ln_g': 1.0 + nrm((DEPTH, 3, D), 0.02),
        'ln_b': nrm((DEPTH, 3, D), 0.02),
    }


def reference(x, mem, rel_table, ev_w_in, ev_ckv_gain, ev_w_uk, ev_w_uv, ev_sinks, ev_w_out,
              od_w_in, od_pe_k, od_w1_k, od_w2_k, od_pe_v, od_w1_v, od_w2_v, od_w_out,
              xa_w_q, xa_w_k, xa_w_v, xa_w_o, ff_w_gate, ff_w_up, ff_w_down,
              moe_w_router, moe_w_gate, moe_w_up, moe_w_down, ln_g, ln_b):
    h = x
    for i in range(DEPTH):
        j = i // 2
        if i % 2 == 0:
            mix = even_mixer(h, ev_w_in[j], ev_ckv_gain[j], ev_w_uk[j], ev_w_uv[j], ev_sinks[j],
                             ev_w_out[j], rel_table)
        else:
            mix = odd_mixer(h, od_w_in[j], od_pe_k[j], od_w1_k[j], od_w2_k[j], od_pe_v[j],
                            od_w1_v[j], od_w2_v[j], od_w_out[j], rel_table)
        h = layer_norm(ALPHA * h + mix, ln_g[i, 0], ln_b[i, 0])
        xa = cross_attention(h, mem, xa_w_q[i], xa_w_k[i], xa_w_v[i], xa_w_o[i])
        h = layer_norm(ALPHA * h + xa, ln_g[i, 1], ln_b[i, 1])
        if i % 2 == 0:
            ff = dense_swiglu(h, ff_w_gate[j], ff_w_up[j], ff_w_down[j])
        else:
            ff = moe_swiglu(h, moe_w_router[j], moe_w_gate[j], moe_w_up[j], moe_w_down[j])
        h = layer_norm(ALPHA * h + ff, ln_g[i, 2], ln_b[i, 2])
    return h
```

```python
import functools
import math

import numpy as np
import jax
import jax.numpy as jnp
from jax import lax
from jax.experimental import pallas as pl
from jax.experimental.pallas import tpu as pltpu

D_MODEL = 1024
DEPTH = 2
HEAD_DIM = 64
N_SLOTS = D_MODEL // HEAD_DIM
A_HEADS = 8
A_LATENT = 128
IDX_HEADS = 8
IDX_DIM = 64
A_TOPK_MAX = 256
B_HEADS = 8
B_KV_HEADS = 2
B_WINDOW = 128
C_HEADS = 16
C_KV_HEADS = 2
CMP_BLOCK = 32
CMP_STRIDE = 16
CMP_HIDDEN = 128
SLC_BLOCK = 64
SLC_COUNT = 8
C_WINDOW = 512
X_HEADS = 4
X_HEAD_DIM = 128
D_FF = 2816
N_EXPERTS = 8
TOP_K = 2
D_FF_EXPERT = 3584
REL_BUCKETS = 32
REL_MAX_DIST = 128
LN_EPS = 1e-5
NEG_INF = -1e30
FORCE_BONUS = 1e6
ALPHA = (2 * DEPTH) ** 0.25

LANES = 128
QB = 128
VMEM_LIMIT_BYTES = 56 * 1024 * 1024

F32 = jnp.float32
BF16 = jnp.bfloat16


def _cparams(*sem):
    return pltpu.CompilerParams(dimension_semantics=sem, vmem_limit_bytes=VMEM_LIMIT_BYTES)


def _dot(a, b):
    return jnp.dot(a, b, preferred_element_type=F32)


def _dot_nt(a, b):
    return lax.dot_general(a, b, (((1,), (1,)), ((), ())), preferred_element_type=F32)


def _layer_norm_rows(y, g, b):
    mu = jnp.mean(y, axis=-1, keepdims=True)
    yc = y - mu
    var = jnp.mean(yc * yc, axis=-1, keepdims=True)
    return yc * lax.rsqrt(var + LN_EPS) * g + b


def _mm_kernel(x_ref, w_ref, o_ref):
    o_ref[...] = _dot(x_ref[...].astype(BF16), w_ref[...]).astype(o_ref.dtype)


def matmul(x, w, out_dtype, tm=512):
    M, K = x.shape
    N = w.shape[1]
    tm = min(tm, M)
    return pl.pallas_call(
        _mm_kernel,
        out_shape=jax.ShapeDtypeStruct((M, N), out_dtype),
        grid=(M // tm,),
        in_specs=[pl.BlockSpec((tm, K), lambda i: (i, 0)),
                  pl.BlockSpec((K, N), lambda i: (0, 0))],
        out_specs=pl.BlockSpec((tm, N), lambda i: (i, 0)),
        compiler_params=_cparams("parallel"),
        name="matmul",
    )(x, w)


def _mm_res_ln_kernel(*refs, n_in):
    a_refs = refs[:n_in]
    w_ref, h_ref, g_ref, b_ref, o_ref = refs[n_in:]
    a = a_refs[0][...]
    if n_in > 1:
        a = a.astype(F32)
        for r in a_refs[1:]:
            a = a + r[...].astype(F32)
    y = ALPHA * h_ref[...] + _dot(a.astype(BF16), w_ref[...])
    o_ref[...] = _layer_norm_rows(y, g_ref[...], b_ref[...])


def matmul_residual_ln(a_list, w, h, g, b, tm=512):
    M, K = a_list[0].shape
    N = w.shape[1]
    n_in = len(a_list)
    row = lambda i: (i, 0)
    fixed = lambda i: (0, 0)
    return pl.pallas_call(
        functools.partial(_mm_res_ln_kernel, n_in=n_in),
        out_shape=jax.ShapeDtypeStruct((M, N), F32),
        grid=(M // tm,),
        in_specs=[pl.BlockSpec((tm, K), row)] * n_in + [
            pl.BlockSpec((K, N), fixed), pl.BlockSpec((tm, N), row),
            pl.BlockSpec((1, N), fixed), pl.BlockSpec((1, N), fixed)],
        out_specs=pl.BlockSpec((tm, N), row),
        compiler_params=_cparams("parallel"),
        name="matmul_residual_ln",
    )(*a_list, w, h, g.reshape(1, N), b.reshape(1, N))


def _rel_bucket(dist):
    n = jnp.maximum(dist, 0)
    max_exact = REL_BUCKETS // 2
    nf = jnp.maximum(n, 1).astype(F32)
    log_b = max_exact + (jnp.log(nf / max_exact) / math.log(REL_MAX_DIST / max_exact)
                         * (REL_BUCKETS - max_exact)).astype(jnp.int32)
    return jnp.where(n < max_exact, n, jnp.minimum(log_b, REL_BUCKETS - 1))


def _tile_bias(tbl):
    i = np.arange(QB)[:, None]
    j = np.arange(QB)[None, :]
    dist = jnp.asarray(np.stack([i - j, QB + i - j, 2 * QB + i - j]))
    return tbl.astype(F32)[_rel_bucket(dist)].transpose(3, 0, 1, 2)


def _band_bias(tbl, window):
    span = window + QB
    dist = jnp.asarray(np.arange(QB)[:, None] + window - np.arange(span)[None, :])
    return tbl.astype(F32)[_rel_bucket(dist)].transpose(2, 0, 1)


def _band_kernel(*refs, window, kvh, g, has_sinks, has_gate):
    it = iter(refs)
    q_ref, k_ref, v_ref, bias_ref = next(it), next(it), next(it), next(it)
    sink_ref = next(it) if has_sinks else None
    gate_ref = next(it) if has_gate else None
    o_ref = next(it)
    n = pl.program_id(1)
    span = window + QB
    start = pl.multiple_of(n * QB, QB)
    i = lax.broadcasted_iota(jnp.int32, (QB, span), 0)
    j = lax.broadcasted_iota(jnp.int32, (QB, span), 1)
    dist = i + window - j
    ok = (dist >= 0) & (dist < window) & (start - window + j >= 0)
    q = q_ref[...]
    kall = k_ref[0, pl.ds(start, span), :]
    vall = v_ref[0, pl.ds(start, span), :]
    for kv in range(kvh):
        ks = kall[:, kv * HEAD_DIM:(kv + 1) * HEAD_DIM]
        vs = vall[:, kv * HEAD_DIM:(kv + 1) * HEAD_DIM]
        for gi in range(g):
            h = kv * g + gi
            lg = _dot_nt(q[:, h * HEAD_DIM:(h + 1) * HEAD_DIM], ks) * HEAD_DIM ** -0.5 + bias_ref[h]
            lg = jnp.where(ok, lg, NEG_INF)
            m = jnp.max(lg, axis=-1, keepdims=True)
            if has_sinks:
                s = sink_ref[h]
                m = jnp.maximum(m, s[:, :1])
            e = jnp.exp(lg - m)
            den = jnp.sum(e, axis=-1, keepdims=True)
            if has_sinks:
                den = den + jnp.exp(s[:, :1] - m)
            o = _dot(e.astype(BF16), vs) / den
            if has_gate:
                gl = gate_ref[:, h:h + 1]
                o = o * (1.0 / (1.0 + jnp.exp(-gl)))
            o_ref[:, h * HEAD_DIM:(h + 1) * HEAD_DIM] = o.astype(o_ref.dtype)


def banded_attention(proj, q_col, k_col, v_col, n_heads, kvh, bias, window, batch, seq,
                     sinks=None, gates=None, gate_col=0):
    g = n_heads // kvh
    kw = kvh * HEAD_DIM
    qw = n_heads * HEAD_DIM
    nq = seq // QB
    span = window + QB
    k = proj[:, k_col * kw:(k_col + 1) * kw].reshape(batch, seq, kw)
    v = proj[:, v_col * kw:(v_col + 1) * kw].reshape(batch, seq, kw)
    kp = jnp.pad(k, ((0, 0), (window, 0), (0, 0)))
    vp = jnp.pad(v, ((0, 0), (window, 0), (0, 0)))
    args = [proj, kp, vp, bias]
    in_specs = [pl.BlockSpec((QB, qw), lambda b, n: (b * nq + n, q_col)),
                pl.BlockSpec((1, seq + window, kw), lambda b, n: (b, 0, 0)),
                pl.BlockSpec((1, seq + window, kw), lambda b, n: (b, 0, 0)),
                pl.BlockSpec((n_heads, QB, span), lambda b, n: (0, 0, 0))]
    if sinks is not None:
        args.append(jnp.broadcast_to(sinks.astype(F32)[:, None, None], (n_heads, 1, LANES)))
        in_specs.append(pl.BlockSpec((n_heads, 1, LANES), lambda b, n: (0, 0, 0)))
    if gates is not None:
        args.append(gates)
        in_specs.append(pl.BlockSpec((QB, LANES), lambda b, n: (b * nq + n, gate_col)))
    return pl.pallas_call(
        functools.partial(_band_kernel, window=window, kvh=kvh, g=g,
                          has_sinks=sinks is not None, has_gate=gates is not None),
        out_shape=jax.ShapeDtypeStruct((batch * seq, qw), BF16),
        grid=(batch, nq),
        in_specs=in_specs,
        out_specs=pl.BlockSpec((QB, qw), lambda b, n: (b * nq + n, 0)),
        compiler_params=_cparams("parallel", "parallel"),
        name="banded_attention",
    )(*args)


def _sort_key(x):
    bits = lax.bitcast_convert_type(x, jnp.int32)
    return bits ^ ((bits >> 31) & jnp.int32(0x7FFFFFFF))


def _dsa_kernel(qa_ref, qi_ref, ki_ref, ckv_ref, wi_ref, gain_ref, wuk_ref, wuv_ref, bias_ref,
                o_ref, c_s, key_s, qlat_s, m_s, l_s, acc_s, *, topk):
    n = pl.program_id(1)
    nt = n + 1

    @pl.when(n == 0)
    def _():
        c = ckv_ref[...]
        y = c * lax.rsqrt(jnp.mean(c * c, axis=-1, keepdims=True) + LN_EPS)
        c_s[...] = (y * gain_ref[...]).astype(BF16)

    qa = qa_ref[...]
    for h in range(A_HEADS):
        qlat_s[h] = _dot(qa[:, h * HEAD_DIM:(h + 1) * HEAD_DIM], wuk_ref[h]).astype(BF16)

    row = lax.broadcasted_iota(jnp.int32, (QB, QB), 0)
    lane = lax.broadcasted_iota(jnp.int32, (QB, QB), 1)
    qi = qi_ref[...]
    wi = wi_ref[:, :IDX_HEADS] * IDX_HEADS ** -0.5

    def score_tile(kt, carry):
        k_t = ki_ref[pl.ds(pl.multiple_of(kt * QB, QB), QB), :][:, :IDX_DIM]
        sc = jnp.zeros((QB, QB), F32)
        for h in range(IDX_HEADS):
            a = _dot_nt(qi[:, h * IDX_DIM:(h + 1) * IDX_DIM], k_t) * IDX_DIM ** -0.5
            sc = sc + wi[:, h:h + 1] * jnp.maximum(a, 0.0)
        sc = jnp.where(kt * QB + lane <= n * QB + row, sc, NEG_INF)
        sc = jnp.where(sc == 0.0, 0.0, sc)
        key_s[kt] = _sort_key(sc)
        return carry

    lax.fori_loop(0, nt, score_tile, 0)

    def count(pred):
        def body(kt, acc):
            return acc + jnp.where(pred(kt, key_s[kt]), 1.0, 0.0)
        acc = lax.fori_loop(0, nt, body, jnp.zeros((QB, QB), F32))
        return jnp.sum(acc, axis=-1, keepdims=True)

    def thr_step(i, lo):
        cand = lo + (jnp.int32(1) << (31 - i))
        cnt = count(lambda kt, key: key >= cand)
        return jnp.where(cnt >= topk, cand, lo)

    thr = lax.fori_loop(0, 32, thr_step, jnp.full((QB, 1), -2 ** 31, jnp.int32))
    need = topk - count(lambda kt, key: key > thr)

    def tie_step(i, j0):
        cand = j0 + (jnp.int32(1) << (10 - i))
        cnt = count(lambda kt, key: (key == thr) & (kt * QB + lane < cand))
        return jnp.where(cnt < need, cand, j0)

    j0 = lax.fori_loop(0, 11, tie_step, jnp.zeros((QB, 1), jnp.int32))

    m_s[...] = jnp.full(m_s.shape, NEG_INF, F32)
    l_s[...] = jnp.zeros(l_s.shape, F32)
    acc_s[...] = jnp.zeros(acc_s.shape, F32)

    def attend_tile(kt, carry):
        key = key_s[kt]
        kpos = kt * QB + lane
        sel = (key > thr) | ((key == thr) & (kpos <= j0))
        ok = sel & (kpos <= n * QB + row)
        c_t = c_s[pl.ds(pl.multiple_of(kt * QB, QB), QB), :]
        d = jnp.minimum(n - kt, 2)
        for h in range(A_HEADS):
            lg = _dot_nt(qlat_s[h], c_t) * HEAD_DIM ** -0.5 + bias_ref[h, d]
            lg = jnp.where(ok, lg, NEG_INF)
            m_old = m_s[h]
            m_new = jnp.maximum(m_old, jnp.max(lg, axis=-1, keepdims=True))
            a = jnp.exp(m_old - m_new)
            p = jnp.where(ok, jnp.exp(lg - m_new), 0.0)
            l_s[h] = a * l_s[h] + jnp.sum(p, axis=-1, keepdims=True)
            acc_s[h] = a * acc_s[h] + _dot(p.astype(BF16), c_t)
            m_s[h] = m_new
        return carry

    lax.fori_loop(0, nt, attend_tile, 0)

    for h in range(A_HEADS):
        o_lat = acc_s[h] / l_s[h]
        o_ref[:, h * HEAD_DIM:(h + 1) * HEAD_DIM] = _dot(o_lat.astype(BF16), wuv_ref[h]).astype(o_ref.dtype)


def dsa_attention(proj, aux, gain, w_uk, w_uv, tbl, batch, seq):
    nq = seq // QB
    topk = min(A_TOPK_MAX, seq // 4)
    qw = A_HEADS * HEAD_DIM
    return pl.pallas_call(
        functools.partial(_dsa_kernel, topk=topk),
        out_shape=jax.ShapeDtypeStruct((batch * seq, qw), BF16),
        grid=(batch, nq),
        in_specs=[
            pl.BlockSpec((QB, qw), lambda b, n: (b * nq + n, 0)),
            pl.BlockSpec((QB, qw), lambda b, n: (b * nq + n, 1)),
            pl.BlockSpec((seq, LANES), lambda b, n: (b, 12)),
            pl.BlockSpec((seq, A_LATENT), lambda b, n: (b, 0)),
            pl.BlockSpec((QB, LANES), lambda b, n: (b * nq + n, 1)),
            pl.BlockSpec((1, A_LATENT), lambda b, n: (0, 0)),
            pl.BlockSpec((A_HEADS, HEAD_DIM, A_LATENT), lambda b, n: (0, 0, 0)),
            pl.BlockSpec((A_HEADS, A_LATENT, HEAD_DIM), lambda b, n: (0, 0, 0)),
            pl.BlockSpec((A_HEADS, 3, QB, QB), lambda b, n: (0, 0, 0, 0)),
        ],
        out_specs=pl.BlockSpec((QB, qw), lambda b, n: (b * nq + n, 0)),
        scratch_shapes=[
            pltpu.VMEM((seq, A_LATENT), BF16),
            pltpu.VMEM((nq, QB, QB), jnp.int32),
            pltpu.VMEM((A_HEADS, QB, A_LATENT), BF16),
            pltpu.VMEM((A_HEADS, QB, 1), F32),
            pltpu.VMEM((A_HEADS, QB, 1), F32),
            pltpu.VMEM((A_HEADS, QB, A_LATENT), F32),
        ],
        compiler_params=_cparams("parallel", "arbitrary"),
        name="dsa_attention",
    )(proj, proj, proj, aux, aux, gain.reshape(1, A_LATENT).astype(F32),
      w_uk.astype(BF16), w_uv.astype(BF16), _tile_bias(tbl))


def _compress_kernel(x_ref, pe_ref, w1_ref, w2_ref, o_ref):
    half = CMP_STRIDE * HEAD_DIM
    x = x_ref[0]
    w1 = w1_ref[...]
    u = _dot(x, w1[:half])
    v = _dot(x, w1[half:])
    pre = u + pltpu.roll(v, v.shape[0] - 1, 0) + _dot(pe_ref[...], w1)[:1]
    h = 0.5 * pre * (1.0 + jnp.tanh(math.sqrt(2.0 / math.pi) * (pre + 0.044715 * pre * pre * pre)))
    o_ref[0] = _dot(h.astype(BF16), w2_ref[...]).astype(o_ref.dtype)


def compress_blocks(kv, pe, w1, w2, batch, seq):
    G = C_KV_HEADS
    nch = seq // CMP_STRIDE
    half = CMP_STRIDE * HEAD_DIM
    x = kv.reshape(batch, nch, CMP_STRIDE, G, HEAD_DIM).transpose(0, 3, 1, 2, 4)
    x = x.reshape(batch * G, nch, half)
    pe_flat = jnp.broadcast_to(pe.reshape(1, CMP_BLOCK * HEAD_DIM), (8, CMP_BLOCK * HEAD_DIM))
    return pl.pallas_call(
        _compress_kernel,
        out_shape=jax.ShapeDtypeStruct((batch * G, nch, HEAD_DIM), BF16),
        grid=(batch * G,),
        in_specs=[pl.BlockSpec((1, nch, half), lambda i: (i, 0, 0)),
                  pl.BlockSpec((8, 2 * half), lambda i: (0, 0)),
                  pl.BlockSpec((2 * half, CMP_HIDDEN), lambda i: (0, 0)),
                  pl.BlockSpec((CMP_HIDDEN, HEAD_DIM), lambda i: (0, 0))],
        out_specs=pl.BlockSpec((1, nch, HEAD_DIM), lambda i: (i, 0, 0)),
        compiler_params=_cparams("parallel"),
        name="compress_blocks",
    )(x, pe_flat.astype(BF16), w1.astype(BF16), w2.astype(BF16))


def _cmp_kernel(q_ref, kc_ref, vc_ref, bias_ref, gate_ref, ov_ref, o_ref, sel_ref, *, n_cmp, n_slc, n_sel):
    n = pl.program_id(1)
    G = C_KV_HEADS
    hpg = C_HEADS // G
    nch = kc_ref.shape[1]
    t = n * QB + lax.broadcasted_iota(jnp.int32, (QB, nch), 0)
    c = lax.broadcasted_iota(jnp.int32, (QB, nch), 1)
    valid = (t - (c * CMP_STRIDE + CMP_BLOCK - 1) >= 0) & (c < n_cmp)
    q = q_ref[...]
    ts = n * QB + lax.broadcasted_iota(jnp.int32, (QB, n_slc), 0)
    jj = lax.broadcasted_iota(jnp.int32, (QB, n_slc), 1)
    blk = ts // SLC_BLOCK
    forced = (jj == 0) | (jj == blk) | (jj == blk - 1)
    admissible = jj * SLC_BLOCK <= ts
    for g in range(G):
        kc = kc_ref[g]
        vc = vc_ref[g]
        psum = jnp.zeros((QB, nch), F32)
        for hi in range(hpg):
            h = g * hpg + hi
            lg = _dot_nt(q[:, h * HEAD_DIM:(h + 1) * HEAD_DIM], kc) * HEAD_DIM ** -0.5 + bias_ref[h]
            lg = jnp.where(valid, lg, NEG_INF)
            m = jnp.max(lg, axis=-1, keepdims=True)
            e = jnp.where(valid, jnp.exp(lg - m), 0.0)
            den = jnp.sum(e, axis=-1, keepdims=True)
            p = e / jnp.where(den > 0.0, den, 1.0)
            psum = psum + p
            gl = gate_ref[:, h:h + 1]
            o = _dot(p.astype(BF16), vc) * (1.0 / (1.0 + jnp.exp(-gl)))
            o_ref[:, h * HEAD_DIM:(h + 1) * HEAD_DIM] = o.astype(o_ref.dtype)
        p_hi = psum.astype(BF16)
        p_lo = (psum - p_hi.astype(F32)).astype(BF16)
        p_slc = _dot(p_hi, ov_ref[...]) + _dot(p_lo, ov_ref[...])
        score = jnp.where(admissible, p_slc + FORCE_BONUS * jnp.where(forced, 1.0, 0.0), NEG_INF)
        rank = jnp.zeros((QB, n_slc), F32)
        for i in range(n_slc):
            si = score[:, i:i + 1]
            ahead = (si > score) | ((si == score) & (i < jj))
            rank = rank + jnp.where(ahead, 1.0, 0.0)
        sel_ref[0, g] = jnp.where(rank < n_sel, 1.0, 0.0).astype(sel_ref.dtype)


def cmp_attention(proj, kc, vc, gates, tbl, batch, seq):
    nq = seq // QB
    G = C_KV_HEADS
    nch = seq // CMP_STRIDE
    n_cmp = (seq - CMP_BLOCK) // CMP_STRIDE + 1
    n_slc = seq // SLC_BLOCK
    n_sel = min(SLC_COUNT, n_slc)
    qw = C_HEADS * HEAD_DIM
    cmp_end = np.arange(nch) * CMP_STRIDE + CMP_BLOCK - 1
    dist_c = jnp.asarray(np.arange(seq)[:, None] - cmp_end[None, :])
    bias_c = tbl.astype(F32)[_rel_bucket(dist_c)].transpose(2, 0, 1)
    cs = np.arange(nch)[:, None] * CMP_STRIDE
    ss = np.arange(n_slc)[None, :] * SLC_BLOCK
    overlap = ((cs < ss + SLC_BLOCK) & (cs + CMP_BLOCK > ss) & (np.arange(nch)[:, None] < n_cmp))
    return pl.pallas_call(
        functools.partial(_cmp_kernel, n_cmp=n_cmp, n_slc=n_slc, n_sel=n_sel),
        out_shape=(jax.ShapeDtypeStruct((batch * seq, qw), BF16),
                   jax.ShapeDtypeStruct((batch, G, seq, n_slc), BF16)),
        grid=(batch, nq),
        in_specs=[
            pl.BlockSpec((QB, qw), lambda b, n: (b * nq + n, 0)),
            pl.BlockSpec((G, nch, HEAD_DIM), lambda b, n: (b, 0, 0)),
            pl.BlockSpec((G, nch, HEAD_DIM), lambda b, n: (b, 0, 0)),
            pl.BlockSpec((C_HEADS, QB, nch), lambda b, n: (0, n, 0)),
            pl.BlockSpec((QB, LANES), lambda b, n: (b * nq + n, 0)),
            pl.BlockSpec((nch, n_slc), lambda b, n: (0, 0)),
        ],
        out_specs=(pl.BlockSpec((QB, qw), lambda b, n: (b * nq + n, 0)),
                   pl.BlockSpec((1, G, QB, n_slc), lambda b, n: (b, 0, n, 0))),
        compiler_params=_cparams("parallel", "parallel"),
        name="cmp_attention",
    )(proj, kc, vc, bias_c, gates, jnp.asarray(overlap.astype(np.float32)).astype(BF16))


def _slc_kernel(q_ref, k_ref, v_ref, sel_ref, bias_ref, gate_ref, o_ref, m_s, l_s, acc_s, *, n_slc):
    n = pl.program_id(1)
    G = C_KV_HEADS
    hpg = C_HEADS // G
    row = lax.broadcasted_iota(jnp.int32, (QB, QB), 0)
    lane = lax.broadcasted_iota(jnp.int32, (QB, QB), 1)
    eb = lax.broadcasted_iota(jnp.int32, (n_slc, QB), 0)
    ej = lax.broadcasted_iota(jnp.int32, (n_slc, QB), 1)
    q = q_ref[...]
    m_s[...] = jnp.full(m_s.shape, NEG_INF, F32)
    l_s[...] = jnp.zeros(l_s.shape, F32)
    acc_s[...] = jnp.zeros(acc_s.shape, F32)

    def tile(kt, carry):
        off = pl.multiple_of(kt * QB, QB)
        k_t = k_ref[pl.ds(off, QB), :]
        v_t = v_ref[pl.ds(off, QB), :]
        causal = kt * QB + lane <= n * QB + row
        expand = jnp.where(eb == (kt * QB + ej) // SLC_BLOCK, 1.0, 0.0).astype(BF16)
        d = jnp.minimum(n - kt, 2)
        for g in range(G):
            ok = (_dot(sel_ref[0, g], expand) > 0.5) & causal
            ks = k_t[:, g * HEAD_DIM:(g + 1) * HEAD_DIM]
            vs = v_t[:, g * HEAD_DIM:(g + 1) * HEAD_DIM]
            for hi in range(hpg):
                h = g * hpg + hi
                lg = _dot_nt(q[:, h * HEAD_DIM:(h + 1) * HEAD_DIM], ks) * HEAD_DIM ** -0.5 + bias_ref[h, d]
                lg = jnp.where(ok, lg, NEG_INF)
                m_old = m_s[h]
                m_new = jnp.maximum(m_old, jnp.max(lg, axis=-1, keepdims=True))
                a = jnp.exp(m_old - m_new)
                p = jnp.where(ok, jnp.exp(lg - m_new), 0.0)
                l_s[h] = a * l_s[h] + jnp.sum(p, axis=-1, keepdims=True)
                acc_s[h] = a * acc_s[h] + _dot(p.astype(BF16), vs)
                m_s[h] = m_new
        return carry

    lax.fori_loop(0, n + 1, tile, 0)

    for h in range(C_HEADS):
        gl = gate_ref[:, h:h + 1]
        o = acc_s[h] / l_s[h] * (1.0 / (1.0 + jnp.exp(-gl)))
        o_ref[:, h * HEAD_DIM:(h + 1) * HEAD_DIM] = o.astype(o_ref.dtype)


def slc_attention(proj, sel, gates, tbl, batch, seq):
    nq = seq // QB
    G = C_KV_HEADS
    n_slc = seq // SLC_BLOCK
    qw = C_HEADS * HEAD_DIM
    return pl.pallas_call(
        functools.partial(_slc_kernel, n_slc=n_slc),
        out_shape=jax.ShapeDtypeStruct((batch * seq, qw), BF16),
        grid=(batch, nq),
        in_specs=[
            pl.BlockSpec((QB, qw), lambda b, n: (b * nq + n, 0)),
            pl.BlockSpec((seq, LANES), lambda b, n: (b, 10)),
            pl.BlockSpec((seq, LANES), lambda b, n: (b, 11)),
            pl.BlockSpec((1, G, QB, n_slc), lambda b, n: (b, 0, n, 0)),
            pl.BlockSpec((C_HEADS, 3, QB, QB), lambda b, n: (0, 0, 0, 0)),
            pl.BlockSpec((QB, LANES), lambda b, n: (b * nq + n, 1)),
        ],
        out_specs=pl.BlockSpec((QB, qw), lambda b, n: (b * nq + n, 0)),
        scratch_shapes=[pltpu.VMEM((C_HEADS, QB, 1), F32), pltpu.VMEM((C_HEADS, QB, 1), F32),
                        pltpu.VMEM((C_HEADS, QB, HEAD_DIM), F32)],
        compiler_params=_cparams("parallel", "parallel"),
        name="slc_attention",
    )(proj, proj, proj, sel, _tile_bias(tbl), gates)


def _xattn_kernel(h_ref, wq_ref, k_ref, v_ref, wo_ref, g_ref, b_ref, o_ref):
    x = h_ref[...]
    q = _dot(x.astype(BF16), wq_ref[...]).astype(BF16)
    k = k_ref[...]
    v = v_ref[...]
    outs = []
    for hd in range(X_HEADS):
        sl = slice(hd * X_HEAD_DIM, (hd + 1) * X_HEAD_DIM)
        lg = _dot_nt(q[:, sl], k[:, sl]) * X_HEAD_DIM ** -0.5
        e = jnp.exp(lg - jnp.max(lg, axis=-1, keepdims=True))
        den = jnp.sum(e, axis=-1, keepdims=True)
        outs.append((_dot(e.astype(BF16), v[:, sl]) / den).astype(BF16))
    o = jnp.concatenate(outs, axis=-1)
    y = ALPHA * x + _dot(o, wo_ref[...])
    o_ref[...] = _layer_norm_rows(y, g_ref[...], b_ref[...])


def cross_attention_ln(h, mem, w_q, w_k, w_v, w_o, g, b, batch, seq, tm=256):
    M = mem.shape[0] // batch
    XW = X_HEADS * X_HEAD_DIM
    D = D_MODEL
    kv = matmul(mem, jnp.concatenate([w_k, w_v], axis=1).astype(BF16), BF16, tm=512)
    nt = seq // tm
    fixed = lambda bb, i: (0, 0)
    return pl.pallas_call(
        _xattn_kernel,
        out_shape=jax.ShapeDtypeStruct((batch * seq, D), F32),
        grid=(batch, nt),
        in_specs=[pl.BlockSpec((tm, D), lambda bb, i: (bb * nt + i, 0)),
                  pl.BlockSpec((D, XW), fixed),
                  pl.BlockSpec((M, XW), lambda bb, i: (bb, 0)),
                  pl.BlockSpec((M, XW), lambda bb, i: (bb, 1)),
                  pl.BlockSpec((XW, D), fixed),
                  pl.BlockSpec((1, D), fixed), pl.BlockSpec((1, D), fixed)],
        out_specs=pl.BlockSpec((tm, D), lambda bb, i: (bb * nt + i, 0)),
        compiler_params=_cparams("parallel", "parallel"),
        name="cross_attention_ln",
    )(h, w_q.astype(BF16), kv, kv, w_o.astype(BF16), g.reshape(1, D), b.reshape(1, D))


def _silu(a):
    return a * (1.0 / (1.0 + jnp.exp(-a)))


def _swiglu_kernel(x_ref, wg_ref, wu_ref, wd_ref, g_ref, b_ref, o_ref):
    x = x_ref[...]
    xb = x.astype(BF16)
    hmid = (_silu(_dot(xb, wg_ref[...])) * _dot(xb, wu_ref[...])).astype(BF16)
    y = ALPHA * x + _dot(hmid, wd_ref[...])
    o_ref[...] = _layer_norm_rows(y, g_ref[...], b_ref[...])


def swiglu_ln(h, w_gate, w_up, w_down, g, b, tm=256):
    T, D = h.shape
    F = w_gate.shape[1]
    fixed = lambda i: (0, 0)
    once = pl.Buffered(1)
    return pl.pallas_call(
        _swiglu_kernel,
        out_shape=jax.ShapeDtypeStruct((T, D), F32),
        grid=(T // tm,),
        in_specs=[pl.BlockSpec((tm, D), lambda i: (i, 0)),
                  pl.BlockSpec((D, F), fixed, pipeline_mode=once),
                  pl.BlockSpec((D, F), fixed, pipeline_mode=once),
                  pl.BlockSpec((F, D), fixed, pipeline_mode=once),
                  pl.BlockSpec((1, D), fixed), pl.BlockSpec((1, D), fixed)],
        out_specs=pl.BlockSpec((tm, D), lambda i: (i, 0)),
        compiler_params=_cparams("parallel"),
        name="swiglu_ln",
    )(h, w_gate.astype(BF16), w_up.astype(BF16), w_down.astype(BF16), g.reshape(1, D), b.reshape(1, D))


def _router_kernel(x_ref, w_ref, o_ref):
    x = x_ref[...]
    w = w_ref[...]
    x_hi = x.astype(BF16)
    x_lo = (x - x_hi.astype(F32)).astype(BF16)
    w_hi = w.astype(BF16)
    w_lo = (w - w_hi.astype(F32)).astype(BF16)
    lg = _dot(x_hi, w_hi) + (_dot(x_hi, w_lo) + _dot(x_lo, w_hi))
    e_iota = lax.broadcasted_iota(jnp.int32, lg.shape, 1).astype(F32)
    m1 = jnp.max(lg, axis=-1, keepdims=True)
    i1 = jnp.min(jnp.where(lg == m1, e_iota, float(N_EXPERTS)), axis=-1, keepdims=True)
    first = e_iota == i1
    rest = jnp.where(first, -jnp.inf, lg)
    m2 = jnp.max(rest, axis=-1, keepdims=True)
    i2 = jnp.min(jnp.where(rest == m2, e_iota, float(N_EXPERTS)), axis=-1, keepdims=True)
    second = e_iota == i2
    w2 = jnp.exp(m2 - m1)
    den = 1.0 + w2
    o_ref[...] = jnp.where(first, 1.0 / den, 0.0) + jnp.where(second, w2 / den, 0.0)


def router_gates(h, w_router, tm=1024):
    T, D = h.shape
    E = w_router.shape[1]
    return pl.pallas_call(
        _router_kernel,
        out_shape=jax.ShapeDtypeStruct((T, E), F32),
        grid=(T // tm,),
        in_specs=[pl.BlockSpec((tm, D), lambda i: (i, 0)), pl.BlockSpec((D, E), lambda i: (0, 0))],
        out_specs=pl.BlockSpec((tm, E), lambda i: (i, 0)),
        compiler_params=_cparams("parallel"),
        name="router_gates",
    )(h, w_router)


def _moe_kernel(x_ref, gate_ref, wg_ref, wu_ref, wd_ref, g_ref, b_ref, o_ref, acc_s):
    e = pl.program_id(1)
    c = pl.program_id(2)

    @pl.when((e == 0) & (c == 0))
    def _():
        acc_s[...] = jnp.zeros(acc_s.shape, F32)

    xb = x_ref[...].astype(BF16)
    gates = gate_ref[...]
    e_iota = lax.broadcasted_iota(jnp.int32, gates.shape, 1)
    ge = jnp.sum(jnp.where(e_iota == e, gates, 0.0), axis=-1, keepdims=True)
    hmid = _silu(_dot(xb, wg_ref[0])) * _dot(xb, wu_ref[0]) * ge
    acc_s[...] += _dot(hmid.astype(BF16), wd_ref[0])

    @pl.when((e == pl.num_programs(1) - 1) & (c == pl.num_programs(2) - 1))
    def _():
        y = ALPHA * x_ref[...] + acc_s[...]
        o_ref[...] = _layer_norm_rows(y, g_ref[...], b_ref[...])


def moe_ln(h, gates, w_gate, w_up, w_down, g, b, tm=1024, fc=896):
    T, D = h.shape
    E, _, F = w_gate.shape
    nfc = F // fc
    fixed = lambda i, e, c: (0, 0)
    return pl.pallas_call(
        _moe_kernel,
        out_shape=jax.ShapeDtypeStruct((T, D), F32),
        grid=(T // tm, E, nfc),
        in_specs=[pl.BlockSpec((tm, D), lambda i, e, c: (i, 0)),
                  pl.BlockSpec((tm, E), lambda i, e, c: (i, 0)),
                  pl.BlockSpec((1, D, fc), lambda i, e, c: (e, 0, c)),
                  pl.BlockSpec((1, D, fc), lambda i, e, c: (e, 0, c)),
                  pl.BlockSpec((1, fc, D), lambda i, e, c: (e, c, 0)),
                  pl.BlockSpec((1, D), fixed), pl.BlockSpec((1, D), fixed)],
        out_specs=pl.BlockSpec((tm, D), lambda i, e, c: (i, 0)),
        scratch_shapes=[pltpu.VMEM((tm, D), F32)],
        compiler_params=_cparams("parallel", "arbitrary", "arbitrary"),
        name="moe_ln",
    )(h, gates, w_gate.astype(BF16), w_up.astype(BF16), w_down.astype(BF16),
      g.reshape(1, D), b.reshape(1, D))


def _pad_cols(w, width):
    return jnp.pad(w, ((0, 0), (0, width - w.shape[1])))


def even_layer_mixer(h, w_in, ckv_gain, w_uk, w_uv, sinks, w_out, rel_table, ln_g, ln_b, batch, seq):
    cuts = np.cumsum((A_HEADS * HEAD_DIM, A_LATENT, IDX_HEADS * IDX_DIM, IDX_DIM, IDX_HEADS,
                      B_HEADS * HEAD_DIM, B_KV_HEADS * HEAD_DIM))
    w_qa, w_ckv, w_qi, w_ki, w_wi, w_qb, w_kb, w_vb = jnp.split(w_in, [int(c) for c in cuts], axis=1)
    w_main = jnp.concatenate([w_qa, w_qi, w_qb, _pad_cols(w_ki, LANES), w_kb, w_vb], axis=1).astype(BF16)
    w_aux = jnp.concatenate([w_ckv, _pad_cols(w_wi, LANES)], axis=1).astype(BF16)
    proj = matmul(h, w_main, BF16)
    aux = matmul(h, w_aux, F32)
    o_a = dsa_attention(proj, aux, ckv_gain, w_uk, w_uv, rel_table[:, :A_HEADS], batch, seq)
    o_b = banded_attention(proj, 2, 13, 14, B_HEADS, B_KV_HEADS,
                           _band_bias(rel_table[:, A_HEADS:A_HEADS + B_HEADS], B_WINDOW), B_WINDOW,
                           batch, seq, sinks=sinks)
    o = jnp.concatenate([o_a, o_b], axis=1)
    return matmul_residual_ln([o], w_out.astype(BF16), h, ln_g, ln_b)


def odd_layer_mixer(h, w_in, pe_k, w1_k, w2_k, pe_v, w1_v, w2_v, w_out, rel_table, ln_g, ln_b, batch, seq):
    kvw = C_KV_HEADS * HEAD_DIM
    qw = C_HEADS * HEAD_DIM
    w_main = w_in[:, :qw + 6 * kvw].astype(BF16)
    w_gl = w_in[:, qw + 6 * kvw:].reshape(D_MODEL, C_HEADS, 3)
    w_gate = jnp.concatenate([_pad_cols(w_gl[:, :, r], LANES) for r in range(3)], axis=1).astype(BF16)
    proj = matmul(h, w_main, BF16)
    gates = matmul(h, w_gate, F32)
    tbl = rel_table[:, :C_HEADS]
    kc = compress_blocks(proj[:, qw:qw + kvw], pe_k, w1_k, w2_k, batch, seq)
    vc = compress_blocks(proj[:, qw + kvw:qw + 2 * kvw], pe_v, w1_v, w2_v, batch, seq)
    o_cmp, sel = cmp_attention(proj, kc, vc, gates, tbl, batch, seq)
    o_slc = slc_attention(proj, sel, gates, tbl, batch, seq)
    o_win = banded_attention(proj, 0, 12, 13, C_HEADS, C_KV_HEADS, _band_bias(tbl, C_WINDOW), C_WINDOW,
                             batch, seq, gates=gates, gate_col=2)
    return matmul_residual_ln([o_cmp, o_slc, o_win], w_out.astype(BF16), h, ln_g, ln_b)


def kernel(x, mem, rel_table, ev_w_in, ev_ckv_gain, ev_w_uk, ev_w_uv, ev_sinks, ev_w_out, od_w_in, od_pe_k, od_w1_k, od_w2_k, od_pe_v, od_w1_v, od_w2_v, od_w_out, xa_w_q, xa_w_k, xa_w_v, xa_w_o, ff_w_gate, ff_w_up, ff_w_down, moe_w_router, moe_w_gate, moe_w_up, moe_w_down, ln_g, ln_b):
    batch, seq, D = x.shape
    h = x.reshape(batch * seq, D)
    mem2 = mem.reshape(-1, D)
    for i in range(DEPTH):
        j = i // 2
        if i % 2 == 0:
            h = even_layer_mixer(h, ev_w_in[j], ev_ckv_gain[j], ev_w_uk[j], ev_w_uv[j], ev_sinks[j],
                                 ev_w_out[j], rel_table, ln_g[i, 0], ln_b[i, 0], batch, seq)
        else:
            h = odd_layer_mixer(h, od_w_in[j], od_pe_k[j], od_w1_k[j], od_w2_k[j], od_pe_v[j],
                                od_w1_v[j], od_w2_v[j], od_w_out[j], rel_table, ln_g[i, 0], ln_b[i, 0],
                                batch, seq)
        h = cross_attention_ln(h, mem2, xa_w_q[i], xa_w_k[i], xa_w_v[i], xa_w_o[i],
                               ln_g[i, 1], ln_b[i, 1], batch, seq)
        if i % 2 == 0:
            h = swiglu_ln(h, ff_w_gate[j], ff_w_up[j], ff_w_down[j], ln_g[i, 2], ln_b[i, 2])
        else:
            gates = router_gates(h, moe_w_router[j])
            h = moe_ln(h, gates, moe_w_gate[j], moe_w_up[j], moe_w_down[j], ln_g[i, 2], ln_b[i, 2])
    return h.reshape(batch, seq, D)
```

```python
import functools
import math

import numpy as np
import jax
import jax.numpy as jnp
from jax import lax
from jax.experimental import pallas as pl
from jax.experimental.pallas import tpu as pltpu

D_MODEL = 1024
DEPTH = 2
HEAD_DIM = 64
N_SLOTS = D_MODEL // HEAD_DIM
A_HEADS = 8
A_LATENT = 128
IDX_HEADS = 8
IDX_DIM = 64
A_TOPK_MAX = 256
B_HEADS = 8
B_KV_HEADS = 2
B_WINDOW = 128
C_HEADS = 16
C_KV_HEADS = 2
CMP_BLOCK = 32
CMP_STRIDE = 16
CMP_HIDDEN = 128
SLC_BLOCK = 64
SLC_COUNT = 8
C_WINDOW = 512
X_HEADS = 4
X_HEAD_DIM = 128
D_FF = 2816
N_EXPERTS = 8
TOP_K = 2
D_FF_EXPERT = 3584
REL_BUCKETS = 32
REL_MAX_DIST = 128
LN_EPS = 1e-5
NEG_INF = -1e30
FORCE_BONUS = 1e6
ALPHA = (2 * DEPTH) ** 0.25

LANES = 128
QB = 128
VMEM_LIMIT_BYTES = 56 * 1024 * 1024

F32 = jnp.float32
BF16 = jnp.bfloat16


def _cparams(*sem):
    return pltpu.CompilerParams(dimension_semantics=sem, vmem_limit_bytes=VMEM_LIMIT_BYTES)


def _dot(a, b):
    return jnp.dot(a, b, preferred_element_type=F32)


def _dot_nt(a, b):
    return lax.dot_general(a, b, (((1,), (1,)), ((), ())), preferred_element_type=F32)


def _layer_norm_rows(y, g, b):
    mu = jnp.mean(y, axis=-1, keepdims=True)
    yc = y - mu
    var = jnp.mean(yc * yc, axis=-1, keepdims=True)
    return yc * lax.rsqrt(var + LN_EPS) * g + b


def _mm_kernel(x_ref, w_ref, o_ref):
    o_ref[...] = _dot(x_ref[...].astype(BF16), w_ref[...]).astype(o_ref.dtype)


def matmul(x, w, out_dtype, tm=512):
    M, K = x.shape
    N = w.shape[1]
    tm = min(tm, M)
    return pl.pallas_call(
        _mm_kernel,
        out_shape=jax.ShapeDtypeStruct((M, N), out_dtype),
        grid=(M // tm,),
        in_specs=[pl.BlockSpec((tm, K), lambda i: (i, 0)),
                  pl.BlockSpec((K, N), lambda i: (0, 0))],
        out_specs=pl.BlockSpec((tm, N), lambda i: (i, 0)),
        compiler_params=_cparams("parallel"),
        name="matmul",
    )(x, w)


def _mm_res_ln_kernel(*refs, n_in):
    a_refs = refs[:n_in]
    w_ref, h_ref, g_ref, b_ref, o_ref = refs[n_in:]
    a = a_refs[0][...]
    if n_in > 1:
        a = a.astype(F32)
        for r in a_refs[1:]:
            a = a + r[...].astype(F32)
    y = ALPHA * h_ref[...] + _dot(a.astype(BF16), w_ref[...])
    o_ref[...] = _layer_norm_rows(y, g_ref[...], b_ref[...])


def matmul_residual_ln(a_list, w, h, g, b, tm=512):
    M, K = a_list[0].shape
    N = w.shape[1]
    n_in = len(a_list)
    row = lambda i: (i, 0)
    fixed = lambda i: (0, 0)
    return pl.pallas_call(
        functools.partial(_mm_res_ln_kernel, n_in=n_in),
        out_shape=jax.ShapeDtypeStruct((M, N), F32),
        grid=(M // tm,),
        in_specs=[pl.BlockSpec((tm, K), row)] * n_in + [
            pl.BlockSpec((K, N), fixed), pl.BlockSpec((tm, N), row),
            pl.BlockSpec((1, N), fixed), pl.BlockSpec((1, N), fixed)],
        out_specs=pl.BlockSpec((tm, N), row),
        compiler_params=_cparams("parallel"),
        name="matmul_residual_ln",
    )(*a_list, w, h, g.reshape(1, N), b.reshape(1, N))


def _rel_bucket(dist):
    n = jnp.maximum(dist, 0)
    max_exact = REL_BUCKETS // 2
    nf = jnp.maximum(n, 1).astype(F32)
    log_b = max_exact + (jnp.log(nf / max_exact) / math.log(REL_MAX_DIST / max_exact)
                         * (REL_BUCKETS - max_exact)).astype(jnp.int32)
    return jnp.where(n < max_exact, n, jnp.minimum(log_b, REL_BUCKETS - 1))


def _tile_bias(tbl):
    i = np.arange(QB)[:, None]
    j = np.arange(QB)[None, :]
    dist = jnp.asarray(np.stack([i - j, QB + i - j, 2 * QB + i - j]))
    return tbl.astype(F32)[_rel_bucket(dist)].transpose(3, 0, 1, 2)


def _band_bias(tbl, window):
    span = window + QB
    dist = jnp.asarray(np.arange(QB)[:, None] + window - np.arange(span)[None, :])
    return tbl.astype(F32)[_rel_bucket(dist)].transpose(2, 0, 1)


def _band_kernel(*refs, window, kvh, g, has_sinks, has_gate):
    it = iter(refs)
    q_ref, k_ref, v_ref, bias_ref = next(it), next(it), next(it), next(it)
    sink_ref = next(it) if has_sinks else None
    gate_ref = next(it) if has_gate else None
    o_ref = next(it)
    n = pl.program_id(1)
    span = window + QB
    start = pl.multiple_of(n * QB, QB)
    i = lax.broadcasted_iota(jnp.int32, (QB, span), 0)
    j = lax.broadcasted_iota(jnp.int32, (QB, span), 1)
    dist = i + window - j
    ok = (dist >= 0) & (dist < window) & (start - window + j >= 0)
    q = q_ref[...]
    kall = k_ref[0, pl.ds(start, span), :]
    vall = v_ref[0, pl.ds(start, span), :]
    for kv in range(kvh):
        ks = kall[:, kv * HEAD_DIM:(kv + 1) * HEAD_DIM]
        vs = vall[:, kv * HEAD_DIM:(kv + 1) * HEAD_DIM]
        for gi in range(g):
            h = kv * g + gi
            lg = _dot_nt(q[:, h * HEAD_DIM:(h + 1) * HEAD_DIM], ks) * HEAD_DIM ** -0.5 + bias_ref[h]
            lg = jnp.where(ok, lg, NEG_INF)
            m = jnp.max(lg, axis=-1, keepdims=True)
            if has_sinks:
                s = sink_ref[h]
                m = jnp.maximum(m, s[:, :1])
            e = jnp.exp(lg - m)
            den = jnp.sum(e, axis=-1, keepdims=True)
            if has_sinks:
                den = den + jnp.exp(s[:, :1] - m)
            o = _dot(e.astype(BF16), vs) / den
            if has_gate:
                gl = gate_ref[:, h:h + 1]
                o = o * (1.0 / (1.0 + jnp.exp(-gl)))
            o_ref[:, h * HEAD_DIM:(h + 1) * HEAD_DIM] = o.astype(o_ref.dtype)


def banded_attention(proj, q_col, k_col, v_col, n_heads, kvh, bias, window, batch, seq,
                     sinks=None, gates=None, gate_col=0):
    g = n_heads // kvh
    kw = kvh * HEAD_DIM
    qw = n_heads * HEAD_DIM
    nq = seq // QB
    span = window + QB
    k = proj[:, k_col * kw:(k_col + 1) * kw].reshape(batch, seq, kw)
    v = proj[:, v_col * kw:(v_col + 1) * kw].reshape(batch, seq, kw)
    kp = jnp.pad(k, ((0, 0), (window, 0), (0, 0)))
    vp = jnp.pad(v, ((0, 0), (window, 0), (0, 0)))
    args = [proj, kp, vp, bias]
    in_specs = [pl.BlockSpec((QB, qw), lambda b, n: (b * nq + n, q_col)),
                pl.BlockSpec((1, seq + window, kw), lambda b, n: (b, 0, 0)),
                pl.BlockSpec((1, seq + window, kw), lambda b, n: (b, 0, 0)),
                pl.BlockSpec((n_heads, QB, span), lambda b, n: (0, 0, 0))]
    if sinks is not None:
        args.append(jnp.broadcast_to(sinks.astype(F32)[:, None, None], (n_heads, 1, LANES)))
        in_specs.append(pl.BlockSpec((n_heads, 1, LANES), lambda b, n: (0, 0, 0)))
    if gates is not None:
        args.append(gates)
        in_specs.append(pl.BlockSpec((QB, LANES), lambda b, n: (b * nq + n, gate_col)))
    return pl.pallas_call(
        functools.partial(_band_kernel, window=window, kvh=kvh, g=g,
                          has_sinks=sinks is not None, has_gate=gates is not None),
        out_shape=jax.ShapeDtypeStruct((batch * seq, qw), BF16),
        grid=(batch, nq),
        in_specs=in_specs,
        out_specs=pl.BlockSpec((QB, qw), lambda b, n: (b * nq + n, 0)),
        compiler_params=_cparams("parallel", "parallel"),
        name="banded_attention",
    )(*args)


def _sort_key(x):
    bits = lax.bitcast_convert_type(x, jnp.int32)
    return bits ^ ((bits >> 31) & jnp.int32(0x7FFFFFFF))


def _dsa_kernel(qa_ref, qi_ref, ki_ref, ckv_ref, wi_ref, gain_ref, wuk_ref, wuv_ref, bias_ref,
                o_ref, c_s, ct_s, key_s, m_s, l_s, acc_s, *, topk):
    n = pl.program_id(1)
    nt = n + 1
    nq = c_s.shape[0]

    @pl.when(n == 0)
    def _():
        for kt in range(nq):
            c = ckv_ref[kt * QB:(kt + 1) * QB, :]
            y = c * lax.rsqrt(jnp.mean(c * c, axis=-1, keepdims=True) + LN_EPS) * gain_ref[...]
            c_s[kt] = y.astype(BF16)
            ct_s[kt] = y.T.astype(BF16)

    qa = qa_ref[...]
    qi = qi_ref[...]
    qlat = jnp.concatenate(
        [(_dot(qa[:, h * HEAD_DIM:(h + 1) * HEAD_DIM], wuk_ref[h]) * HEAD_DIM ** -0.5).astype(BF16)
         for h in range(A_HEADS)], axis=0)
    qidx = jnp.concatenate(
        [qi[:, h * IDX_DIM:(h + 1) * IDX_DIM] * IDX_DIM ** -0.5 for h in range(IDX_HEADS)], axis=0)
    wi_t = wi_ref[...].T
    wi_row = jnp.concatenate([wi_t[h:h + 1, :] for h in range(IDX_HEADS)], axis=1) * IDX_HEADS ** -0.5

    kpos0 = lax.broadcasted_iota(jnp.int32, (QB, QB), 0)
    qpos = n * QB + lax.broadcasted_iota(jnp.int32, (QB, QB), 1)

    def score_tile(kt, carry):
        k_t = ki_ref[pl.ds(pl.multiple_of(kt * QB, QB), QB), :][:, :IDX_DIM]
        act = jnp.maximum(_dot_nt(k_t, qidx), 0.0) * wi_row
        sc = act[:, :QB]
        for h in range(1, IDX_HEADS):
            sc = sc + act[:, h * QB:(h + 1) * QB]
        sc = jnp.where(kt * QB + kpos0 <= qpos, sc, NEG_INF)
        sc = jnp.where(sc == 0.0, 0.0, sc)
        key_s[kt] = _sort_key(sc)
        return carry

    lax.fori_loop(0, nt, score_tile, 0)

    def count(pred):
        def body(kt, acc):
            return acc + jnp.where(pred(kt, key_s[kt]), 1.0, 0.0)
        acc = lax.fori_loop(0, nt, body, jnp.zeros((QB, QB), F32))
        return jnp.sum(acc, axis=0, keepdims=True)

    def thr_step(i, lo):
        cand = lo + (jnp.int32(1) << (31 - i))
        cnt = count(lambda kt, key: key >= cand)
        return jnp.where(cnt >= topk, cand, lo)

    thr = lax.fori_loop(0, 32, thr_step, jnp.full((1, QB), -2 ** 31, jnp.int32))
    need = topk - count(lambda kt, key: key > thr)

    def tie_step(i, j0):
        cand = j0 + (jnp.int32(1) << (10 - i))
        cnt = count(lambda kt, key: (key == thr) & (kt * QB + kpos0 < cand))
        return jnp.where(cnt < need, cand, j0)

    j0 = lax.fori_loop(0, 11, tie_step, jnp.zeros((1, QB), jnp.int32))

    m_s[...] = jnp.full(m_s.shape, NEG_INF, F32)
    l_s[...] = jnp.zeros(l_s.shape, F32)
    acc_s[...] = jnp.zeros(acc_s.shape, F32)

    def attend_tile(kt, carry):
        key = key_s[kt]
        kpos = kt * QB + kpos0
        ok = ((key > thr) | ((key == thr) & (kpos <= j0))) & (kpos <= qpos)
        d = jnp.minimum(n - kt, 2)
        s = _dot_nt(c_s[kt], qlat)
        scale, probs = [], []
        for h in range(A_HEADS):
            sl = slice(h * QB, (h + 1) * QB)
            sh = jnp.where(ok, s[:, sl] + bias_ref[h, d], NEG_INF)
            m_old = m_s[:, sl]
            m_new = jnp.maximum(m_old, jnp.max(sh, axis=0, keepdims=True))
            a = jnp.exp(m_old - m_new)
            p = jnp.where(ok, jnp.exp(sh - m_new), 0.0)
            l_s[:, sl] = a * l_s[:, sl] + jnp.sum(p, axis=0, keepdims=True)
            m_s[:, sl] = m_new
            scale.append(a)
            probs.append(p.astype(BF16))
        pv = _dot(ct_s[kt], jnp.concatenate(probs, axis=1))
        for h in range(A_HEADS):
            sl = slice(h * QB, (h + 1) * QB)
            acc_s[:, sl] = scale[h] * acc_s[:, sl] + pv[:, sl]
        return carry

    lax.fori_loop(0, nt, attend_tile, 0)

    for h in range(A_HEADS):
        sl = slice(h * QB, (h + 1) * QB)
        o_lat = (acc_s[:, sl] / l_s[:, sl]).T
        o_ref[:, h * HEAD_DIM:(h + 1) * HEAD_DIM] = _dot(o_lat.astype(BF16), wuv_ref[h]).astype(o_ref.dtype)


def dsa_attention(proj, aux, gain, w_uk, w_uv, tbl, batch, seq):
    nq = seq // QB
    topk = min(A_TOPK_MAX, seq // 4)
    qw = A_HEADS * HEAD_DIM
    return pl.pallas_call(
        functools.partial(_dsa_kernel, topk=topk),
        out_shape=jax.ShapeDtypeStruct((batch * seq, qw), BF16),
        grid=(batch, nq),
        in_specs=[
            pl.BlockSpec((QB, qw), lambda b, n: (b * nq + n, 0)),
            pl.BlockSpec((QB, qw), lambda b, n: (b * nq + n, 1)),
            pl.BlockSpec((seq, LANES), lambda b, n: (b, 12)),
            pl.BlockSpec((seq, A_LATENT), lambda b, n: (b, 0)),
            pl.BlockSpec((QB, LANES), lambda b, n: (b * nq + n, 1)),
            pl.BlockSpec((1, A_LATENT), lambda b, n: (0, 0)),
            pl.BlockSpec((A_HEADS, HEAD_DIM, A_LATENT), lambda b, n: (0, 0, 0)),
            pl.BlockSpec((A_HEADS, A_LATENT, HEAD_DIM), lambda b, n: (0, 0, 0)),
            pl.BlockSpec((A_HEADS, 3, QB, QB), lambda b, n: (0, 0, 0, 0)),
        ],
        out_specs=pl.BlockSpec((QB, qw), lambda b, n: (b * nq + n, 0)),
        scratch_shapes=[
            pltpu.VMEM((nq, QB, A_LATENT), BF16),
            pltpu.VMEM((nq, A_LATENT, QB), BF16),
            pltpu.VMEM((nq, QB, QB), jnp.int32),
            pltpu.VMEM((1, A_HEADS * QB), F32),
            pltpu.VMEM((1, A_HEADS * QB), F32),
            pltpu.VMEM((A_LATENT, A_HEADS * QB), F32),
        ],
        compiler_params=_cparams("parallel", "arbitrary"),
        name="dsa_attention",
    )(proj, proj, proj, aux, aux, gain.reshape(1, A_LATENT).astype(F32),
      w_uk.astype(BF16), w_uv.astype(BF16), _tile_bias(tbl).swapaxes(2, 3))


def _compress_kernel(x_ref, pe_ref, w1_ref, w2_ref, o_ref):
    half = CMP_STRIDE * HEAD_DIM
    x = x_ref[0]
    w1 = w1_ref[...]
    u = _dot(x, w1[:half])
    v = _dot(x, w1[half:])
    pre = u + pltpu.roll(v, v.shape[0] - 1, 0) + _dot(pe_ref[...], w1)[:1]
    h = 0.5 * pre * (1.0 + jnp.tanh(math.sqrt(2.0 / math.pi) * (pre + 0.044715 * pre * pre * pre)))
    o_ref[0] = _dot(h.astype(BF16), w2_ref[...]).astype(o_ref.dtype)


def compress_blocks(kv, pe, w1, w2, batch, seq):
    G = C_KV_HEADS
    nch = seq // CMP_STRIDE
    half = CMP_STRIDE * HEAD_DIM
    x = kv.reshape(batch, nch, CMP_STRIDE, G, HEAD_DIM).transpose(0, 3, 1, 2, 4)
    x = x.reshape(batch * G, nch, half)
    pe_flat = jnp.broadcast_to(pe.reshape(1, CMP_BLOCK * HEAD_DIM), (8, CMP_BLOCK * HEAD_DIM))
    return pl.pallas_call(
        _compress_kernel,
        out_shape=jax.ShapeDtypeStruct((batch * G, nch, HEAD_DIM), BF16),
        grid=(batch * G,),
        in_specs=[pl.BlockSpec((1, nch, half), lambda i: (i, 0, 0)),
                  pl.BlockSpec((8, 2 * half), lambda i: (0, 0)),
                  pl.BlockSpec((2 * half, CMP_HIDDEN), lambda i: (0, 0)),
                  pl.BlockSpec((CMP_HIDDEN, HEAD_DIM), lambda i: (0, 0))],
        out_specs=pl.BlockSpec((1, nch, HEAD_DIM), lambda i: (i, 0, 0)),
        compiler_params=_cparams("parallel"),
        name="compress_blocks",
    )(x, pe_flat.astype(BF16), w1.astype(BF16), w2.astype(BF16))


def _cmp_kernel(q_ref, kc_ref, vc_ref, bias_ref, gate_ref, ov_ref, o_ref, sel_ref, *, n_cmp, n_slc, n_sel):
    n = pl.program_id(1)
    G = C_KV_HEADS
    hpg = C_HEADS // G
    nch = kc_ref.shape[1]
    t = n * QB + lax.broadcasted_iota(jnp.int32, (QB, nch), 0)
    c = lax.broadcasted_iota(jnp.int32, (QB, nch), 1)
    valid = (t - (c * CMP_STRIDE + CMP_BLOCK - 1) >= 0) & (c < n_cmp)
    q = q_ref[...]
    ts = n * QB + lax.broadcasted_iota(jnp.int32, (QB, n_slc), 0)
    jj = lax.broadcasted_iota(jnp.int32, (QB, n_slc), 1)
    blk = ts // SLC_BLOCK
    forced = (jj == 0) | (jj == blk) | (jj == blk - 1)
    admissible = jj * SLC_BLOCK <= ts
    for g in range(G):
        kc = kc_ref[g]
        vc = vc_ref[g]
        psum = jnp.zeros((QB, nch), F32)
        for hi in range(hpg):
            h = g * hpg + hi
            lg = _dot_nt(q[:, h * HEAD_DIM:(h + 1) * HEAD_DIM], kc) * HEAD_DIM ** -0.5 + bias_ref[h]
            lg = jnp.where(valid, lg, NEG_INF)
            m = jnp.max(lg, axis=-1, keepdims=True)
            e = jnp.where(valid, jnp.exp(lg - m), 0.0)
            den = jnp.sum(e, axis=-1, keepdims=True)
            p = e / jnp.where(den > 0.0, den, 1.0)
            psum = psum + p
            gl = gate_ref[:, h:h + 1]
            o = _dot(p.astype(BF16), vc) * (1.0 / (1.0 + jnp.exp(-gl)))
            o_ref[:, h * HEAD_DIM:(h + 1) * HEAD_DIM] = o.astype(o_ref.dtype)
        p_hi = psum.astype(BF16)
        p_lo = (psum - p_hi.astype(F32)).astype(BF16)
        p_slc = _dot(p_hi, ov_ref[...]) + _dot(p_lo, ov_ref[...])
        score = jnp.where(admissible, p_slc + FORCE_BONUS * jnp.where(forced, 1.0, 0.0), NEG_INF)
        rank = jnp.zeros((QB, n_slc), F32)
        for i in range(n_slc):
            si = score[:, i:i + 1]
            ahead = (si > score) | ((si == score) & (i < jj))
            rank = rank + jnp.where(ahead, 1.0, 0.0)
        sel_ref[0, g] = jnp.where(rank < n_sel, 1.0, 0.0).astype(sel_ref.dtype)


def cmp_attention(proj, kc, vc, gates, tbl, batch, seq):
    nq = seq // QB
    G = C_KV_HEADS
    nch = seq // CMP_STRIDE
    n_cmp = (seq - CMP_BLOCK) // CMP_STRIDE + 1
    n_slc = seq // SLC_BLOCK
    n_sel = min(SLC_COUNT, n_slc)
    qw = C_HEADS * HEAD_DIM
    cmp_end = np.arange(nch) * CMP_STRIDE + CMP_BLOCK - 1
    dist_c = jnp.asarray(np.arange(seq)[:, None] - cmp_end[None, :])
    bias_c = tbl.astype(F32)[_rel_bucket(dist_c)].transpose(2, 0, 1)
    cs = np.arange(nch)[:, None] * CMP_STRIDE
    ss = np.arange(n_slc)[None, :] * SLC_BLOCK
    overlap = ((cs < ss + SLC_BLOCK) & (cs + CMP_BLOCK > ss) & (np.arange(nch)[:, None] < n_cmp))
    return pl.pallas_call(
        functools.partial(_cmp_kernel, n_cmp=n_cmp, n_slc=n_slc, n_sel=n_sel),
        out_shape=(jax.ShapeDtypeStruct((batch * seq, qw), BF16),
                   jax.ShapeDtypeStruct((batch, G, seq, n_slc), BF16)),
        grid=(batch, nq),
        in_specs=[
            pl.BlockSpec((QB, qw), lambda b, n: (b * nq + n, 0)),
            pl.BlockSpec((G, nch, HEAD_DIM), lambda b, n: (b, 0, 0)),
            pl.BlockSpec((G, nch, HEAD_DIM), lambda b, n: (b, 0, 0)),
            pl.BlockSpec((C_HEADS, QB, nch), lambda b, n: (0, n, 0)),
            pl.BlockSpec((QB, LANES), lambda b, n: (b * nq + n, 0)),
            pl.BlockSpec((nch, n_slc), lambda b, n: (0, 0)),
        ],
        out_specs=(pl.BlockSpec((QB, qw), lambda b, n: (b * nq + n, 0)),
                   pl.BlockSpec((1, G, QB, n_slc), lambda b, n: (b, 0, n, 0))),
        compiler_params=_cparams("parallel", "parallel"),
        name="cmp_attention",
    )(proj, kc, vc, bias_c, gates, jnp.asarray(overlap.astype(np.float32)).astype(BF16))


def _slc_kernel(q_ref, k_ref, v_ref, sel_ref, bias_ref, gate_ref, o_ref, vt_s, m_s, l_s, acc_s, *, n_slc):
    n = pl.program_id(1)
    G = C_KV_HEADS
    hpg = C_HEADS // G
    nq = vt_s.shape[0]

    @pl.when(n == 0)
    def _():
        for kt in range(nq):
            vt_s[kt] = v_ref[kt * QB:(kt + 1) * QB, :].astype(F32).T.astype(BF16)

    q = q_ref[...]
    qst = [jnp.concatenate([q[:, (g * hpg + hi) * HEAD_DIM:(g * hpg + hi + 1) * HEAD_DIM] * HEAD_DIM ** -0.5
                            for hi in range(hpg)], axis=0) for g in range(G)]
    selst = [jnp.concatenate([sel_ref[0, g]] * hpg, axis=0) for g in range(G)]
    kpos0 = lax.broadcasted_iota(jnp.int32, (QB, QB), 0)
    qpos = n * QB + lax.broadcasted_iota(jnp.int32, (QB, QB), 1)
    ej = lax.broadcasted_iota(jnp.int32, (QB, n_slc), 0)
    eb = lax.broadcasted_iota(jnp.int32, (QB, n_slc), 1)
    m_s[...] = jnp.full(m_s.shape, NEG_INF, F32)
    l_s[...] = jnp.zeros(l_s.shape, F32)
    acc_s[...] = jnp.zeros(acc_s.shape, F32)

    def tile(kt, carry):
        k_t = k_ref[pl.ds(pl.multiple_of(kt * QB, QB), QB), :]
        vt_t = vt_s[kt]
        causal = kt * QB + kpos0 <= qpos
        expand = jnp.where(eb == (kt * QB + ej) // SLC_BLOCK, 1.0, 0.0).astype(BF16)
        d = jnp.minimum(n - kt, 2)
        for g in range(G):
            s = _dot_nt(k_t[:, g * HEAD_DIM:(g + 1) * HEAD_DIM], qst[g])
            picked = _dot_nt(expand, selst[g])
            scale, probs = [], []
            for hi in range(hpg):
                sl = slice(hi * QB, (hi + 1) * QB)
                ok = (picked[:, sl] > 0.5) & causal
                sh = jnp.where(ok, s[:, sl] + bias_ref[g * hpg + hi, d], NEG_INF)
                m_old = m_s[g, :, sl]
                m_new = jnp.maximum(m_old, jnp.max(sh, axis=0, keepdims=True))
                a = jnp.exp(m_old - m_new)
                p = jnp.exp(sh - m_new)
                l_s[g, :, sl] = a * l_s[g, :, sl] + jnp.sum(p, axis=0, keepdims=True)
                m_s[g, :, sl] = m_new
                scale.append(a)
                probs.append(p.astype(BF16))
            pv = _dot(vt_t[g * HEAD_DIM:(g + 1) * HEAD_DIM, :], jnp.concatenate(probs, axis=1))
            for hi in range(hpg):
                sl = slice(hi * QB, (hi + 1) * QB)
                acc_s[g, :, sl] = scale[hi] * acc_s[g, :, sl] + pv[:, sl]
        return carry

    lax.fori_loop(0, n + 1, tile, 0)

    gate_t = gate_ref[...].T
    for g in range(G):
        for hi in range(hpg):
            h = g * hpg + hi
            sl = slice(hi * QB, (hi + 1) * QB)
            sig = 1.0 / (1.0 + jnp.exp(-gate_t[h:h + 1, :]))
            o = acc_s[g, :, sl] / l_s[g, :, sl] * sig
            o_ref[:, h * HEAD_DIM:(h + 1) * HEAD_DIM] = o.T.astype(o_ref.dtype)


def slc_attention(proj, sel, gates, tbl, batch, seq):
    nq = seq // QB
    G = C_KV_HEADS
    n_slc = seq // SLC_BLOCK
    qw = C_HEADS * HEAD_DIM
    return pl.pallas_call(
        functools.partial(_slc_kernel, n_slc=n_slc),
        out_shape=jax.ShapeDtypeStruct((batch * seq, qw), BF16),
        grid=(batch, nq),
        in_specs=[
            pl.BlockSpec((QB, qw), lambda b, n: (b * nq + n, 0)),
            pl.BlockSpec((seq, LANES), lambda b, n: (b, 10)),
            pl.BlockSpec((seq, LANES), lambda b, n: (b, 11)),
            pl.BlockSpec((1, G, QB, n_slc), lambda b, n: (b, 0, n, 0)),
            pl.BlockSpec((C_HEADS, 3, QB, QB), lambda b, n: (0, 0, 0, 0)),
            pl.BlockSpec((QB, LANES), lambda b, n: (b * nq + n, 1)),
        ],
        out_specs=pl.BlockSpec((QB, qw), lambda b, n: (b * nq + n, 0)),
        scratch_shapes=[pltpu.VMEM((nq, G * HEAD_DIM, QB), BF16),
                        pltpu.VMEM((G, 1, C_HEADS // G * QB), F32),
                        pltpu.VMEM((G, 1, C_HEADS // G * QB), F32),
                        pltpu.VMEM((G, HEAD_DIM, C_HEADS // G * QB), F32)],
        compiler_params=_cparams("parallel", "arbitrary"),
        name="slc_attention",
    )(proj, proj, proj, sel, _tile_bias(tbl).swapaxes(2, 3), gates)


def _xattn_kernel(h_ref, wq_ref, k_ref, v_ref, wo_ref, g_ref, b_ref, o_ref):
    x = h_ref[...]
    q = _dot(x.astype(BF16), wq_ref[...]).astype(BF16)
    k = k_ref[...]
    v = v_ref[...]
    outs = []
    for hd in range(X_HEADS):
        sl = slice(hd * X_HEAD_DIM, (hd + 1) * X_HEAD_DIM)
        lg = _dot_nt(q[:, sl], k[:, sl]) * X_HEAD_DIM ** -0.5
        e = jnp.exp(lg - jnp.max(lg, axis=-1, keepdims=True))
        den = jnp.sum(e, axis=-1, keepdims=True)
        outs.append((_dot(e.astype(BF16), v[:, sl]) / den).astype(BF16))
    o = jnp.concatenate(outs, axis=-1)
    y = ALPHA * x + _dot(o, wo_ref[...])
    o_ref[...] = _layer_norm_rows(y, g_ref[...], b_ref[...])


def cross_attention_ln(h, mem, w_q, w_k, w_v, w_o, g, b, batch, seq, tm=256):
    M = mem.shape[0] // batch
    XW = X_HEADS * X_HEAD_DIM
    D = D_MODEL
    kv = matmul(mem, jnp.concatenate([w_k, w_v], axis=1).astype(BF16), BF16, tm=512)
    nt = seq // tm
    fixed = lambda bb, i: (0, 0)
    return pl.pallas_call(
        _xattn_kernel,
        out_shape=jax.ShapeDtypeStruct((batch * seq, D), F32),
        grid=(batch, nt),
        in_specs=[pl.BlockSpec((tm, D), lambda bb, i: (bb * nt + i, 0)),
                  pl.BlockSpec((D, XW), fixed),
                  pl.BlockSpec((M, XW), lambda bb, i: (bb, 0)),
                  pl.BlockSpec((M, XW), lambda bb, i: (bb, 1)),
                  pl.BlockSpec((XW, D), fixed),
                  pl.BlockSpec((1, D), fixed), pl.BlockSpec((1, D), fixed)],
        out_specs=pl.BlockSpec((tm, D), lambda bb, i: (bb * nt + i, 0)),
        compiler_params=_cparams("parallel", "parallel"),
        name="cross_attention_ln",
    )(h, w_q.astype(BF16), kv, kv, w_o.astype(BF16), g.reshape(1, D), b.reshape(1, D))


def _silu(a):
    return a * (1.0 / (1.0 + jnp.exp(-a)))


def _swiglu_kernel(x_ref, wg_ref, wu_ref, wd_ref, g_ref, b_ref, o_ref):
    x = x_ref[...]
    xb = x.astype(BF16)
    hmid = (_silu(_dot(xb, wg_ref[...])) * _dot(xb, wu_ref[...])).astype(BF16)
    y = ALPHA * x + _dot(hmid, wd_ref[...])
    o_ref[...] = _layer_norm_rows(y, g_ref[...], b_ref[...])


def swiglu_ln(h, w_gate, w_up, w_down, g, b, tm=256):
    T, D = h.shape
    F = w_gate.shape[1]
    fixed = lambda i: (0, 0)
    once = pl.Buffered(1)
    return pl.pallas_call(
        _swiglu_kernel,
        out_shape=jax.ShapeDtypeStruct((T, D), F32),
        grid=(T // tm,),
        in_specs=[pl.BlockSpec((tm, D), lambda i: (i, 0)),
                  pl.BlockSpec((D, F), fixed, pipeline_mode=once),
                  pl.BlockSpec((D, F), fixed, pipeline_mode=once),
                  pl.BlockSpec((F, D), fixed, pipeline_mode=once),
                  pl.BlockSpec((1, D), fixed), pl.BlockSpec((1, D), fixed)],
        out_specs=pl.BlockSpec((tm, D), lambda i: (i, 0)),
        compiler_params=_cparams("parallel"),
        name="swiglu_ln",
    )(h, w_gate.astype(BF16), w_up.astype(BF16), w_down.astype(BF16), g.reshape(1, D), b.reshape(1, D))


def _router_kernel(x_ref, w_ref, o_ref):
    x = x_ref[...]
    w = w_ref[...]
    x_hi = x.astype(BF16)
    x_lo = (x - x_hi.astype(F32)).astype(BF16)
    w_hi = w.astype(BF16)
    w_lo = (w - w_hi.astype(F32)).astype(BF16)
    lg = _dot(x_hi, w_hi) + (_dot(x_hi, w_lo) + _dot(x_lo, w_hi))
    e_iota = lax.broadcasted_iota(jnp.int32, lg.shape, 1).astype(F32)
    m1 = jnp.max(lg, axis=-1, keepdims=True)
    i1 = jnp.min(jnp.where(lg == m1, e_iota, float(N_EXPERTS)), axis=-1, keepdims=True)
    first = e_iota == i1
    rest = jnp.where(first, -jnp.inf, lg)
    m2 = jnp.max(rest, axis=-1, keepdims=True)
    i2 = jnp.min(jnp.where(rest == m2, e_iota, float(N_EXPERTS)), axis=-1, keepdims=True)
    second = e_iota == i2
    w2 = jnp.exp(m2 - m1)
    den = 1.0 + w2
    o_ref[...] = jnp.where(first, 1.0 / den, 0.0) + jnp.where(second, w2 / den, 0.0)


def router_gates(h, w_router, tm=1024):
    T, D = h.shape
    E = w_router.shape[1]
    return pl.pallas_call(
        _router_kernel,
        out_shape=jax.ShapeDtypeStruct((T, E), F32),
        grid=(T // tm,),
        in_specs=[pl.BlockSpec((tm, D), lambda i: (i, 0)), pl.BlockSpec((D, E), lambda i: (0, 0))],
        out_specs=pl.BlockSpec((tm, E), lambda i: (i, 0)),
        compiler_params=_cparams("parallel"),
        name="router_gates",
    )(h, w_router)


def _moe_kernel(x_ref, gate_ref, wg_ref, wu_ref, wd_ref, g_ref, b_ref, o_ref, acc_s):
    e = pl.program_id(1)
    c = pl.program_id(2)

    @pl.when((e == 0) & (c == 0))
    def _():
        acc_s[...] = jnp.zeros(acc_s.shape, F32)

    xb = x_ref[...].astype(BF16)
    gates = gate_ref[...]
    e_iota = lax.broadcasted_iota(jnp.int32, gates.shape, 1)
    ge = jnp.sum(jnp.where(e_iota == e, gates, 0.0), axis=-1, keepdims=True)
    hmid = _silu(_dot(xb, wg_ref[0])) * _dot(xb, wu_ref[0]) * ge
    acc_s[...] += _dot(hmid.astype(BF16), wd_ref[0])

    @pl.when((e == pl.num_programs(1) - 1) & (c == pl.num_programs(2) - 1))
    def _():
        y = ALPHA * x_ref[...] + acc_s[...]
        o_ref[...] = _layer_norm_rows(y, g_ref[...], b_ref[...])


def moe_ln(h, gates, w_gate, w_up, w_down, g, b, tm=1024, fc=896):
    T, D = h.shape
    E, _, F = w_gate.shape
    nfc = F // fc
    fixed = lambda i, e, c: (0, 0)
    return pl.pallas_call(
        _moe_kernel,
        out_shape=jax.ShapeDtypeStruct((T, D), F32),
        grid=(T // tm, E, nfc),
        in_specs=[pl.BlockSpec((tm, D), lambda i, e, c: (i, 0)),
                  pl.BlockSpec((tm, E), lambda i, e, c: (i, 0)),
                  pl.BlockSpec((1, D, fc), lambda i, e, c: (e, 0, c)),
                  pl.BlockSpec((1, D, fc), lambda i, e, c: (e, 0, c)),
                  pl.BlockSpec((1, fc, D), lambda i, e, c: (e, c, 0)),
                  pl.BlockSpec((1, D), fixed), pl.BlockSpec((1, D), fixed)],
        out_specs=pl.BlockSpec((tm, D), lambda i, e, c: (i, 0)),
        scratch_shapes=[pltpu.VMEM((tm, D), F32)],
        compiler_params=_cparams("parallel", "arbitrary", "arbitrary"),
        name="moe_ln",
    )(h, gates, w_gate.astype(BF16), w_up.astype(BF16), w_down.astype(BF16),
      g.reshape(1, D), b.reshape(1, D))


def _pad_cols(w, width):
    return jnp.pad(w, ((0, 0), (0, width - w.shape[1])))


def even_layer_mixer(h, w_in, ckv_gain, w_uk, w_uv, sinks, w_out, rel_table, ln_g, ln_b, batch, seq):
    cuts = np.cumsum((A_HEADS * HEAD_DIM, A_LATENT, IDX_HEADS * IDX_DIM, IDX_DIM, IDX_HEADS,
                      B_HEADS * HEAD_DIM, B_KV_HEADS * HEAD_DIM))
    w_qa, w_ckv, w_qi, w_ki, w_wi, w_qb, w_kb, w_vb = jnp.split(w_in, [int(c) for c in cuts], axis=1)
    w_main = jnp.concatenate([w_qa, w_qi, w_qb, _pad_cols(w_ki, LANES), w_kb, w_vb], axis=1).astype(BF16)
    w_aux = jnp.concatenate([w_ckv, _pad_cols(w_wi, LANES)], axis=1).astype(BF16)
    proj = matmul(h, w_main, BF16)
    aux = matmul(h, w_aux, F32)
    o_a = dsa_attention(proj, aux, ckv_gain, w_uk, w_uv, rel_table[:, :A_HEADS], batch, seq)
    o_b = banded_attention(proj, 2, 13, 14, B_HEADS, B_KV_HEADS,
                           _band_bias(rel_table[:, A_HEADS:A_HEADS + B_HEADS], B_WINDOW), B_WINDOW,
                           batch, seq, sinks=sinks)
    o = jnp.concatenate([o_a, o_b], axis=1)
    return matmul_residual_ln([o], w_out.astype(BF16), h, ln_g, ln_b)


def odd_layer_mixer(h, w_in, pe_k, w1_k, w2_k, pe_v, w1_v, w2_v, w_out, rel_table, ln_g, ln_b, batch, seq):
    kvw = C_KV_HEADS * HEAD_DIM
    qw = C_HEADS * HEAD_DIM
    w_main = w_in[:, :qw + 6 * kvw].astype(BF16)
    w_gl = w_in[:, qw + 6 * kvw:].reshape(D_MODEL, C_HEADS, 3)
    w_gate = jnp.concatenate([_pad_cols(w_gl[:, :, r], LANES) for r in range(3)], axis=1).astype(BF16)
    proj = matmul(h, w_main, BF16)
    gates = matmul(h, w_gate, F32)
    tbl = rel_table[:, :C_HEADS]
    kc = compress_blocks(proj[:, qw:qw + kvw], pe_k, w1_k, w2_k, batch, seq)
    vc = compress_blocks(proj[:, qw + kvw:qw + 2 * kvw], pe_v, w1_v, w2_v, batch, seq)
    o_cmp, sel = cmp_attention(proj, kc, vc, gates, tbl, batch, seq)
    o_slc = slc_attention(proj, sel, gates, tbl, batch, seq)
    o_win = banded_attention(proj, 0, 12, 13, C_HEADS, C_KV_HEADS, _band_bias(tbl, C_WINDOW), C_WINDOW,
                             batch, seq, gates=gates, gate_col=2)
    return matmul_residual_ln([o_cmp, o_slc, o_win], w_out.astype(BF16), h, ln_g, ln_b)


def kernel(x, mem, rel_table, ev_w_in, ev_ckv_gain, ev_w_uk, ev_w_uv, ev_sinks, ev_w_out, od_w_in, od_pe_k, od_w1_k, od_w2_k, od_pe_v, od_w1_v, od_w2_v, od_w_out, xa_w_q, xa_w_k, xa_w_v, xa_w_o, ff_w_gate, ff_w_up, ff_w_down, moe_w_router, moe_w_gate, moe_w_up, moe_w_down, ln_g, ln_b):
    batch, seq, D = x.shape
    h = x.reshape(batch * seq, D)
    mem2 = mem.reshape(-1, D)
    for i in range(DEPTH):
        j = i // 2
        if i % 2 == 0:
            h = even_layer_mixer(h, ev_w_in[j], ev_ckv_gain[j], ev_w_uk[j], ev_w_uv[j], ev_sinks[j],
                                 ev_w_out[j], rel_table, ln_g[i, 0], ln_b[i, 0], batch, seq)
        else:
            h = odd_layer_mixer(h, od_w_in[j], od_pe_k[j], od_w1_k[j], od_w2_k[j], od_pe_v[j],
                                od_w1_v[j], od_w2_v[j], od_w_out[j], rel_table, ln_g[i, 0], ln_b[i, 0],
                                batch, seq)
        h = cross_attention_ln(h, mem2, xa_w_q[i], xa_w_k[i], xa_w_v[i], xa_w_o[i],
                               ln_g[i, 1], ln_b[i, 1], batch, seq)
        if i % 2 == 0:
            h = swiglu_ln(h, ff_w_gate[j], ff_w_up[j], ff_w_down[j], ln_g[i, 2], ln_b[i, 2])
        else:
            gates = router_gates(h, moe_w_router[j])
            h = moe_ln(h, gates, moe_w_gate[j], moe_w_up[j], moe_w_down[j], ln_g[i, 2], ln_b[i, 2])
    return h.reshape(batch, seq, D)
```

```python
import functools
import math

import numpy as np
import jax
import jax.numpy as jnp
from jax import lax
from jax.experimental import pallas as pl
from jax.experimental.pallas import tpu as pltpu
from jax.experimental.pallas import tpu_sc as plsc

D_MODEL = 1024
DEPTH = 2
HEAD_DIM = 64
N_SLOTS = D_MODEL // HEAD_DIM
A_HEADS = 8
A_LATENT = 128
IDX_HEADS = 8
IDX_DIM = 64
A_TOPK_MAX = 256
B_HEADS = 8
B_KV_HEADS = 2
B_WINDOW = 128
C_HEADS = 16
C_KV_HEADS = 2
CMP_BLOCK = 32
CMP_STRIDE = 16
CMP_HIDDEN = 128
SLC_BLOCK = 64
SLC_COUNT = 8
C_WINDOW = 512
X_HEADS = 4
X_HEAD_DIM = 128
D_FF = 2816
N_EXPERTS = 8
TOP_K = 2
D_FF_EXPERT = 3584
REL_BUCKETS = 32
REL_MAX_DIST = 128
LN_EPS = 1e-5
NEG_INF = -1e30
FORCE_BONUS = 1e6
ALPHA = (2 * DEPTH) ** 0.25

LANES = 128
QB = 128
VMEM_LIMIT_BYTES = 56 * 1024 * 1024
SC_GATHER_WINDOW = 128

F32 = jnp.float32
BF16 = jnp.bfloat16


def _cparams(*sem):
    return pltpu.CompilerParams(dimension_semantics=sem, vmem_limit_bytes=VMEM_LIMIT_BYTES)


def _dot(a, b):
    return jnp.dot(a, b, preferred_element_type=F32)


def _dot_nt(a, b):
    return lax.dot_general(a, b, (((1,), (1,)), ((), ())), preferred_element_type=F32)


def _layer_norm_rows(y, g, b):
    mu = jnp.mean(y, axis=-1, keepdims=True)
    yc = y - mu
    var = jnp.mean(yc * yc, axis=-1, keepdims=True)
    return yc * lax.rsqrt(var + LN_EPS) * g + b


def _mm_kernel(x_ref, w_ref, o_ref):
    o_ref[...] = _dot(x_ref[...].astype(BF16), w_ref[...]).astype(o_ref.dtype)


def matmul(x, w, out_dtype, tm=512):
    M, K = x.shape
    N = w.shape[1]
    tm = min(tm, M)
    return pl.pallas_call(
        _mm_kernel,
        out_shape=jax.ShapeDtypeStruct((M, N), out_dtype),
        grid=(M // tm,),
        in_specs=[pl.BlockSpec((tm, K), lambda i: (i, 0)),
                  pl.BlockSpec((K, N), lambda i: (0, 0))],
        out_specs=pl.BlockSpec((tm, N), lambda i: (i, 0)),
        compiler_params=_cparams("parallel"),
        name="matmul",
    )(x, w)


def _mm_res_ln_kernel(*refs, n_in):
    a_refs = refs[:n_in]
    w_ref, h_ref, g_ref, b_ref, o_ref = refs[n_in:]
    a = a_refs[0][...]
    if n_in > 1:
        a = a.astype(F32)
        for r in a_refs[1:]:
            a = a + r[...].astype(F32)
    y = ALPHA * h_ref[...] + _dot(a.astype(BF16), w_ref[...])
    o_ref[...] = _layer_norm_rows(y, g_ref[...], b_ref[...])


def matmul_residual_ln(a_list, w, h, g, b, tm=512):
    M, K = a_list[0].shape
    N = w.shape[1]
    n_in = len(a_list)
    row = lambda i: (i, 0)
    fixed = lambda i: (0, 0)
    return pl.pallas_call(
        functools.partial(_mm_res_ln_kernel, n_in=n_in),
        out_shape=jax.ShapeDtypeStruct((M, N), F32),
        grid=(M // tm,),
        in_specs=[pl.BlockSpec((tm, K), row)] * n_in + [
            pl.BlockSpec((K, N), fixed), pl.BlockSpec((tm, N), row),
            pl.BlockSpec((1, N), fixed), pl.BlockSpec((1, N), fixed)],
        out_specs=pl.BlockSpec((tm, N), row),
        compiler_params=_cparams("parallel"),
        name="matmul_residual_ln",
    )(*a_list, w, h, g.reshape(1, N), b.reshape(1, N))


def _rel_bucket(dist):
    n = jnp.maximum(dist, 0)
    max_exact = REL_BUCKETS // 2
    nf = jnp.maximum(n, 1).astype(F32)
    log_b = max_exact + (jnp.log(nf / max_exact) / math.log(REL_MAX_DIST / max_exact)
                         * (REL_BUCKETS - max_exact)).astype(jnp.int32)
    return jnp.where(n < max_exact, n, jnp.minimum(log_b, REL_BUCKETS - 1))


def _bias_lookup(tbl, dist):
    onehot = (_rel_bucket(dist)[..., None] == jnp.arange(REL_BUCKETS, dtype=jnp.int32)).astype(F32)
    return jnp.einsum("...r,rh->...h", onehot, tbl.astype(F32), precision=lax.Precision.HIGHEST)


def _tile_bias(tbl):
    i = np.arange(QB)[:, None]
    j = np.arange(QB)[None, :]
    dist = jnp.asarray(np.stack([i - j, QB + i - j, 2 * QB + i - j]))
    return _bias_lookup(tbl, dist).transpose(3, 0, 1, 2)


def _band_bias(tbl, window):
    span = window + QB
    dist = jnp.asarray(np.arange(QB)[:, None] + window - np.arange(span)[None, :])
    return _bias_lookup(tbl, dist).transpose(2, 0, 1)


def _band_kernel(*refs, window, kvh, g, has_sinks, has_gate):
    it = iter(refs)
    q_ref, k_ref, v_ref, bias_ref = next(it), next(it), next(it), next(it)
    sink_ref = next(it) if has_sinks else None
    gate_ref = next(it) if has_gate else None
    o_ref = next(it)
    n = pl.program_id(1)
    span = window + QB
    start = pl.multiple_of(n * QB, QB)
    i = lax.broadcasted_iota(jnp.int32, (QB, span), 0)
    j = lax.broadcasted_iota(jnp.int32, (QB, span), 1)
    dist = i + window - j
    ok = (dist >= 0) & (dist < window) & (start - window + j >= 0)
    q = q_ref[...]
    kall = k_ref[0, pl.ds(start, span), :]
    vall = v_ref[0, pl.ds(start, span), :]
    for kv in range(kvh):
        ks = kall[:, kv * HEAD_DIM:(kv + 1) * HEAD_DIM]
        vs = vall[:, kv * HEAD_DIM:(kv + 1) * HEAD_DIM]
        for gi in range(g):
            h = kv * g + gi
            lg = _dot_nt(q[:, h * HEAD_DIM:(h + 1) * HEAD_DIM], ks) * HEAD_DIM ** -0.5 + bias_ref[h]
            lg = jnp.where(ok, lg, NEG_INF)
            m = jnp.max(lg, axis=-1, keepdims=True)
            if has_sinks:
                s = sink_ref[h]
                m = jnp.maximum(m, s[:, :1])
            e = jnp.exp(lg - m)
            den = jnp.sum(e, axis=-1, keepdims=True)
            if has_sinks:
                den = den + jnp.exp(s[:, :1] - m)
            o = _dot(e.astype(BF16), vs) / den
            if has_gate:
                gl = gate_ref[:, h:h + 1]
                o = o * (1.0 / (1.0 + jnp.exp(-gl)))
            o_ref[:, h * HEAD_DIM:(h + 1) * HEAD_DIM] = o.astype(o_ref.dtype)


def banded_attention(proj, q_col, k_col, v_col, n_heads, kvh, bias, window, batch, seq,
                     sinks=None, gates=None, gate_col=0):
    g = n_heads // kvh
    kw = kvh * HEAD_DIM
    qw = n_heads * HEAD_DIM
    nq = seq // QB
    span = window + QB
    k = proj[:, k_col * kw:(k_col + 1) * kw].reshape(batch, seq, kw)
    v = proj[:, v_col * kw:(v_col + 1) * kw].reshape(batch, seq, kw)
    kp = jnp.pad(k, ((0, 0), (window, 0), (0, 0)))
    vp = jnp.pad(v, ((0, 0), (window, 0), (0, 0)))
    args = [proj, kp, vp, bias]
    in_specs = [pl.BlockSpec((QB, qw), lambda b, n: (b * nq + n, q_col)),
                pl.BlockSpec((1, seq + window, kw), lambda b, n: (b, 0, 0)),
                pl.BlockSpec((1, seq + window, kw), lambda b, n: (b, 0, 0)),
                pl.BlockSpec((n_heads, QB, span), lambda b, n: (0, 0, 0))]
    if sinks is not None:
        args.append(jnp.broadcast_to(sinks.astype(F32)[:, None, None], (n_heads, 1, LANES)))
        in_specs.append(pl.BlockSpec((n_heads, 1, LANES), lambda b, n: (0, 0, 0)))
    if gates is not None:
        args.append(gates)
        in_specs.append(pl.BlockSpec((QB, LANES), lambda b, n: (b * nq + n, gate_col)))
    return pl.pallas_call(
        functools.partial(_band_kernel, window=window, kvh=kvh, g=g,
                          has_sinks=sinks is not None, has_gate=gates is not None),
        out_shape=jax.ShapeDtypeStruct((batch * seq, qw), BF16),
        grid=(batch, nq),
        in_specs=in_specs,
        out_specs=pl.BlockSpec((QB, qw), lambda b, n: (b * nq + n, 0)),
        compiler_params=_cparams("parallel", "parallel"),
        name="banded_attention",
    )(*args)


def _sort_key(x):
    bits = lax.bitcast_convert_type(x, jnp.int32)
    return bits ^ ((bits >> 31) & jnp.int32(0x7FFFFFFF))


def _dsa_kernel(qa_ref, qi_ref, ki_ref, ckv_ref, wi_ref, gain_ref, wuk_ref, wuv_ref, bias_ref,
                o_ref, c_s, ct_s, key_s, m_s, l_s, acc_s, *, topk):
    n = pl.program_id(1)
    nt = n + 1
    nq = c_s.shape[0]

    @pl.when(n == 0)
    def _():
        for kt in range(nq):
            c = ckv_ref[kt * QB:(kt + 1) * QB, :]
            y = c * lax.rsqrt(jnp.mean(c * c, axis=-1, keepdims=True) + LN_EPS) * gain_ref[...]
            c_s[kt] = y.astype(BF16)
            ct_s[kt] = y.T.astype(BF16)

    qa = qa_ref[...]
    qi = qi_ref[...]
    qlat = jnp.concatenate(
        [(_dot(qa[:, h * HEAD_DIM:(h + 1) * HEAD_DIM], wuk_ref[h]) * HEAD_DIM ** -0.5).astype(BF16)
         for h in range(A_HEADS)], axis=0)
    qidx = jnp.concatenate(
        [qi[:, h * IDX_DIM:(h + 1) * IDX_DIM] * IDX_DIM ** -0.5 for h in range(IDX_HEADS)], axis=0)
    wi_t = wi_ref[...].T
    wi_row = jnp.concatenate([wi_t[h:h + 1, :] for h in range(IDX_HEADS)], axis=1) * IDX_HEADS ** -0.5

    kpos0 = lax.broadcasted_iota(jnp.int32, (QB, QB), 0)
    qpos = n * QB + lax.broadcasted_iota(jnp.int32, (QB, QB), 1)

    def score_tile(kt, carry):
        k_t = ki_ref[pl.ds(pl.multiple_of(kt * QB, QB), QB), :][:, :IDX_DIM]
        act = jnp.maximum(_dot_nt(k_t, qidx), 0.0) * wi_row
        sc = act[:, :QB]
        for h in range(1, IDX_HEADS):
            sc = sc + act[:, h * QB:(h + 1) * QB]
        sc = jnp.where(kt * QB + kpos0 <= qpos, sc, NEG_INF)
        sc = jnp.where(sc == 0.0, 0.0, sc)
        key_s[kt] = _sort_key(sc)
        return carry

    lax.fori_loop(0, nt, score_tile, 0)

    def count(pred):
        def body(kt, acc):
            return acc + jnp.where(pred(kt, key_s[kt]), 1.0, 0.0)
        acc = lax.fori_loop(0, nt, body, jnp.zeros((QB, QB), F32))
        return jnp.sum(acc, axis=0, keepdims=True)

    def thr_step(i, lo):
        cand = lo + (jnp.int32(1) << (31 - i))
        cnt = count(lambda kt, key: key >= cand)
        return jnp.where(cnt >= topk, cand, lo)

    thr = lax.fori_loop(0, 32, thr_step, jnp.full((1, QB), -2 ** 31, jnp.int32))
    need = topk - count(lambda kt, key: key > thr)

    def tie_step(i, j0):
        cand = j0 + (jnp.int32(1) << (10 - i))
        cnt = count(lambda kt, key: (key == thr) & (kt * QB + kpos0 < cand))
        return jnp.where(cnt < need, cand, j0)

    j0 = lax.fori_loop(0, 11, tie_step, jnp.zeros((1, QB), jnp.int32))

    m_s[...] = jnp.full(m_s.shape, NEG_INF, F32)
    l_s[...] = jnp.zeros(l_s.shape, F32)
    acc_s[...] = jnp.zeros(acc_s.shape, F32)

    def attend_tile(kt, carry):
        key = key_s[kt]
        kpos = kt * QB + kpos0
        ok = ((key > thr) | ((key == thr) & (kpos <= j0))) & (kpos <= qpos)
        d = jnp.minimum(n - kt, 2)
        s = _dot_nt(c_s[kt], qlat)
        scale, probs = [], []
        for h in range(A_HEADS):
            sl = slice(h * QB, (h + 1) * QB)
            sh = jnp.where(ok, s[:, sl] + bias_ref[h, d], NEG_INF)
            m_old = m_s[:, sl]
            m_new = jnp.maximum(m_old, jnp.max(sh, axis=0, keepdims=True))
            a = jnp.exp(m_old - m_new)
            p = jnp.where(ok, jnp.exp(sh - m_new), 0.0)
            l_s[:, sl] = a * l_s[:, sl] + jnp.sum(p, axis=0, keepdims=True)
            m_s[:, sl] = m_new
            scale.append(a)
            probs.append(p.astype(BF16))
        pv = _dot(ct_s[kt], jnp.concatenate(probs, axis=1))
        for h in range(A_HEADS):
            sl = slice(h * QB, (h + 1) * QB)
            acc_s[:, sl] = scale[h] * acc_s[:, sl] + pv[:, sl]
        return carry

    lax.fori_loop(0, nt, attend_tile, 0)

    for h in range(A_HEADS):
        sl = slice(h * QB, (h + 1) * QB)
        o_lat = (acc_s[:, sl] / l_s[:, sl]).T
        o_ref[:, h * HEAD_DIM:(h + 1) * HEAD_DIM] = _dot(o_lat.astype(BF16), wuv_ref[h]).astype(o_ref.dtype)


def dsa_attention(proj, aux, gain, w_uk, w_uv, tbl, batch, seq):
    nq = seq // QB
    topk = min(A_TOPK_MAX, seq // 4)
    qw = A_HEADS * HEAD_DIM
    return pl.pallas_call(
        functools.partial(_dsa_kernel, topk=topk),
        out_shape=jax.ShapeDtypeStruct((batch * seq, qw), BF16),
        grid=(batch, nq),
        in_specs=[
            pl.BlockSpec((QB, qw), lambda b, n: (b * nq + n, 0)),
            pl.BlockSpec((QB, qw), lambda b, n: (b * nq + n, 1)),
            pl.BlockSpec((seq, LANES), lambda b, n: (b, 12)),
            pl.BlockSpec((seq, A_LATENT), lambda b, n: (b, 0)),
            pl.BlockSpec((QB, LANES), lambda b, n: (b * nq + n, 1)),
            pl.BlockSpec((1, A_LATENT), lambda b, n: (0, 0)),
            pl.BlockSpec((A_HEADS, HEAD_DIM, A_LATENT), lambda b, n: (0, 0, 0)),
            pl.BlockSpec((A_HEADS, A_LATENT, HEAD_DIM), lambda b, n: (0, 0, 0)),
            pl.BlockSpec((A_HEADS, 3, QB, QB), lambda b, n: (0, 0, 0, 0)),
        ],
        out_specs=pl.BlockSpec((QB, qw), lambda b, n: (b * nq + n, 0)),
        scratch_shapes=[
            pltpu.VMEM((nq, QB, A_LATENT), BF16),
            pltpu.VMEM((nq, A_LATENT, QB), BF16),
            pltpu.VMEM((nq, QB, QB), jnp.int32),
            pltpu.VMEM((1, A_HEADS * QB), F32),
            pltpu.VMEM((1, A_HEADS * QB), F32),
            pltpu.VMEM((A_LATENT, A_HEADS * QB), F32),
        ],
        compiler_params=_cparams("parallel", "arbitrary"),
        name="dsa_attention",
    )(proj, proj, proj, aux, aux, gain.reshape(1, A_LATENT).astype(F32),
      w_uk.astype(BF16), w_uv.astype(BF16), _tile_bias(tbl).swapaxes(2, 3))


def _compress_kernel(x_ref, pe_ref, w1_ref, w2_ref, o_ref):
    half = CMP_STRIDE * HEAD_DIM
    x = x_ref[0]
    w1 = w1_ref[...]
    u = _dot(x, w1[:half])
    v = _dot(x, w1[half:])
    pre = u + pltpu.roll(v, v.shape[0] - 1, 0) + _dot(pe_ref[...], w1)[:1]
    h = 0.5 * pre * (1.0 + jnp.tanh(math.sqrt(2.0 / math.pi) * (pre + 0.044715 * pre * pre * pre)))
    o_ref[0] = _dot(h.astype(BF16), w2_ref[...]).astype(o_ref.dtype)


def compress_blocks(kv, pe, w1, w2, batch, seq):
    G = C_KV_HEADS
    nch = seq // CMP_STRIDE
    half = CMP_STRIDE * HEAD_DIM
    x = kv.reshape(batch, nch, CMP_STRIDE, G, HEAD_DIM).transpose(0, 3, 1, 2, 4)
    x = x.reshape(batch * G, nch, half)
    pe_flat = jnp.broadcast_to(pe.reshape(1, CMP_BLOCK * HEAD_DIM), (8, CMP_BLOCK * HEAD_DIM))
    return pl.pallas_call(
        _compress_kernel,
        out_shape=jax.ShapeDtypeStruct((batch * G, nch, HEAD_DIM), BF16),
        grid=(batch * G,),
        in_specs=[pl.BlockSpec((1, nch, half), lambda i: (i, 0, 0)),
                  pl.BlockSpec((8, 2 * half), lambda i: (0, 0)),
                  pl.BlockSpec((2 * half, CMP_HIDDEN), lambda i: (0, 0)),
                  pl.BlockSpec((CMP_HIDDEN, HEAD_DIM), lambda i: (0, 0))],
        out_specs=pl.BlockSpec((1, nch, HEAD_DIM), lambda i: (i, 0, 0)),
        compiler_params=_cparams("parallel"),
        name="compress_blocks",
    )(x, pe_flat.astype(BF16), w1.astype(BF16), w2.astype(BF16))


def _cmp_kernel(q_ref, kc_ref, vc_ref, bias_ref, gate_ref, ov_ref, o_ref, sel_ref, *, n_cmp, n_slc, n_sel):
    n = pl.program_id(1)
    G = C_KV_HEADS
    hpg = C_HEADS // G
    nch = kc_ref.shape[1]
    t = n * QB + lax.broadcasted_iota(jnp.int32, (QB, nch), 0)
    c = lax.broadcasted_iota(jnp.int32, (QB, nch), 1)
    valid = (t - (c * CMP_STRIDE + CMP_BLOCK - 1) >= 0) & (c < n_cmp)
    q = q_ref[...]
    ts = n * QB + lax.broadcasted_iota(jnp.int32, (QB, n_slc), 0)
    jj = lax.broadcasted_iota(jnp.int32, (QB, n_slc), 1)
    blk = ts // SLC_BLOCK
    forced = (jj == 0) | (jj == blk) | (jj == blk - 1)
    admissible = jj * SLC_BLOCK <= ts
    for g in range(G):
        kc = kc_ref[g]
        vc = vc_ref[g]
        psum = jnp.zeros((QB, nch), F32)
        for hi in range(hpg):
            h = g * hpg + hi
            lg = _dot_nt(q[:, h * HEAD_DIM:(h + 1) * HEAD_DIM], kc) * HEAD_DIM ** -0.5 + bias_ref[h]
            lg = jnp.where(valid, lg, NEG_INF)
            m = jnp.max(lg, axis=-1, keepdims=True)
            e = jnp.where(valid, jnp.exp(lg - m), 0.0)
            den = jnp.sum(e, axis=-1, keepdims=True)
            p = e / jnp.where(den > 0.0, den, 1.0)
            psum = psum + p
            gl = gate_ref[:, h:h + 1]
            o = _dot(p.astype(BF16), vc) * (1.0 / (1.0 + jnp.exp(-gl)))
            o_ref[:, h * HEAD_DIM:(h + 1) * HEAD_DIM] = o.astype(o_ref.dtype)
        p_hi = psum.astype(BF16)
        p_lo = (psum - p_hi.astype(F32)).astype(BF16)
        p_slc = _dot(p_hi, ov_ref[...]) + _dot(p_lo, ov_ref[...])
        score = jnp.where(admissible, p_slc + FORCE_BONUS * jnp.where(forced, 1.0, 0.0), NEG_INF)
        rank = jnp.zeros((QB, n_slc), F32)
        for i in range(n_slc):
            si = score[:, i:i + 1]
            ahead = (si > score) | ((si == score) & (i < jj))
            rank = rank + jnp.where(ahead, 1.0, 0.0)
        sel_ref[0, g] = jnp.where(rank < n_sel, 1.0, 0.0).astype(sel_ref.dtype)


def cmp_attention(proj, kc, vc, gates, tbl, batch, seq):
    nq = seq // QB
    G = C_KV_HEADS
    nch = seq // CMP_STRIDE
    n_cmp = (seq - CMP_BLOCK) // CMP_STRIDE + 1
    n_slc = seq // SLC_BLOCK
    n_sel = min(SLC_COUNT, n_slc)
    qw = C_HEADS * HEAD_DIM
    cmp_end = np.arange(nch) * CMP_STRIDE + CMP_BLOCK - 1
    dist_c = jnp.asarray(np.arange(seq)[:, None] - cmp_end[None, :])
    bias_c = _bias_lookup(tbl, dist_c).transpose(2, 0, 1)
    cs = np.arange(nch)[:, None] * CMP_STRIDE
    ss = np.arange(n_slc)[None, :] * SLC_BLOCK
    overlap = ((cs < ss + SLC_BLOCK) & (cs + CMP_BLOCK > ss) & (np.arange(nch)[:, None] < n_cmp))
    return pl.pallas_call(
        functools.partial(_cmp_kernel, n_cmp=n_cmp, n_slc=n_slc, n_sel=n_sel),
        out_shape=(jax.ShapeDtypeStruct((batch * seq, qw), BF16),
                   jax.ShapeDtypeStruct((batch, G, seq, n_slc), BF16)),
        grid=(batch, nq),
        in_specs=[
            pl.BlockSpec((QB, qw), lambda b, n: (b * nq + n, 0)),
            pl.BlockSpec((G, nch, HEAD_DIM), lambda b, n: (b, 0, 0)),
            pl.BlockSpec((G, nch, HEAD_DIM), lambda b, n: (b, 0, 0)),
            pl.BlockSpec((C_HEADS, QB, nch), lambda b, n: (0, n, 0)),
            pl.BlockSpec((QB, LANES), lambda b, n: (b * nq + n, 0)),
            pl.BlockSpec((nch, n_slc), lambda b, n: (0, 0)),
        ],
        out_specs=(pl.BlockSpec((QB, qw), lambda b, n: (b * nq + n, 0)),
                   pl.BlockSpec((1, G, QB, n_slc), lambda b, n: (b, 0, n, 0))),
        compiler_params=_cparams("parallel", "parallel"),
        name="cmp_attention",
    )(proj, kc, vc, bias_c, gates, jnp.asarray(overlap.astype(np.float32)).astype(BF16))


def _slc_kernel(q_ref, k_ref, v_ref, sel_ref, bias_ref, gate_ref, o_ref, vt_s, m_s, l_s, acc_s, *, n_slc):
    n = pl.program_id(1)
    G = C_KV_HEADS
    hpg = C_HEADS // G
    nq = vt_s.shape[0]

    @pl.when(n == 0)
    def _():
        for kt in range(nq):
            vt_s[kt] = v_ref[kt * QB:(kt + 1) * QB, :].astype(F32).T.astype(BF16)

    q = q_ref[...]
    qst = [jnp.concatenate([q[:, (g * hpg + hi) * HEAD_DIM:(g * hpg + hi + 1) * HEAD_DIM] * HEAD_DIM ** -0.5
                            for hi in range(hpg)], axis=0) for g in range(G)]
    selst = [jnp.concatenate([sel_ref[0, g]] * hpg, axis=0) for g in range(G)]
    kpos0 = lax.broadcasted_iota(jnp.int32, (QB, QB), 0)
    qpos = n * QB + lax.broadcasted_iota(jnp.int32, (QB, QB), 1)
    ej = lax.broadcasted_iota(jnp.int32, (QB, n_slc), 0)
    eb = lax.broadcasted_iota(jnp.int32, (QB, n_slc), 1)
    m_s[...] = jnp.full(m_s.shape, NEG_INF, F32)
    l_s[...] = jnp.zeros(l_s.shape, F32)
    acc_s[...] = jnp.zeros(acc_s.shape, F32)

    def tile(kt, carry):
        k_t = k_ref[pl.ds(pl.multiple_of(kt * QB, QB), QB), :]
        vt_t = vt_s[kt]
        causal = kt * QB + kpos0 <= qpos
        expand = jnp.where(eb == (kt * QB + ej) // SLC_BLOCK, 1.0, 0.0).astype(BF16)
        d = jnp.minimum(n - kt, 2)
        for g in range(G):
            s = _dot_nt(k_t[:, g * HEAD_DIM:(g + 1) * HEAD_DIM], qst[g])
            picked = _dot_nt(expand, selst[g])
            scale, probs = [], []
            for hi in range(hpg):
                sl = slice(hi * QB, (hi + 1) * QB)
                ok = (picked[:, sl] > 0.5) & causal
                sh = jnp.where(ok, s[:, sl] + bias_ref[g * hpg + hi, d], NEG_INF)
                m_old = m_s[g, :, sl]
                m_new = jnp.maximum(m_old, jnp.max(sh, axis=0, keepdims=True))
                a = jnp.exp(m_old - m_new)
                p = jnp.exp(sh - m_new)
                l_s[g, :, sl] = a * l_s[g, :, sl] + jnp.sum(p, axis=0, keepdims=True)
                m_s[g, :, sl] = m_new
                scale.append(a)
                probs.append(p.astype(BF16))
            pv = _dot(vt_t[g * HEAD_DIM:(g + 1) * HEAD_DIM, :], jnp.concatenate(probs, axis=1))
            for hi in range(hpg):
                sl = slice(hi * QB, (hi + 1) * QB)
                acc_s[g, :, sl] = scale[hi] * acc_s[g, :, sl] + pv[:, sl]
        return carry

    lax.fori_loop(0, n + 1, tile, 0)

    gate_t = gate_ref[...].T
    for g in range(G):
        for hi in range(hpg):
            h = g * hpg + hi
            sl = slice(hi * QB, (hi + 1) * QB)
            sig = 1.0 / (1.0 + jnp.exp(-gate_t[h:h + 1, :]))
            o = acc_s[g, :, sl] / l_s[g, :, sl] * sig
            o_ref[:, h * HEAD_DIM:(h + 1) * HEAD_DIM] = o.T.astype(o_ref.dtype)


def slc_attention(proj, sel, gates, tbl, batch, seq):
    nq = seq // QB
    G = C_KV_HEADS
    n_slc = seq // SLC_BLOCK
    qw = C_HEADS * HEAD_DIM
    return pl.pallas_call(
        functools.partial(_slc_kernel, n_slc=n_slc),
        out_shape=jax.ShapeDtypeStruct((batch * seq, qw), BF16),
        grid=(batch, nq),
        in_specs=[
            pl.BlockSpec((QB, qw), lambda b, n: (b * nq + n, 0)),
            pl.BlockSpec((seq, LANES), lambda b, n: (b, 10)),
            pl.BlockSpec((seq, LANES), lambda b, n: (b, 11)),
            pl.BlockSpec((1, G, QB, n_slc), lambda b, n: (b, 0, n, 0)),
            pl.BlockSpec((C_HEADS, 3, QB, QB), lambda b, n: (0, 0, 0, 0)),
            pl.BlockSpec((QB, LANES), lambda b, n: (b * nq + n, 1)),
        ],
        out_specs=pl.BlockSpec((QB, qw), lambda b, n: (b * nq + n, 0)),
        scratch_shapes=[pltpu.VMEM((nq, G * HEAD_DIM, QB), BF16),
                        pltpu.VMEM((G, 1, C_HEADS // G * QB), F32),
                        pltpu.VMEM((G, 1, C_HEADS // G * QB), F32),
                        pltpu.VMEM((G, HEAD_DIM, C_HEADS // G * QB), F32)],
        compiler_params=_cparams("parallel", "arbitrary"),
        name="slc_attention",
    )(proj, proj, proj, sel, _tile_bias(tbl).swapaxes(2, 3), gates)


def _xattn_kernel(h_ref, wq_ref, k_ref, v_ref, wo_ref, g_ref, b_ref, o_ref):
    x = h_ref[...]
    q = _dot(x.astype(BF16), wq_ref[...]).astype(BF16)
    k = k_ref[...]
    v = v_ref[...]
    outs = []
    for hd in range(X_HEADS):
        sl = slice(hd * X_HEAD_DIM, (hd + 1) * X_HEAD_DIM)
        lg = _dot_nt(q[:, sl], k[:, sl]) * X_HEAD_DIM ** -0.5
        e = jnp.exp(lg - jnp.max(lg, axis=-1, keepdims=True))
        den = jnp.sum(e, axis=-1, keepdims=True)
        outs.append((_dot(e.astype(BF16), v[:, sl]) / den).astype(BF16))
    o = jnp.concatenate(outs, axis=-1)
    y = ALPHA * x + _dot(o, wo_ref[...])
    o_ref[...] = _layer_norm_rows(y, g_ref[...], b_ref[...])


def cross_attention_ln(h, mem, w_q, w_k, w_v, w_o, g, b, batch, seq, tm=256):
    M = mem.shape[0] // batch
    XW = X_HEADS * X_HEAD_DIM
    D = D_MODEL
    kv = matmul(mem, jnp.concatenate([w_k, w_v], axis=1).astype(BF16), BF16, tm=512)
    nt = seq // tm
    fixed = lambda bb, i: (0, 0)
    return pl.pallas_call(
        _xattn_kernel,
        out_shape=jax.ShapeDtypeStruct((batch * seq, D), F32),
        grid=(batch, nt),
        in_specs=[pl.BlockSpec((tm, D), lambda bb, i: (bb * nt + i, 0)),
                  pl.BlockSpec((D, XW), fixed),
                  pl.BlockSpec((M, XW), lambda bb, i: (bb, 0)),
                  pl.BlockSpec((M, XW), lambda bb, i: (bb, 1)),
                  pl.BlockSpec((XW, D), fixed),
                  pl.BlockSpec((1, D), fixed), pl.BlockSpec((1, D), fixed)],
        out_specs=pl.BlockSpec((tm, D), lambda bb, i: (bb * nt + i, 0)),
        compiler_params=_cparams("parallel", "parallel"),
        name="cross_attention_ln",
    )(h, w_q.astype(BF16), kv, kv, w_o.astype(BF16), g.reshape(1, D), b.reshape(1, D))


def _silu(a):
    return a * (1.0 / (1.0 + jnp.exp(-a)))


def _swiglu_kernel(x_ref, wg_ref, wu_ref, wd_ref, g_ref, b_ref, o_ref):
    x = x_ref[...]
    xb = x.astype(BF16)
    hmid = (_silu(_dot(xb, wg_ref[...])) * _dot(xb, wu_ref[...])).astype(BF16)
    y = ALPHA * x + _dot(hmid, wd_ref[...])
    o_ref[...] = _layer_norm_rows(y, g_ref[...], b_ref[...])


def swiglu_ln(h, w_gate, w_up, w_down, g, b, tm=256):
    T, D = h.shape
    F = w_gate.shape[1]
    fixed = lambda i: (0, 0)
    once = pl.Buffered(1)
    return pl.pallas_call(
        _swiglu_kernel,
        out_shape=jax.ShapeDtypeStruct((T, D), F32),
        grid=(T // tm,),
        in_specs=[pl.BlockSpec((tm, D), lambda i: (i, 0)),
                  pl.BlockSpec((D, F), fixed, pipeline_mode=once),
                  pl.BlockSpec((D, F), fixed, pipeline_mode=once),
                  pl.BlockSpec((F, D), fixed, pipeline_mode=once),
                  pl.BlockSpec((1, D), fixed), pl.BlockSpec((1, D), fixed)],
        out_specs=pl.BlockSpec((tm, D), lambda i: (i, 0)),
        compiler_params=_cparams("parallel"),
        name="swiglu_ln",
    )(h, w_gate.astype(BF16), w_up.astype(BF16), w_down.astype(BF16), g.reshape(1, D), b.reshape(1, D))


def _router_kernel(x_ref, w_ref, idx_ref, wgt_ref, xlo_ref, xhi_ref):
    x = x_ref[...]
    w = w_ref[...]
    x_hi = x.astype(BF16)
    x_lo = (x - x_hi.astype(F32)).astype(BF16)
    w_hi = w.astype(BF16)
    w_lo = (w - w_hi.astype(F32)).astype(BF16)
    lg = _dot(x_hi, w_hi) + (_dot(x_hi, w_lo) + _dot(x_lo, w_hi))
    e_iota = lax.broadcasted_iota(jnp.int32, lg.shape, 1).astype(F32)
    m1 = jnp.max(lg, axis=-1, keepdims=True)
    i1 = jnp.min(jnp.where(lg == m1, e_iota, float(N_EXPERTS)), axis=-1, keepdims=True)
    first = e_iota == i1
    rest = jnp.where(first, -jnp.inf, lg)
    m2 = jnp.max(rest, axis=-1, keepdims=True)
    i2 = jnp.min(jnp.where(rest == m2, e_iota, float(N_EXPERTS)), axis=-1, keepdims=True)
    w2 = jnp.exp(m2 - m1)
    den = 1.0 + w2
    k_iota = lax.broadcasted_iota(jnp.int32, idx_ref.shape, 1)
    idx_ref[...] = jnp.where(k_iota == 0, i1, i2).astype(jnp.int32)
    wgt_ref[...] = jnp.where(k_iota == 0, 1.0 / den, w2 / den)
    half = x.shape[1] // 2
    xlo_ref[...] = _pack_bf16_pairs(x[:, :half])
    xhi_ref[...] = _pack_bf16_pairs(x[:, half:])


def _pack_bf16_pairs(x):
    w = x.shape[1] // 2
    lo = lax.bitcast_convert_type(x[:, :w].astype(BF16).astype(F32), jnp.uint32)
    hi = lax.bitcast_convert_type(x[:, w:].astype(BF16).astype(F32), jnp.uint32)
    return (lo >> 16) | (hi & jnp.uint32(0xFFFF0000))


def _unpack_bf16_pairs(words):
    lo = lax.bitcast_convert_type(words << 16, F32).astype(BF16)
    hi = lax.bitcast_convert_type(words & jnp.uint32(0xFFFF0000), F32).astype(BF16)
    return jnp.concatenate([lo, hi], axis=1)


def moe_route(h, w_router, tm=1024):
    T, D = h.shape
    E = w_router.shape[1]
    row = lambda i: (i, 0)
    return pl.pallas_call(
        _router_kernel,
        out_shape=(jax.ShapeDtypeStruct((T, TOP_K), jnp.int32), jax.ShapeDtypeStruct((T, TOP_K), F32),
                   jax.ShapeDtypeStruct((T, D // 4), jnp.uint32), jax.ShapeDtypeStruct((T, D // 4), jnp.uint32)),
        grid=(T // tm,),
        in_specs=[pl.BlockSpec((tm, D), row), pl.BlockSpec((D, E), lambda i: (0, 0))],
        out_specs=(pl.BlockSpec((tm, TOP_K), row), pl.BlockSpec((tm, TOP_K), row),
                   pl.BlockSpec((tm, D // 4), row), pl.BlockSpec((tm, D // 4), row)),
        compiler_params=_cparams("parallel"),
        name="moe_route",
    )(h, w_router)


def sc_gather_rows(xs, idx, window):
    n_idx = idx.shape[0]
    n_arr = len(xs)
    mesh = plsc.VectorSubcoreMesh(core_axis_name="core", subcore_axis_name="subcore")

    @pl.kernel(out_type=[jax.ShapeDtypeStruct((n_idx, x.shape[1]), x.dtype) for x in xs], mesh=mesh,
               scratch_types=[], name="sc_gather_rows")
    def gather(*refs):
        x_hbms, i_hbm, o_hbms = refs[:n_arr], refs[n_arr], refs[n_arr + 1:]
        for x_hbm, o_hbm in zip(x_hbms, o_hbms):
            def body(i_vmem, o_vmem, x_hbm=x_hbm):
                pltpu.sync_copy(x_hbm.at[i_vmem.at[0]], o_vmem)

            pltpu.emit_pipeline(
                body,
                grid=(n_idx // window,),
                in_specs=[pl.BlockSpec((1, window), lambda i: (0, i))],
                out_specs=[pl.BlockSpec((window, x_hbm.shape[1]), lambda i: (i, 0))],
                core_axis_name=("core", "subcore"),
                dimension_semantics=(pltpu.PARALLEL,),
            )(i_hbm, o_hbm)

    return gather(*xs, idx.reshape(1, n_idx))


def _moe_plan(top_idx, tm):
    T = top_idx.shape[0]
    n_asg = T * TOP_K
    n_tiles = n_asg // tm + N_EXPERTS
    flat_e = top_idx.reshape(n_asg)
    onehot = (flat_e[:, None] == jnp.arange(N_EXPERTS, dtype=jnp.int32)[None, :]).astype(jnp.int32)
    incl = jnp.cumsum(onehot, axis=0)
    counts = incl[-1]
    rank = jnp.sum((incl - onehot) * onehot, axis=1)
    tiles_e = (counts + tm - 1) // tm
    tile_end = jnp.cumsum(tiles_e)
    row_start_p = (tile_end - tiles_e) * tm
    row_start = jnp.cumsum(counts) - counts
    pos = (row_start_p[flat_e] + rank).reshape(T, TOP_K).T.reshape(n_asg)
    order = jnp.argsort(flat_e, stable=True).astype(jnp.int32)
    tile_ids = jnp.arange(n_tiles, dtype=jnp.int32)
    tile_expert = jnp.minimum(jnp.searchsorted(tile_end, tile_ids, side="right"), N_EXPERTS - 1).astype(jnp.int32)
    r = jnp.arange(n_tiles * tm, dtype=jnp.int32)
    e_r = tile_expert[r // tm]
    j = r - row_start_p[e_r]
    valid = (j < counts[e_r]) & (r // tm < tile_end[-1])
    a_r = order[jnp.clip(row_start[e_r] + j, 0, n_asg - 1)]
    src_tok = jnp.where(valid, a_r // TOP_K, 0).astype(jnp.int32)
    return src_tok, pos.astype(jnp.int32), tile_expert, tile_end[-1:].astype(jnp.int32)


def _moe_ffn_kernel(te_ref, nu_ref, xlo_ref, xhi_ref, wg_ref, wu_ref, wd_ref, olo_ref, ohi_ref, acc_s):
    i = pl.program_id(0)
    c = pl.program_id(1)
    last = pl.num_programs(1) - 1
    used = i < nu_ref[0]
    half = acc_s.shape[1] // 2

    @pl.when(used & (c == 0))
    def _():
        acc_s[...] = jnp.zeros(acc_s.shape, F32)

    @pl.when(used)
    def _():
        x = jnp.concatenate([_unpack_bf16_pairs(xlo_ref[...]), _unpack_bf16_pairs(xhi_ref[...])], axis=1)
        hmid = _silu(_dot(x, wg_ref[0])) * _dot(x, wu_ref[0])
        acc_s[...] += _dot(hmid.astype(BF16), wd_ref[0])

    @pl.when(used & (c == last))
    def _():
        olo_ref[...] = _pack_bf16_pairs(acc_s[:, :half])
        ohi_ref[...] = _pack_bf16_pairs(acc_s[:, half:])

    @pl.when(jnp.logical_not(used) & (c == last))
    def _():
        olo_ref[...] = jnp.zeros(olo_ref.shape, olo_ref.dtype)
        ohi_ref[...] = jnp.zeros(ohi_ref.shape, ohi_ref.dtype)


def moe_expert_ffn(xs_lo, xs_hi, tile_expert, n_used, w_gate, w_up, w_down, tm, fc=896):
    R, half = xs_lo.shape
    D = 4 * half
    E, _, F = w_gate.shape
    rows = lambda i, c, te, nu: (i, 0)
    return pl.pallas_call(
        _moe_ffn_kernel,
        out_shape=(jax.ShapeDtypeStruct((R, half), jnp.uint32), jax.ShapeDtypeStruct((R, half), jnp.uint32)),
        grid_spec=pltpu.PrefetchScalarGridSpec(
            num_scalar_prefetch=2,
            grid=(R // tm, F // fc),
            in_specs=[pl.BlockSpec((tm, half), rows), pl.BlockSpec((tm, half), rows),
                      pl.BlockSpec((1, D, fc), lambda i, c, te, nu: (te[i], 0, c)),
                      pl.BlockSpec((1, D, fc), lambda i, c, te, nu: (te[i], 0, c)),
                      pl.BlockSpec((1, fc, D), lambda i, c, te, nu: (te[i], c, 0))],
            out_specs=(pl.BlockSpec((tm, half), rows), pl.BlockSpec((tm, half), rows)),
            scratch_shapes=[pltpu.VMEM((tm, D), F32)]),
        compiler_params=_cparams("parallel", "arbitrary"),
        name="moe_expert_ffn",
    )(tile_expert, n_used, xs_lo, xs_hi, w_gate.astype(BF16), w_up.astype(BF16), w_down.astype(BF16))


def _moe_combine_kernel(h_ref, lo0_ref, hi0_ref, lo1_ref, hi1_ref, w_ref, g_ref, b_ref, o_ref):
    w = w_ref[...]
    y0 = jnp.concatenate([_unpack_bf16_pairs(lo0_ref[...]), _unpack_bf16_pairs(hi0_ref[...])], axis=1).astype(F32)
    y1 = jnp.concatenate([_unpack_bf16_pairs(lo1_ref[...]), _unpack_bf16_pairs(hi1_ref[...])], axis=1).astype(F32)
    ff = w[:, 0:1] * y0 + w[:, 1:2] * y1
    o_ref[...] = _layer_norm_rows(ALPHA * h_ref[...] + ff, g_ref[...], b_ref[...])


def moe_combine_ln(h, y_lo, y_hi, top_w, g, b, tm=512):
    T, D = h.shape
    half = D // 4
    nt = T // tm
    row = lambda i: (i, 0)
    second = lambda i: (nt + i, 0)
    fixed = lambda i: (0, 0)
    return pl.pallas_call(
        _moe_combine_kernel,
        out_shape=jax.ShapeDtypeStruct((T, D), F32),
        grid=(nt,),
        in_specs=[pl.BlockSpec((tm, D), row),
                  pl.BlockSpec((tm, half), row), pl.BlockSpec((tm, half), row),
                  pl.BlockSpec((tm, half), second), pl.BlockSpec((tm, half), second),
                  pl.BlockSpec((tm, TOP_K), row),
                  pl.BlockSpec((1, D), fixed), pl.BlockSpec((1, D), fixed)],
        out_specs=pl.BlockSpec((tm, D), row),
        compiler_params=_cparams("parallel"),
        name="moe_combine_ln",
    )(h, y_lo, y_hi, y_lo, y_hi, top_w, g.reshape(1, D), b.reshape(1, D))


def moe_ln(h, w_router, w_gate, w_up, w_down, g, b, tm=512):
    top_idx, top_w, hb_lo, hb_hi = moe_route(h, w_router)
    src_tok, pos, tile_expert, n_used = _moe_plan(top_idx, tm)
    xs_lo, xs_hi = sc_gather_rows([hb_lo, hb_hi], src_tok, SC_GATHER_WINDOW)
    ys_lo, ys_hi = moe_expert_ffn(xs_lo, xs_hi, tile_expert, n_used, w_gate, w_up, w_down, tm)
    y_lo, y_hi = sc_gather_rows([ys_lo, ys_hi], pos, SC_GATHER_WINDOW)
    return moe_combine_ln(h, y_lo, y_hi, top_w, g, b)


def _pad_cols(w, width):
    return jnp.pad(w, ((0, 0), (0, width - w.shape[1])))


def even_layer_mixer(h, w_in, ckv_gain, w_uk, w_uv, sinks, w_out, rel_table, ln_g, ln_b, batch, seq):
    cuts = np.cumsum((A_HEADS * HEAD_DIM, A_LATENT, IDX_HEADS * IDX_DIM, IDX_DIM, IDX_HEADS,
                      B_HEADS * HEAD_DIM, B_KV_HEADS * HEAD_DIM))
    w_qa, w_ckv, w_qi, w_ki, w_wi, w_qb, w_kb, w_vb = jnp.split(w_in, [int(c) for c in cuts], axis=1)
    w_main = jnp.concatenate([w_qa, w_qi, w_qb, _pad_cols(w_ki, LANES), w_kb, w_vb], axis=1).astype(BF16)
    w_aux = jnp.concatenate([w_ckv, _pad_cols(w_wi, LANES)], axis=1).astype(BF16)
    proj = matmul(h, w_main, BF16)
    aux = matmul(h, w_aux, F32)
    o_a = dsa_attention(proj, aux, ckv_gain, w_uk, w_uv, rel_table[:, :A_HEADS], batch, seq)
    o_b = banded_attention(proj, 2, 13, 14, B_HEADS, B_KV_HEADS,
                           _band_bias(rel_table[:, A_HEADS:A_HEADS + B_HEADS], B_WINDOW), B_WINDOW,
                           batch, seq, sinks=sinks)
    o = jnp.concatenate([o_a, o_b], axis=1)
    return matmul_residual_ln([o], w_out.astype(BF16), h, ln_g, ln_b)


def odd_layer_mixer(h, w_in, pe_k, w1_k, w2_k, pe_v, w1_v, w2_v, w_out, rel_table, ln_g, ln_b, batch, seq):
    kvw = C_KV_HEADS * HEAD_DIM
    qw = C_HEADS * HEAD_DIM
    w_main = w_in[:, :qw + 6 * kvw].astype(BF16)
    w_gl = w_in[:, qw + 6 * kvw:].reshape(D_MODEL, C_HEADS, 3)
    w_gate = jnp.concatenate([_pad_cols(w_gl[:, :, r], LANES) for r in range(3)], axis=1).astype(BF16)
    proj = matmul(h, w_main, BF16)
    gates = matmul(h, w_gate, F32)
    tbl = rel_table[:, :C_HEADS]
    kc = compress_blocks(proj[:, qw:qw + kvw], pe_k, w1_k, w2_k, batch, seq)
    vc = compress_blocks(proj[:, qw + kvw:qw + 2 * kvw], pe_v, w1_v, w2_v, batch, seq)
    o_cmp, sel = cmp_attention(proj, kc, vc, gates, tbl, batch, seq)
    o_slc = slc_attention(proj, sel, gates, tbl, batch, seq)
    o_win = banded_attention(proj, 0, 12, 13, C_HEADS, C_KV_HEADS, _band_bias(tbl, C_WINDOW), C_WINDOW,
                             batch, seq, gates=gates, gate_col=2)
    return matmul_residual_ln([o_cmp, o_slc, o_win], w_out.astype(BF16), h, ln_g, ln_b)


def kernel(x, mem, rel_table, ev_w_in, ev_ckv_gain, ev_w_uk, ev_w_uv, ev_sinks, ev_w_out, od_w_in, od_pe_k, od_w1_k, od_w2_k, od_pe_v, od_w1_v, od_w2_v, od_w_out, xa_w_q, xa_w_k, xa_w_v, xa_w_o, ff_w_gate, ff_w_up, ff_w_down, moe_w_router, moe_w_gate, moe_w_up, moe_w_down, ln_g, ln_b):
    batch, seq, D = x.shape
    h = x.reshape(batch * seq, D)
    mem2 = mem.reshape(-1, D)
    for i in range(DEPTH):
        j = i // 2
        if i % 2 == 0:
            h = even_layer_mixer(h, ev_w_in[j], ev_ckv_gain[j], ev_w_uk[j], ev_w_uv[j], ev_sinks[j],
                                 ev_w_out[j], rel_table, ln_g[i, 0], ln_b[i, 0], batch, seq)
        else:
            h = odd_layer_mixer(h, od_w_in[j], od_pe_k[j], od_w1_k[j], od_w2_k[j], od_pe_v[j],
                                od_w1_v[j], od_w2_v[j], od_w_out[j], rel_table, ln_g[i, 0], ln_b[i, 0],
                                batch, seq)
        h = cross_attention_ln(h, mem2, xa_w_q[i], xa_w_k[i], xa_w_v[i], xa_w_o[i],
                               ln_g[i, 1], ln_b[i, 1], batch, seq)
        if i % 2 == 0:
            h = swiglu_ln(h, ff_w_gate[j], ff_w_up[j], ff_w_down[j], ln_g[i, 2], ln_b[i, 2])
        else:
            h = moe_ln(h, moe_w_router[j], moe_w_gate[j], moe_w_up[j], moe_w_down[j], ln_g[i, 2], ln_b[i, 2])
    return h.reshape(batch, seq, D)
```

```python
import functools
import math

import numpy as np
import jax
import jax.numpy as jnp
from jax import lax
from jax.experimental import pallas as pl
from jax.experimental.pallas import tpu as pltpu
from jax.experimental.pallas import tpu_sc as plsc

D_MODEL = 1024
DEPTH = 2
HEAD_DIM = 64
N_SLOTS = D_MODEL // HEAD_DIM
A_HEADS = 8
A_LATENT = 128
IDX_HEADS = 8
IDX_DIM = 64
A_TOPK_MAX = 256
B_HEADS = 8
B_KV_HEADS = 2
B_WINDOW = 128
C_HEADS = 16
C_KV_HEADS = 2
CMP_BLOCK = 32
CMP_STRIDE = 16
CMP_HIDDEN = 128
SLC_BLOCK = 64
SLC_COUNT = 8
C_WINDOW = 512
X_HEADS = 4
X_HEAD_DIM = 128
D_FF = 2816
N_EXPERTS = 8
TOP_K = 2
D_FF_EXPERT = 3584
REL_BUCKETS = 32
REL_MAX_DIST = 128
LN_EPS = 1e-5
NEG_INF = -1e30
FORCE_BONUS = 1e6
ALPHA = (2 * DEPTH) ** 0.25
MASK_BIG = 2.0 ** 100

LANES = 128
QB = 128
VMEM_LIMIT_BYTES = 56 * 1024 * 1024
SC_GATHER_WINDOW = 128

F32 = jnp.float32
BF16 = jnp.bfloat16


def _cparams(*sem):
    return pltpu.CompilerParams(dimension_semantics=sem, vmem_limit_bytes=VMEM_LIMIT_BYTES)


def _dot(a, b):
    return jnp.dot(a, b, preferred_element_type=F32)


def _dot_nt(a, b):
    return lax.dot_general(a, b, (((1,), (1,)), ((), ())), preferred_element_type=F32)


def _layer_norm_rows(y, g, b):
    mu = jnp.mean(y, axis=-1, keepdims=True)
    yc = y - mu
    var = jnp.mean(yc * yc, axis=-1, keepdims=True)
    return yc * lax.rsqrt(var + LN_EPS) * g + b


def _mm_kernel(x_ref, w_ref, o_ref):
    o_ref[...] = _dot(x_ref[...].astype(BF16), w_ref[...]).astype(o_ref.dtype)


def matmul(x, w, out_dtype, tm=512):
    M, K = x.shape
    N = w.shape[1]
    tm = min(tm, M)
    return pl.pallas_call(
        _mm_kernel,
        out_shape=jax.ShapeDtypeStruct((M, N), out_dtype),
        grid=(M // tm,),
        in_specs=[pl.BlockSpec((tm, K), lambda i: (i, 0)),
                  pl.BlockSpec((K, N), lambda i: (0, 0))],
        out_specs=pl.BlockSpec((tm, N), lambda i: (i, 0)),
        compiler_params=_cparams("parallel"),
        name="matmul",
    )(x, w)


def _mm_res_ln_kernel(*refs, n_in):
    a_refs = refs[:n_in]
    w_ref, h_ref, g_ref, b_ref, o_ref = refs[n_in:]
    a = a_refs[0][...]
    if n_in > 1:
        a = a.astype(F32)
        for r in a_refs[1:]:
            a = a + r[...].astype(F32)
    y = ALPHA * h_ref[...] + _dot(a.astype(BF16), w_ref[...])
    o_ref[...] = _layer_norm_rows(y, g_ref[...], b_ref[...])


def matmul_residual_ln(a_list, w, h, g, b, tm=512):
    M, K = a_list[0].shape
    N = w.shape[1]
    n_in = len(a_list)
    row = lambda i: (i, 0)
    fixed = lambda i: (0, 0)
    return pl.pallas_call(
        functools.partial(_mm_res_ln_kernel, n_in=n_in),
        out_shape=jax.ShapeDtypeStruct((M, N), F32),
        grid=(M // tm,),
        in_specs=[pl.BlockSpec((tm, K), row)] * n_in + [
            pl.BlockSpec((K, N), fixed), pl.BlockSpec((tm, N), row),
            pl.BlockSpec((1, N), fixed), pl.BlockSpec((1, N), fixed)],
        out_specs=pl.BlockSpec((tm, N), row),
        compiler_params=_cparams("parallel"),
        name="matmul_residual_ln",
    )(*a_list, w, h, g.reshape(1, N), b.reshape(1, N))


def _rel_bucket(dist):
    n = jnp.maximum(dist, 0)
    max_exact = REL_BUCKETS // 2
    nf = jnp.maximum(n, 1).astype(F32)
    log_b = max_exact + (jnp.log(nf / max_exact) / math.log(REL_MAX_DIST / max_exact)
                         * (REL_BUCKETS - max_exact)).astype(jnp.int32)
    return jnp.where(n < max_exact, n, jnp.minimum(log_b, REL_BUCKETS - 1))


def _bias_lookup(tbl, dist):
    onehot = (_rel_bucket(dist)[..., None] == jnp.arange(REL_BUCKETS, dtype=jnp.int32)).astype(F32)
    return jnp.einsum("...r,rh->...h", onehot, tbl.astype(F32), precision=lax.Precision.HIGHEST)


def _tile_bias(tbl):
    i = np.arange(QB)[:, None]
    j = np.arange(QB)[None, :]
    dist = jnp.asarray(np.stack([i - j, QB + i - j, 2 * QB + i - j]))
    return _bias_lookup(tbl, dist).transpose(3, 0, 1, 2)


def _tile_bias_masked_t(tbl, n_tiles, window=None):
    j = np.arange(QB)[:, None]
    i = np.arange(QB)[None, :]
    dist = np.stack([d * QB + i - j for d in range(n_tiles)])
    valid = dist >= 0
    if window is not None:
        valid &= dist < window
    bias = jnp.where(jnp.asarray(valid)[..., None], _bias_lookup(tbl, jnp.asarray(dist)), NEG_INF)
    return bias.transpose(3, 0, 1, 2)


def _flash_tile(s, add, m_ref, l_ref, acc_ref, vt, hpg):
    scale, probs = [], []
    for hi in range(hpg):
        sl = slice(hi * QB, (hi + 1) * QB)
        sh = s[:, sl] + add(hi)
        m_old = m_ref[:, sl]
        m_new = jnp.maximum(m_old, jnp.max(sh, axis=0, keepdims=True))
        a = jnp.exp(m_old - m_new)
        p = jnp.exp(sh - m_new)
        l_ref[:, sl] = a * l_ref[:, sl] + jnp.sum(p, axis=0, keepdims=True)
        m_ref[:, sl] = m_new
        scale.append(a)
        probs.append(p.astype(BF16))
    pv = _dot(vt, jnp.concatenate(probs, axis=1))
    for hi in range(hpg):
        sl = slice(hi * QB, (hi + 1) * QB)
        acc_ref[:, sl] = scale[hi] * acc_ref[:, sl] + pv[:, sl]


def _for_tiles_pairwise(n_tiles, tile):
    def pair(i, carry):
        tile(2 * i)
        tile(2 * i + 1)
        return carry

    lax.fori_loop(0, lax.shift_right_logical(n_tiles, 1), pair, 0)

    @pl.when((n_tiles & 1) == 1)
    def _():
        tile(n_tiles - 1)


def _stack_heads(q, first, count, scale):
    return jnp.concatenate([q[:, (first + hi) * HEAD_DIM:(first + hi + 1) * HEAD_DIM] * scale
                            for hi in range(count)], axis=0)


def _store_heads_t(o_ref, acc_ref, l_ref, first, count, gate_t):
    for hi in range(count):
        h = first + hi
        sl = slice(hi * QB, (hi + 1) * QB)
        o = acc_ref[:, sl] / l_ref[:, sl]
        if gate_t is not None:
            o = o * (1.0 / (1.0 + jnp.exp(-gate_t[h:h + 1, :])))
        o_ref[:, h * HEAD_DIM:(h + 1) * HEAD_DIM] = o.T.astype(o_ref.dtype)


def _band_kernel(*refs, n_tiles, kvh, g, has_sinks, has_gate):
    it = iter(refs)
    q_ref, k_ref, v_ref, bias_ref = next(it), next(it), next(it), next(it)
    sink_ref = next(it) if has_sinks else None
    gate_ref = next(it) if has_gate else None
    o_ref, vt_s, m_s, l_s, acc_s = next(it), next(it), next(it), next(it), next(it)
    n = pl.program_id(1)
    nq = vt_s.shape[0]

    @pl.when(n == 0)
    def _():
        for kt in range(nq):
            vt_s[kt] = v_ref[kt * QB:(kt + 1) * QB, :].astype(F32).T.astype(BF16)

    q = q_ref[...]
    qst = [_stack_heads(q, kv * g, g, HEAD_DIM ** -0.5) for kv in range(kvh)]
    if has_sinks:
        m_s[...] = sink_ref[...]
        l_s[...] = jnp.ones(l_s.shape, F32)
    else:
        m_s[...] = jnp.full(m_s.shape, NEG_INF, F32)
        l_s[...] = jnp.zeros(l_s.shape, F32)
    acc_s[...] = jnp.zeros(acc_s.shape, F32)

    def tile(d):
        kt = n - d
        k_t = k_ref[pl.ds(pl.multiple_of(kt * QB, QB), QB), :]
        vt_t = vt_s[kt]
        for kv in range(kvh):
            s = _dot_nt(k_t[:, kv * HEAD_DIM:(kv + 1) * HEAD_DIM], qst[kv])
            _flash_tile(s, lambda hi, kv=kv: bias_ref[kv * g + hi, d], m_s.at[kv], l_s.at[kv],
                        acc_s.at[kv], vt_t[kv * HEAD_DIM:(kv + 1) * HEAD_DIM, :], g)

    @pl.when(n >= n_tiles - 1)
    def _():
        for d in range(n_tiles):
            tile(d)

    @pl.when(n < n_tiles - 1)
    def _():
        for d in range(n_tiles - 1):
            @pl.when(n >= d)
            def _(d=d):
                tile(d)

    gate_t = gate_ref[...].T if has_gate else None
    for kv in range(kvh):
        _store_heads_t(o_ref, acc_s.at[kv], l_s.at[kv], kv * g, g, gate_t)


def banded_attention(proj, q_col, k_col, v_col, n_heads, kvh, tbl, window, batch, seq,
                     sinks=None, gates=None, gate_col=0):
    g = n_heads // kvh
    kw = kvh * HEAD_DIM
    qw = n_heads * HEAD_DIM
    nq = seq // QB
    n_tiles = window // QB + 1
    args = [proj, proj, proj, _tile_bias_masked_t(tbl, n_tiles, window)]
    in_specs = [pl.BlockSpec((QB, qw), lambda b, n: (b * nq + n, q_col)),
                pl.BlockSpec((seq, kw), lambda b, n: (b, k_col)),
                pl.BlockSpec((seq, kw), lambda b, n: (b, v_col)),
                pl.BlockSpec((n_heads, n_tiles, QB, QB), lambda b, n: (0, 0, 0, 0))]
    if sinks is not None:
        args.append(jnp.repeat(sinks.astype(F32).reshape(kvh, 1, g), QB, axis=2))
        in_specs.append(pl.BlockSpec((kvh, 1, g * QB), lambda b, n: (0, 0, 0)))
    if gates is not None:
        args.append(gates)
        in_specs.append(pl.BlockSpec((QB, LANES), lambda b, n: (b * nq + n, gate_col)))
    return pl.pallas_call(
        functools.partial(_band_kernel, n_tiles=n_tiles, kvh=kvh, g=g,
                          has_sinks=sinks is not None, has_gate=gates is not None),
        out_shape=jax.ShapeDtypeStruct((batch * seq, qw), BF16),
        grid=(batch, nq),
        in_specs=in_specs,
        out_specs=pl.BlockSpec((QB, qw), lambda b, n: (b * nq + n, 0)),
        scratch_shapes=[pltpu.VMEM((nq, kw, QB), BF16),
                        pltpu.VMEM((kvh, 1, g * QB), F32), pltpu.VMEM((kvh, 1, g * QB), F32),
                        pltpu.VMEM((kvh, HEAD_DIM, g * QB), F32)],
        compiler_params=_cparams("parallel", "arbitrary"),
        name="banded_attention",
    )(*args)


def _sort_key(x):
    bits = lax.bitcast_convert_type(x, jnp.int32)
    return bits ^ ((bits >> 31) & jnp.int32(0x7FFFFFFF))


def _dsa_kernel(qa_ref, qi_ref, ki_ref, ckv_ref, wi_ref, gain_ref, wuk_ref, wuv_ref, bias_ref,
                o_ref, c_s, ct_s, key_s, m_s, l_s, acc_s, *, topk):
    n = pl.program_id(1)
    nt = n + 1
    nq = c_s.shape[0]

    @pl.when(n == 0)
    def _():
        for kt in range(nq):
            c = ckv_ref[kt * QB:(kt + 1) * QB, :]
            y = c * lax.rsqrt(jnp.mean(c * c, axis=-1, keepdims=True) + LN_EPS) * gain_ref[...]
            c_s[kt] = y.astype(BF16)
            ct_s[kt] = y.T.astype(BF16)

    qa = qa_ref[...]
    qi = qi_ref[...]
    qlat = jnp.concatenate(
        [(_dot(qa[:, h * HEAD_DIM:(h + 1) * HEAD_DIM], wuk_ref[h]) * HEAD_DIM ** -0.5).astype(BF16)
         for h in range(A_HEADS)], axis=0)
    qidx = jnp.concatenate(
        [qi[:, h * IDX_DIM:(h + 1) * IDX_DIM] * IDX_DIM ** -0.5 for h in range(IDX_HEADS)], axis=0)
    wi_t = wi_ref[...].T
    wi_row = jnp.concatenate([wi_t[h:h + 1, :] for h in range(IDX_HEADS)], axis=1) * IDX_HEADS ** -0.5

    kpos0 = lax.broadcasted_iota(jnp.int32, (QB, QB), 0)
    qpos = n * QB + lax.broadcasted_iota(jnp.int32, (QB, QB), 1)

    def score_tile(kt):
        k_t = ki_ref[pl.ds(pl.multiple_of(kt * QB, QB), QB), :][:, :IDX_DIM]
        act = jnp.maximum(_dot_nt(k_t, qidx), 0.0) * wi_row
        sc = act[:, :QB]
        for h in range(1, IDX_HEADS):
            sc = sc + act[:, h * QB:(h + 1) * QB]
        sc = jnp.where(kt * QB + kpos0 <= qpos, sc, NEG_INF)
        sc = jnp.where(sc == 0.0, 0.0, sc)
        key_s[kt] = _sort_key(sc)

    _for_tiles_pairwise(nt, score_tile)

    def count(pred):
        def hits(kt):
            return jnp.where(pred(kt, key_s[kt]), 1.0, 0.0)

        def pair(i, acc):
            return acc + (hits(2 * i) + hits(2 * i + 1))
        acc = lax.fori_loop(0, lax.shift_right_logical(nt, 1), pair, jnp.zeros((QB, QB), F32))
        acc = acc + jnp.where((nt & 1) == 1, hits(nt - 1), 0.0)
        return jnp.sum(acc, axis=0, keepdims=True)

    def thr_step(i, lo):
        cand = lo + (jnp.int32(1) << (31 - i))
        cnt = count(lambda kt, key: key >= cand)
        return jnp.where(cnt >= topk, cand, lo)

    thr = lax.fori_loop(0, 32, thr_step, jnp.full((1, QB), -2 ** 31, jnp.int32))
    need = topk - count(lambda kt, key: key > thr)

    def tie_step(i, j0):
        cand = j0 + (jnp.int32(1) << (10 - i))
        cnt = count(lambda kt, key: (key == thr) & (kt * QB + kpos0 < cand))
        return jnp.where(cnt < need, cand, j0)

    j0 = lax.fori_loop(0, 11, tie_step, jnp.zeros((1, QB), jnp.int32))

    m_s[...] = jnp.full(m_s.shape, NEG_INF, F32)
    l_s[...] = jnp.zeros(l_s.shape, F32)
    acc_s[...] = jnp.zeros(acc_s.shape, F32)

    def attend_tile(kt):
        key = key_s[kt]
        kpos = kt * QB + kpos0
        ok = ((key > thr) | ((key == thr) & (kpos <= j0))) & (kpos <= qpos)
        d = jnp.minimum(n - kt, 2)
        s = _dot_nt(c_s[kt], qlat)
        scale, probs = [], []
        for h in range(A_HEADS):
            sl = slice(h * QB, (h + 1) * QB)
            sh = jnp.where(ok, s[:, sl] + bias_ref[h, d], NEG_INF)
            m_old = m_s[:, sl]
            m_new = jnp.maximum(m_old, jnp.max(sh, axis=0, keepdims=True))
            a = jnp.exp(m_old - m_new)
            p = jnp.where(ok, jnp.exp(sh - m_new), 0.0)
            l_s[:, sl] = a * l_s[:, sl] + jnp.sum(p, axis=0, keepdims=True)
            m_s[:, sl] = m_new
            scale.append(a)
            probs.append(p.astype(BF16))
        pv = _dot(ct_s[kt], jnp.concatenate(probs, axis=1))
        for h in range(A_HEADS):
            sl = slice(h * QB, (h + 1) * QB)
            acc_s[:, sl] = scale[h] * acc_s[:, sl] + pv[:, sl]

    _for_tiles_pairwise(nt, attend_tile)

    for h in range(A_HEADS):
        sl = slice(h * QB, (h + 1) * QB)
        o_lat = (acc_s[:, sl] / l_s[:, sl]).T
        o_ref[:, h * HEAD_DIM:(h + 1) * HEAD_DIM] = _dot(o_lat.astype(BF16), wuv_ref[h]).astype(o_ref.dtype)


def dsa_attention(proj, aux, gain, w_uk, w_uv, tbl, batch, seq):
    nq = seq // QB
    topk = min(A_TOPK_MAX, seq // 4)
    qw = A_HEADS * HEAD_DIM
    return pl.pallas_call(
        functools.partial(_dsa_kernel, topk=topk),
        out_shape=jax.ShapeDtypeStruct((batch * seq, qw), BF16),
        grid=(batch, nq),
        in_specs=[
            pl.BlockSpec((QB, qw), lambda b, n: (b * nq + n, 0)),
            pl.BlockSpec((QB, qw), lambda b, n: (b * nq + n, 1)),
            pl.BlockSpec((seq, LANES), lambda b, n: (b, 12)),
            pl.BlockSpec((seq, A_LATENT), lambda b, n: (b, 0)),
            pl.BlockSpec((QB, LANES), lambda b, n: (b * nq + n, 1)),
            pl.BlockSpec((1, A_LATENT), lambda b, n: (0, 0)),
            pl.BlockSpec((A_HEADS, HEAD_DIM, A_LATENT), lambda b, n: (0, 0, 0)),
            pl.BlockSpec((A_HEADS, A_LATENT, HEAD_DIM), lambda b, n: (0, 0, 0)),
            pl.BlockSpec((A_HEADS, 3, QB, QB), lambda b, n: (0, 0, 0, 0)),
        ],
        out_specs=pl.BlockSpec((QB, qw), lambda b, n: (b * nq + n, 0)),
        scratch_shapes=[
            pltpu.VMEM((nq, QB, A_LATENT), BF16),
            pltpu.VMEM((nq, A_LATENT, QB), BF16),
            pltpu.VMEM((nq, QB, QB), jnp.int32),
            pltpu.VMEM((1, A_HEADS * QB), F32),
            pltpu.VMEM((1, A_HEADS * QB), F32),
            pltpu.VMEM((A_LATENT, A_HEADS * QB), F32),
        ],
        compiler_params=_cparams("parallel", "arbitrary"),
        name="dsa_attention",
    )(proj, proj, proj, aux, aux, gain.reshape(1, A_LATENT).astype(F32),
      w_uk.astype(BF16), w_uv.astype(BF16), _tile_bias(tbl).swapaxes(2, 3))


def _compress_kernel(x_ref, pe_ref, w1_ref, w2_ref, o_ref):
    half = CMP_STRIDE * HEAD_DIM
    x = x_ref[0]
    w1 = w1_ref[...]
    u = _dot(x, w1[:half])
    v = _dot(x, w1[half:])
    pre = u + pltpu.roll(v, v.shape[0] - 1, 0) + _dot(pe_ref[...], w1)[:1]
    h = 0.5 * pre * (1.0 + jnp.tanh(math.sqrt(2.0 / math.pi) * (pre + 0.044715 * pre * pre * pre)))
    o_ref[0] = _dot(h.astype(BF16), w2_ref[...]).astype(o_ref.dtype)


def compress_blocks(kv, pe, w1, w2, batch, seq):
    G = C_KV_HEADS
    nch = seq // CMP_STRIDE
    half = CMP_STRIDE * HEAD_DIM
    x = kv.reshape(batch, nch, CMP_STRIDE, G, HEAD_DIM).transpose(0, 3, 1, 2, 4)
    x = x.reshape(batch * G, nch, half)
    pe_flat = jnp.broadcast_to(pe.reshape(1, CMP_BLOCK * HEAD_DIM), (8, CMP_BLOCK * HEAD_DIM))
    return pl.pallas_call(
        _compress_kernel,
        out_shape=jax.ShapeDtypeStruct((batch * G, nch, HEAD_DIM), BF16),
        grid=(batch * G,),
        in_specs=[pl.BlockSpec((1, nch, half), lambda i: (i, 0, 0)),
                  pl.BlockSpec((8, 2 * half), lambda i: (0, 0)),
                  pl.BlockSpec((2 * half, CMP_HIDDEN), lambda i: (0, 0)),
                  pl.BlockSpec((CMP_HIDDEN, HEAD_DIM), lambda i: (0, 0))],
        out_specs=pl.BlockSpec((1, nch, HEAD_DIM), lambda i: (i, 0, 0)),
        compiler_params=_cparams("parallel"),
        name="compress_blocks",
    )(x, pe_flat.astype(BF16), w1.astype(BF16), w2.astype(BF16))


def _cmp_kernel(q_ref, kc_ref, vc_ref, bias_ref, gate_ref, ovt_ref, o_ref, pick_ref, *, n_cmp, n_slc, n_sel):
    n = pl.program_id(1)
    G = C_KV_HEADS
    hpg = C_HEADS // G
    nch = kc_ref.shape[1]
    c = lax.broadcasted_iota(jnp.int32, (nch, QB), 0)
    t = n * QB + lax.broadcasted_iota(jnp.int32, (nch, QB), 1)
    valid = (t - (c * CMP_STRIDE + CMP_BLOCK - 1) >= 0) & (c < n_cmp)
    jj = lax.broadcasted_iota(jnp.int32, (n_slc, QB), 0)
    ts = n * QB + lax.broadcasted_iota(jnp.int32, (n_slc, QB), 1)
    blk = ts // SLC_BLOCK
    bonus = FORCE_BONUS * jnp.where((jj == 0) | (jj == blk) | (jj == blk - 1), 1.0, 0.0)
    admissible = jj * SLC_BLOCK <= ts
    q = q_ref[...]
    gate_t = gate_ref[...].T
    for g in range(G):
        s = _dot_nt(kc_ref[g], _stack_heads(q, g * hpg, hpg, HEAD_DIM ** -0.5))
        vct = vc_ref[g].astype(F32).T.astype(BF16)
        psum = jnp.zeros((nch, QB), F32)
        probs = []
        for hi in range(hpg):
            sl = slice(hi * QB, (hi + 1) * QB)
            sh = jnp.where(valid, s[:, sl] + bias_ref[g * hpg + hi], NEG_INF)
            e = jnp.where(valid, jnp.exp(sh - jnp.max(sh, axis=0, keepdims=True)), 0.0)
            den = jnp.sum(e, axis=0, keepdims=True)
            p = e * (1.0 / jnp.where(den > 0.0, den, 1.0))
            psum = psum + p
            probs.append(p.astype(BF16))
        pv = _dot(vct, jnp.concatenate(probs, axis=1))
        for hi in range(hpg):
            h = g * hpg + hi
            o = pv[:, hi * QB:(hi + 1) * QB] * (1.0 / (1.0 + jnp.exp(-gate_t[h:h + 1, :])))
            o_ref[:, h * HEAD_DIM:(h + 1) * HEAD_DIM] = o.T.astype(o_ref.dtype)
        p_hi = psum.astype(BF16)
        p_lo = (psum - p_hi.astype(F32)).astype(BF16)
        p_slc = _dot(ovt_ref[...], p_hi) + _dot(ovt_ref[...], p_lo)
        score = jnp.where(admissible, p_slc + bonus, NEG_INF)
        rank = jnp.zeros((n_slc, QB), F32)
        for i in range(n_slc):
            si = score[i:i + 1, :]
            rank = rank + jnp.where(si > score, 1.0, 0.0) + jnp.where((si == score) & (i < jj), 1.0, 0.0)
        pick_ref[0, g] = jnp.where(rank < n_sel, 0.0, -MASK_BIG).astype(pick_ref.dtype)


def cmp_attention(proj, kc, vc, gates, tbl, batch, seq):
    nq = seq // QB
    G = C_KV_HEADS
    nch = seq // CMP_STRIDE
    n_cmp = (seq - CMP_BLOCK) // CMP_STRIDE + 1
    n_slc = seq // SLC_BLOCK
    n_sel = min(SLC_COUNT, n_slc)
    qw = C_HEADS * HEAD_DIM
    cmp_end = np.arange(nch) * CMP_STRIDE + CMP_BLOCK - 1
    dist_c = jnp.asarray(np.arange(seq)[None, :] - cmp_end[:, None])
    bias_c = _bias_lookup(tbl, dist_c).transpose(2, 0, 1)
    cs = np.arange(nch)[None, :] * CMP_STRIDE
    ss = np.arange(n_slc)[:, None] * SLC_BLOCK
    overlap_t = ((cs < ss + SLC_BLOCK) & (cs + CMP_BLOCK > ss) & (np.arange(nch)[None, :] < n_cmp))
    return pl.pallas_call(
        functools.partial(_cmp_kernel, n_cmp=n_cmp, n_slc=n_slc, n_sel=n_sel),
        out_shape=(jax.ShapeDtypeStruct((batch * seq, qw), BF16),
                   jax.ShapeDtypeStruct((batch, G, n_slc, seq), BF16)),
        grid=(batch, nq),
        in_specs=[
            pl.BlockSpec((QB, qw), lambda b, n: (b * nq + n, 0)),
            pl.BlockSpec((G, nch, HEAD_DIM), lambda b, n: (b, 0, 0)),
            pl.BlockSpec((G, nch, HEAD_DIM), lambda b, n: (b, 0, 0)),
            pl.BlockSpec((C_HEADS, nch, QB), lambda b, n: (0, 0, n)),
            pl.BlockSpec((QB, LANES), lambda b, n: (b * nq + n, 0)),
            pl.BlockSpec((n_slc, nch), lambda b, n: (0, 0)),
        ],
        out_specs=(pl.BlockSpec((QB, qw), lambda b, n: (b * nq + n, 0)),
                   pl.BlockSpec((1, G, n_slc, QB), lambda b, n: (b, 0, 0, n))),
        compiler_params=_cparams("parallel", "parallel"),
        name="cmp_attention",
    )(proj, kc, vc, bias_c, gates, jnp.asarray(overlap_t.astype(np.float32)).astype(BF16))


def _slc_kernel(q_ref, k_ref, v_ref, pick_ref, bias_ref, gate_ref, o_ref, vt_s, m_s, l_s, acc_s, *, n_slc):
    n = pl.program_id(1)
    G = C_KV_HEADS
    hpg = C_HEADS // G
    nq = vt_s.shape[0]

    @pl.when(n == 0)
    def _():
        for kt in range(nq):
            vt_s[kt] = v_ref[kt * QB:(kt + 1) * QB, :].astype(F32).T.astype(BF16)

    q = q_ref[...]
    qst = [_stack_heads(q, g * hpg, hpg, HEAD_DIM ** -0.5) for g in range(G)]
    pickst = [jnp.concatenate([pick_ref[0, g]] * hpg, axis=1) for g in range(G)]
    ej = lax.broadcasted_iota(jnp.int32, (QB, n_slc), 0)
    eb = lax.broadcasted_iota(jnp.int32, (QB, n_slc), 1)
    m_s[...] = jnp.full(m_s.shape, NEG_INF, F32)
    l_s[...] = jnp.zeros(l_s.shape, F32)
    acc_s[...] = jnp.zeros(acc_s.shape, F32)

    def tile(kt):
        k_t = k_ref[pl.ds(pl.multiple_of(kt * QB, QB), QB), :]
        vt_t = vt_s[kt]
        expand = jnp.where(eb == (kt * QB + ej) // SLC_BLOCK, 1.0, 0.0).astype(BF16)
        d = jnp.minimum(n - kt, 2)
        for g in range(G):
            s = _dot_nt(k_t[:, g * HEAD_DIM:(g + 1) * HEAD_DIM], qst[g]) + _dot(expand, pickst[g])
            _flash_tile(s, lambda hi, g=g: bias_ref[g * hpg + hi, d], m_s.at[g], l_s.at[g], acc_s.at[g],
                        vt_t[g * HEAD_DIM:(g + 1) * HEAD_DIM, :], hpg)

    _for_tiles_pairwise(n + 1, tile)

    gate_t = gate_ref[...].T
    for g in range(G):
        _store_heads_t(o_ref, acc_s.at[g], l_s.at[g], g * hpg, hpg, gate_t)


def slc_attention(proj, pick, gates, tbl, batch, seq):
    nq = seq // QB
    G = C_KV_HEADS
    n_slc = seq // SLC_BLOCK
    qw = C_HEADS * HEAD_DIM
    return pl.pallas_call(
        functools.partial(_slc_kernel, n_slc=n_slc),
        out_shape=jax.ShapeDtypeStruct((batch * seq, qw), BF16),
        grid=(batch, nq),
        in_specs=[
            pl.BlockSpec((QB, qw), lambda b, n: (b * nq + n, 0)),
            pl.BlockSpec((seq, LANES), lambda b, n: (b, 10)),
            pl.BlockSpec((seq, LANES), lambda b, n: (b, 11)),
            pl.BlockSpec((1, G, n_slc, QB), lambda b, n: (b, 0, 0, n)),
            pl.BlockSpec((C_HEADS, 3, QB, QB), lambda b, n: (0, 0, 0, 0)),
            pl.BlockSpec((QB, LANES), lambda b, n: (b * nq + n, 1)),
        ],
        out_specs=pl.BlockSpec((QB, qw), lambda b, n: (b * nq + n, 0)),
        scratch_shapes=[pltpu.VMEM((nq, G * HEAD_DIM, QB), BF16),
                        pltpu.VMEM((G, 1, C_HEADS // G * QB), F32),
                        pltpu.VMEM((G, 1, C_HEADS // G * QB), F32),
                        pltpu.VMEM((G, HEAD_DIM, C_HEADS // G * QB), F32)],
        compiler_params=_cparams("parallel", "arbitrary"),
        name="slc_attention",
    )(proj, proj, proj, pick, _tile_bias_masked_t(tbl, 3), gates)


def _xattn_kernel(h_ref, wq_ref, k_ref, v_ref, wo_ref, g_ref, b_ref, o_ref):
    x = h_ref[...]
    q = _dot(x.astype(BF16), wq_ref[...]).astype(BF16)
    k = k_ref[...]
    v = v_ref[...]
    outs = []
    for hd in range(X_HEADS):
        sl = slice(hd * X_HEAD_DIM, (hd + 1) * X_HEAD_DIM)
        lg = _dot_nt(q[:, sl], k[:, sl]) * X_HEAD_DIM ** -0.5
        e = jnp.exp(lg - jnp.max(lg, axis=-1, keepdims=True))
        den = jnp.sum(e, axis=-1, keepdims=True)
        outs.append((_dot(e.astype(BF16), v[:, sl]) / den).astype(BF16))
    o = jnp.concatenate(outs, axis=-1)
    y = ALPHA * x + _dot(o, wo_ref[...])
    o_ref[...] = _layer_norm_rows(y, g_ref[...], b_ref[...])


def cross_attention_ln(h, mem, w_q, w_k, w_v, w_o, g, b, batch, seq, tm=256):
    M = mem.shape[0] // batch
    XW = X_HEADS * X_HEAD_DIM
    D = D_MODEL
    kv = matmul(mem, jnp.concatenate([w_k, w_v], axis=1).astype(BF16), BF16, tm=512)
    nt = seq // tm
    fixed = lambda bb, i: (0, 0)
    return pl.pallas_call(
        _xattn_kernel,
        out_shape=jax.ShapeDtypeStruct((batch * seq, D), F32),
        grid=(batch, nt),
        in_specs=[pl.BlockSpec((tm, D), lambda bb, i: (bb * nt + i, 0)),
                  pl.BlockSpec((D, XW), fixed),
                  pl.BlockSpec((M, XW), lambda bb, i: (bb, 0)),
                  pl.BlockSpec((M, XW), lambda bb, i: (bb, 1)),
                  pl.BlockSpec((XW, D), fixed),
                  pl.BlockSpec((1, D), fixed), pl.BlockSpec((1, D), fixed)],
        out_specs=pl.BlockSpec((tm, D), lambda bb, i: (bb * nt + i, 0)),
        compiler_params=_cparams("parallel", "parallel"),
        name="cross_attention_ln",
    )(h, w_q.astype(BF16), kv, kv, w_o.astype(BF16), g.reshape(1, D), b.reshape(1, D))


def _silu(a):
    return a * (1.0 / (1.0 + jnp.exp(-a)))


def _swiglu_kernel(x_ref, wg_ref, wu_ref, wd_ref, g_ref, b_ref, o_ref):
    x = x_ref[...]
    xb = x.astype(BF16)
    hmid = (_silu(_dot(xb, wg_ref[...])) * _dot(xb, wu_ref[...])).astype(BF16)
    y = ALPHA * x + _dot(hmid, wd_ref[...])
    o_ref[...] = _layer_norm_rows(y, g_ref[...], b_ref[...])


def swiglu_ln(h, w_gate, w_up, w_down, g, b, tm=256):
    T, D = h.shape
    F = w_gate.shape[1]
    fixed = lambda i: (0, 0)
    once = pl.Buffered(1)
    return pl.pallas_call(
        _swiglu_kernel,
        out_shape=jax.ShapeDtypeStruct((T, D), F32),
        grid=(T // tm,),
        in_specs=[pl.BlockSpec((tm, D), lambda i: (i, 0)),
                  pl.BlockSpec((D, F), fixed, pipeline_mode=once),
                  pl.BlockSpec((D, F), fixed, pipeline_mode=once),
                  pl.BlockSpec((F, D), fixed, pipeline_mode=once),
                  pl.BlockSpec((1, D), fixed), pl.BlockSpec((1, D), fixed)],
        out_specs=pl.BlockSpec((tm, D), lambda i: (i, 0)),
        compiler_params=_cparams("parallel"),
        name="swiglu_ln",
    )(h, w_gate.astype(BF16), w_up.astype(BF16), w_down.astype(BF16), g.reshape(1, D), b.reshape(1, D))


def _router_kernel(x_ref, w_ref, idx_ref, wgt_ref, xlo_ref, xhi_ref):
    x = x_ref[...]
    w = w_ref[...]
    x_hi = x.astype(BF16)
    x_lo = (x - x_hi.astype(F32)).astype(BF16)
    w_hi = w.astype(BF16)
    w_lo = (w - w_hi.astype(F32)).astype(BF16)
    lg = _dot(x_hi, w_hi) + (_dot(x_hi, w_lo) + _dot(x_lo, w_hi))
    e_iota = lax.broadcasted_iota(jnp.int32, lg.shape, 1).astype(F32)
    m1 = jnp.max(lg, axis=-1, keepdims=True)
    i1 = jnp.min(jnp.where(lg == m1, e_iota, float(N_EXPERTS)), axis=-1, keepdims=True)
    first = e_iota == i1
    rest = jnp.where(first, -jnp.inf, lg)
    m2 = jnp.max(rest, axis=-1, keepdims=True)
    i2 = jnp.min(jnp.where(rest == m2, e_iota, float(N_EXPERTS)), axis=-1, keepdims=True)
    w2 = jnp.exp(m2 - m1)
    den = 1.0 + w2
    k_iota = lax.broadcasted_iota(jnp.int32, idx_ref.shape, 1)
    idx_ref[...] = jnp.where(k_iota == 0, i1, i2).astype(jnp.int32)
    wgt_ref[...] = jnp.where(k_iota == 0, 1.0 / den, w2 / den)
    half = x.shape[1] // 2
    xlo_ref[...] = _pack_bf16_pairs(x[:, :half])
    xhi_ref[...] = _pack_bf16_pairs(x[:, half:])


def _pack_bf16_pairs(x):
    w = x.shape[1] // 2
    lo = lax.bitcast_convert_type(x[:, :w].astype(BF16).astype(F32), jnp.uint32)
    hi = lax.bitcast_convert_type(x[:, w:].astype(BF16).astype(F32), jnp.uint32)
    return (lo >> 16) | (hi & jnp.uint32(0xFFFF0000))


def _unpack_bf16_pairs(words):
    lo = lax.bitcast_convert_type(words << 16, F32).astype(BF16)
    hi = lax.bitcast_convert_type(words & jnp.uint32(0xFFFF0000), F32).astype(BF16)
    return jnp.concatenate([lo, hi], axis=1)


def moe_route(h, w_router, tm=1024):
    T, D = h.shape
    E = w_router.shape[1]
    row = lambda i: (i, 0)
    return pl.pallas_call(
        _router_kernel,
        out_shape=(jax.ShapeDtypeStruct((T, TOP_K), jnp.int32), jax.ShapeDtypeStruct((T, TOP_K), F32),
                   jax.ShapeDtypeStruct((T, D // 4), jnp.uint32), jax.ShapeDtypeStruct((T, D // 4), jnp.uint32)),
        grid=(T // tm,),
        in_specs=[pl.BlockSpec((tm, D), row), pl.BlockSpec((D, E), lambda i: (0, 0))],
        out_specs=(pl.BlockSpec((tm, TOP_K), row), pl.BlockSpec((tm, TOP_K), row),
                   pl.BlockSpec((tm, D // 4), row), pl.BlockSpec((tm, D // 4), row)),
        compiler_params=_cparams("parallel"),
        name="moe_route",
    )(h, w_router)


def sc_gather_rows(xs, idx, window):
    n_idx = idx.shape[0]
    n_arr = len(xs)
    mesh = plsc.VectorSubcoreMesh(core_axis_name="core", subcore_axis_name="subcore")

    @pl.kernel(out_type=[jax.ShapeDtypeStruct((n_idx, x.shape[1]), x.dtype) for x in xs], mesh=mesh,
               scratch_types=[], name="sc_gather_rows")
    def gather(*refs):
        x_hbms, i_hbm, o_hbms = refs[:n_arr], refs[n_arr], refs[n_arr + 1:]
        for x_hbm, o_hbm in zip(x_hbms, o_hbms):
            def body(i_vmem, o_vmem, x_hbm=x_hbm):
                pltpu.sync_copy(x_hbm.at[i_vmem.at[0]], o_vmem)

            pltpu.emit_pipeline(
                body,
                grid=(n_idx // window,),
                in_specs=[pl.BlockSpec((1, window), lambda i: (0, i))],
                out_specs=[pl.BlockSpec((window, x_hbm.shape[1]), lambda i: (i, 0))],
                core_axis_name=("core", "subcore"),
                dimension_semantics=(pltpu.PARALLEL,),
            )(i_hbm, o_hbm)

    return gather(*xs, idx.reshape(1, n_idx))


def _moe_plan(top_idx, tm):
    T = top_idx.shape[0]
    n_asg = T * TOP_K
    n_tiles = n_asg // tm + N_EXPERTS
    flat_e = top_idx.reshape(n_asg)
    onehot = (flat_e[:, None] == jnp.arange(N_EXPERTS, dtype=jnp.int32)[None, :]).astype(jnp.int32)
    incl = jnp.cumsum(onehot, axis=0)
    counts = incl[-1]
    rank = jnp.sum((incl - onehot) * onehot, axis=1)
    tiles_e = (counts + tm - 1) // tm
    tile_end = jnp.cumsum(tiles_e)
    row_start_p = (tile_end - tiles_e) * tm
    row_start = jnp.cumsum(counts) - counts
    pos = (row_start_p[flat_e] + rank).reshape(T, TOP_K).T.reshape(n_asg)
    order = jnp.argsort(flat_e, stable=True).astype(jnp.int32)
    tile_ids = jnp.arange(n_tiles, dtype=jnp.int32)
    tile_expert = jnp.minimum(jnp.searchsorted(tile_end, tile_ids, side="right"), N_EXPERTS - 1).astype(jnp.int32)
    r = jnp.arange(n_tiles * tm, dtype=jnp.int32)
    e_r = tile_expert[r // tm]
    j = r - row_start_p[e_r]
    valid = (j < counts[e_r]) & (r // tm < tile_end[-1])
    a_r = order[jnp.clip(row_start[e_r] + j, 0, n_asg - 1)]
    src_tok = jnp.where(valid, a_r // TOP_K, 0).astype(jnp.int32)
    return src_tok, pos.astype(jnp.int32), tile_expert, tile_end[-1:].astype(jnp.int32)


def _moe_ffn_kernel(te_ref, nu_ref, xlo_ref, xhi_ref, wg_ref, wu_ref, wd_ref, olo_ref, ohi_ref, acc_s):
    i = pl.program_id(0)
    c = pl.program_id(1)
    last = pl.num_programs(1) - 1
    used = i < nu_ref[0]
    half = acc_s.shape[1] // 2

    @pl.when(used & (c == 0))
    def _():
        acc_s[...] = jnp.zeros(acc_s.shape, F32)

    @pl.when(used)
    def _():
        x = jnp.concatenate([_unpack_bf16_pairs(xlo_ref[...]), _unpack_bf16_pairs(xhi_ref[...])], axis=1)
        hmid = _silu(_dot(x, wg_ref[0])) * _dot(x, wu_ref[0])
        acc_s[...] += _dot(hmid.astype(BF16), wd_ref[0])

    @pl.when(used & (c == last))
    def _():
        olo_ref[...] = _pack_bf16_pairs(acc_s[:, :half])
        ohi_ref[...] = _pack_bf16_pairs(acc_s[:, half:])

    @pl.when(jnp.logical_not(used) & (c == last))
    def _():
        olo_ref[...] = jnp.zeros(olo_ref.shape, olo_ref.dtype)
        ohi_ref[...] = jnp.zeros(ohi_ref.shape, ohi_ref.dtype)


def moe_expert_ffn(xs_lo, xs_hi, tile_expert, n_used, w_gate, w_up, w_down, tm, fc=896):
    R, half = xs_lo.shape
    D = 4 * half
    E, _, F = w_gate.shape
    rows = lambda i, c, te, nu: (i, 0)
    return pl.pallas_call(
        _moe_ffn_kernel,
        out_shape=(jax.ShapeDtypeStruct((R, half), jnp.uint32), jax.ShapeDtypeStruct((R, half), jnp.uint32)),
        grid_spec=pltpu.PrefetchScalarGridSpec(
            num_scalar_prefetch=2,
            grid=(R // tm, F // fc),
            in_specs=[pl.BlockSpec((tm, half), rows), pl.BlockSpec((tm, half), rows),
                      pl.BlockSpec((1, D, fc), lambda i, c, te, nu: (te[i], 0, c)),
                      pl.BlockSpec((1, D, fc), lambda i, c, te, nu: (te[i], 0, c)),
                      pl.BlockSpec((1, fc, D), lambda i, c, te, nu: (te[i], c, 0))],
            out_specs=(pl.BlockSpec((tm, half), rows), pl.BlockSpec((tm, half), rows)),
            scratch_shapes=[pltpu.VMEM((tm, D), F32)]),
        compiler_params=_cparams("parallel", "arbitrary"),
        name="moe_expert_ffn",
    )(tile_expert, n_used, xs_lo, xs_hi, w_gate.astype(BF16), w_up.astype(BF16), w_down.astype(BF16))


def _moe_combine_kernel(h_ref, lo0_ref, hi0_ref, lo1_ref, hi1_ref, w_ref, g_ref, b_ref, o_ref):
    w = w_ref[...]
    y0 = jnp.concatenate([_unpack_bf16_pairs(lo0_ref[...]), _unpack_bf16_pairs(hi0_ref[...])], axis=1).astype(F32)
    y1 = jnp.concatenate([_unpack_bf16_pairs(lo1_ref[...]), _unpack_bf16_pairs(hi1_ref[...])], axis=1).astype(F32)
    ff = w[:, 0:1] * y0 + w[:, 1:2] * y1
    o_ref[...] = _layer_norm_rows(ALPHA * h_ref[...] + ff, g_ref[...], b_ref[...])


def moe_combine_ln(h, y_lo, y_hi, top_w, g, b, tm=512):
    T, D = h.shape
    half = D // 4
    nt = T // tm
    row = lambda i: (i, 0)
    second = lambda i: (nt + i, 0)
    fixed = lambda i: (0, 0)
    return pl.pallas_call(
        _moe_combine_kernel,
        out_shape=jax.ShapeDtypeStruct((T, D), F32),
        grid=(nt,),
        in_specs=[pl.BlockSpec((tm, D), row),
                  pl.BlockSpec((tm, half), row), pl.BlockSpec((tm, half), row),
                  pl.BlockSpec((tm, half), second), pl.BlockSpec((tm, half), second),
                  pl.BlockSpec((tm, TOP_K), row),
                  pl.BlockSpec((1, D), fixed), pl.BlockSpec((1, D), fixed)],
        out_specs=pl.BlockSpec((tm, D), row),
        compiler_params=_cparams("parallel"),
        name="moe_combine_ln",
    )(h, y_lo, y_hi, y_lo, y_hi, top_w, g.reshape(1, D), b.reshape(1, D))


def moe_ln(h, w_router, w_gate, w_up, w_down, g, b, tm=512):
    top_idx, top_w, hb_lo, hb_hi = moe_route(h, w_router)
    src_tok, pos, tile_expert, n_used = _moe_plan(top_idx, tm)
    xs_lo, xs_hi = sc_gather_rows([hb_lo, hb_hi], src_tok, SC_GATHER_WINDOW)
    ys_lo, ys_hi = moe_expert_ffn(xs_lo, xs_hi, tile_expert, n_used, w_gate, w_up, w_down, tm)
    y_lo, y_hi = sc_gather_rows([ys_lo, ys_hi], pos, SC_GATHER_WINDOW)
    return moe_combine_ln(h, y_lo, y_hi, top_w, g, b)


def _pad_cols(w, width):
    return jnp.pad(w, ((0, 0), (0, width - w.shape[1])))


def even_layer_mixer(h, w_in, ckv_gain, w_uk, w_uv, sinks, w_out, rel_table, ln_g, ln_b, batch, seq):
    cuts = np.cumsum((A_HEADS * HEAD_DIM, A_LATENT, IDX_HEADS * IDX_DIM, IDX_DIM, IDX_HEADS,
                      B_HEADS * HEAD_DIM, B_KV_HEADS * HEAD_DIM))
    w_qa, w_ckv, w_qi, w_ki, w_wi, w_qb, w_kb, w_vb = jnp.split(w_in, [int(c) for c in cuts], axis=1)
    w_main = jnp.concatenate([w_qa, w_qi, w_qb, _pad_cols(w_ki, LANES), w_kb, w_vb], axis=1).astype(BF16)
    w_aux = jnp.concatenate([w_ckv, _pad_cols(w_wi, LANES)], axis=1).astype(BF16)
    proj = matmul(h, w_main, BF16)
    aux = matmul(h, w_aux, F32)
    o_a = dsa_attention(proj, aux, ckv_gain, w_uk, w_uv, rel_table[:, :A_HEADS], batch, seq)
    o_b = banded_attention(proj, 2, 13, 14, B_HEADS, B_KV_HEADS,
                           rel_table[:, A_HEADS:A_HEADS + B_HEADS], B_WINDOW,
                           batch, seq, sinks=sinks)
    o = jnp.concatenate([o_a, o_b], axis=1)
    return matmul_residual_ln([o], w_out.astype(BF16), h, ln_g, ln_b)


def odd_layer_mixer(h, w_in, pe_k, w1_k, w2_k, pe_v, w1_v, w2_v, w_out, rel_table, ln_g, ln_b, batch, seq):
    kvw = C_KV_HEADS * HEAD_DIM
    qw = C_HEADS * HEAD_DIM
    w_main = w_in[:, :qw + 6 * kvw].astype(BF16)
    w_gl = w_in[:, qw + 6 * kvw:].reshape(D_MODEL, C_HEADS, 3)
    w_gate = jnp.concatenate([_pad_cols(w_gl[:, :, r], LANES) for r in range(3)], axis=1).astype(BF16)
    proj = matmul(h, w_main, BF16)
    gates = matmul(h, w_gate, F32)
    tbl = rel_table[:, :C_HEADS]
    kc = compress_blocks(proj[:, qw:qw + kvw], pe_k, w1_k, w2_k, batch, seq)
    vc = compress_blocks(proj[:, qw + kvw:qw + 2 * kvw], pe_v, w1_v, w2_v, batch, seq)
    o_cmp, pick = cmp_attention(proj, kc, vc, gates, tbl, batch, seq)
    o_slc = slc_attention(proj, pick, gates, tbl, batch, seq)
    o_win = banded_attention(proj, 0, 12, 13, C_HEADS, C_KV_HEADS, tbl, C_WINDOW,
                             batch, seq, gates=gates, gate_col=2)
    return matmul_residual_ln([o_cmp, o_slc, o_win], w_out.astype(BF16), h, ln_g, ln_b)


def kernel(x, mem, rel_table, ev_w_in, ev_ckv_gain, ev_w_uk, ev_w_uv, ev_sinks, ev_w_out, od_w_in, od_pe_k, od_w1_k, od_w2_k, od_pe_v, od_w1_v, od_w2_v, od_w_out, xa_w_q, xa_w_k, xa_w_v, xa_w_o, ff_w_gate, ff_w_up, ff_w_down, moe_w_router, moe_w_gate, moe_w_up, moe_w_down, ln_g, ln_b):
    batch, seq, D = x.shape
    h = x.reshape(batch * seq, D)
    mem2 = mem.reshape(-1, D)
    for i in range(DEPTH):
        j = i // 2
        if i % 2 == 0:
            h = even_layer_mixer(h, ev_w_in[j], ev_ckv_gain[j], ev_w_uk[j], ev_w_uv[j], ev_sinks[j],
                                 ev_w_out[j], rel_table, ln_g[i, 0], ln_b[i, 0], batch, seq)
        else:
            h = odd_layer_mixer(h, od_w_in[j], od_pe_k[j], od_w1_k[j], od_w2_k[j], od_pe_v[j],
                                od_w1_v[j], od_w2_v[j], od_w_out[j], rel_table, ln_g[i, 0], ln_b[i, 0],
                                batch, seq)
        h = cross_attention_ln(h, mem2, xa_w_q[i], xa_w_k[i], xa_w_v[i], xa_w_o[i],
                               ln_g[i, 1], ln_b[i, 1], batch, seq)
        if i % 2 == 0:
            h = swiglu_ln(h, ff_w_gate[j], ff_w_up[j], ff_w_down[j], ln_g[i, 2], ln_b[i, 2])
        else:
            h = moe_ln(h, moe_w_router[j], moe_w_gate[j], moe_w_up[j], moe_w_down[j], ln_g[i, 2], ln_b[i, 2])
    return h.reshape(batch, seq, D)
```

```python
import functools
import math

import numpy as np
import jax
import jax.numpy as jnp
from jax import lax
from jax.experimental import pallas as pl
from jax.experimental.pallas import tpu as pltpu
from jax.experimental.pallas import tpu_sc as plsc

D_MODEL = 1024
DEPTH = 2
HEAD_DIM = 64
N_SLOTS = D_MODEL // HEAD_DIM
A_HEADS = 8
A_LATENT = 128
IDX_HEADS = 8
IDX_DIM = 64
A_TOPK_MAX = 256
B_HEADS = 8
B_KV_HEADS = 2
B_WINDOW = 128
C_HEADS = 16
C_KV_HEADS = 2
CMP_BLOCK = 32
CMP_STRIDE = 16
CMP_HIDDEN = 128
SLC_BLOCK = 64
SLC_COUNT = 8
C_WINDOW = 512
X_HEADS = 4
X_HEAD_DIM = 128
D_FF = 2816
N_EXPERTS = 8
TOP_K = 2
D_FF_EXPERT = 3584
REL_BUCKETS = 32
REL_MAX_DIST = 128
LN_EPS = 1e-5
NEG_INF = -1e30
FORCE_BONUS = 1e6
ALPHA = (2 * DEPTH) ** 0.25
MASK_BIG = 2.0 ** 100
LOG2E = math.log2(math.e)
ONES_ROWS = 16

LANES = 128
QB = 128
VMEM_LIMIT_BYTES = 56 * 1024 * 1024
SC_GATHER_WINDOW = 128

F32 = jnp.float32
BF16 = jnp.bfloat16


def _cparams(*sem):
    return pltpu.CompilerParams(dimension_semantics=sem, vmem_limit_bytes=VMEM_LIMIT_BYTES)


def _dot(a, b):
    return jnp.dot(a, b, preferred_element_type=F32)


def _dot_nt(a, b):
    return lax.dot_general(a, b, (((1,), (1,)), ((), ())), preferred_element_type=F32)


def _layer_norm_rows(y, g, b):
    mu = jnp.mean(y, axis=-1, keepdims=True)
    yc = y - mu
    var = jnp.mean(yc * yc, axis=-1, keepdims=True)
    return yc * lax.rsqrt(var + LN_EPS) * g + b


def _mm_kernel(x_ref, w_ref, o_ref):
    o_ref[...] = _dot(x_ref[...].astype(BF16), w_ref[...]).astype(o_ref.dtype)


def matmul(x, w, out_dtype, tm=512):
    M, K = x.shape
    N = w.shape[1]
    tm = min(tm, M)
    return pl.pallas_call(
        _mm_kernel,
        out_shape=jax.ShapeDtypeStruct((M, N), out_dtype),
        grid=(M // tm,),
        in_specs=[pl.BlockSpec((tm, K), lambda i: (i, 0)),
                  pl.BlockSpec((K, N), lambda i: (0, 0))],
        out_specs=pl.BlockSpec((tm, N), lambda i: (i, 0)),
        compiler_params=_cparams("parallel"),
        name="matmul",
    )(x, w)


def _mm_res_ln_kernel(*refs, n_in):
    a_refs = refs[:n_in]
    w_ref, h_ref, g_ref, b_ref, o_ref = refs[n_in:]
    a = a_refs[0][...]
    if n_in > 1:
        a = a.astype(F32)
        for r in a_refs[1:]:
            a = a + r[...].astype(F32)
    y = ALPHA * h_ref[...] + _dot(a.astype(BF16), w_ref[...])
    o_ref[...] = _layer_norm_rows(y, g_ref[...], b_ref[...])


def matmul_residual_ln(a_list, w, h, g, b, tm=512):
    M, K = a_list[0].shape
    N = w.shape[1]
    n_in = len(a_list)
    row = lambda i: (i, 0)
    fixed = lambda i: (0, 0)
    return pl.pallas_call(
        functools.partial(_mm_res_ln_kernel, n_in=n_in),
        out_shape=jax.ShapeDtypeStruct((M, N), F32),
        grid=(M // tm,),
        in_specs=[pl.BlockSpec((tm, K), row)] * n_in + [
            pl.BlockSpec((K, N), fixed), pl.BlockSpec((tm, N), row),
            pl.BlockSpec((1, N), fixed), pl.BlockSpec((1, N), fixed)],
        out_specs=pl.BlockSpec((tm, N), row),
        compiler_params=_cparams("parallel"),
        name="matmul_residual_ln",
    )(*a_list, w, h, g.reshape(1, N), b.reshape(1, N))


def _rel_bucket(dist):
    n = jnp.maximum(dist, 0)
    max_exact = REL_BUCKETS // 2
    nf = jnp.maximum(n, 1).astype(F32)
    log_b = max_exact + (jnp.log(nf / max_exact) / math.log(REL_MAX_DIST / max_exact)
                         * (REL_BUCKETS - max_exact)).astype(jnp.int32)
    return jnp.where(n < max_exact, n, jnp.minimum(log_b, REL_BUCKETS - 1))


def _bias_lookup(tbl, dist):
    onehot = (_rel_bucket(dist)[..., None] == jnp.arange(REL_BUCKETS, dtype=jnp.int32)).astype(F32)
    return jnp.einsum("...r,rh->...h", onehot, tbl.astype(F32), precision=lax.Precision.HIGHEST)


def _tile_bias(tbl):
    i = np.arange(QB)[:, None]
    j = np.arange(QB)[None, :]
    dist = jnp.asarray(np.stack([i - j, QB + i - j, 2 * QB + i - j]))
    return _bias_lookup(tbl, dist).transpose(3, 0, 1, 2)


def _tile_bias_masked_t(tbl, n_tiles, window=None):
    j = np.arange(QB)[:, None]
    i = np.arange(QB)[None, :]
    dist = np.stack([d * QB + i - j for d in range(n_tiles)])
    valid = dist >= 0
    if window is not None:
        valid &= dist < window
    bias = jnp.where(jnp.asarray(valid)[..., None], _bias_lookup(tbl, jnp.asarray(dist)) * LOG2E, NEG_INF)
    return bias.transpose(3, 0, 1, 2)


def _flash_tile(s, add, m_ref, l_ref, acc_ref, vt_ones, hpg):
    dv = acc_ref.shape[0]
    scale, probs = [], []
    for hi in range(hpg):
        sl = slice(hi * QB, (hi + 1) * QB)
        sh = s[:, sl] + add(hi)
        m_old = m_ref[:, sl]
        m_new = jnp.maximum(m_old, jnp.max(sh, axis=0, keepdims=True))
        scale.append(jnp.exp2(m_old - m_new))
        probs.append(jnp.exp2(sh - m_new).astype(BF16))
        m_ref[:, sl] = m_new
    pv = _dot(vt_ones, jnp.concatenate(probs, axis=1))
    for hi in range(hpg):
        sl = slice(hi * QB, (hi + 1) * QB)
        acc_ref[:, sl] = scale[hi] * acc_ref[:, sl] + pv[:dv, sl]
        l_ref[:, sl] = scale[hi] * l_ref[:, sl] + pv[dv:dv + 1, sl]


def _value_tile_t(v_tile, groups):
    vt = v_tile.astype(F32).T.astype(BF16)
    ones = jnp.ones((ONES_ROWS, v_tile.shape[0]), BF16)
    parts = []
    for g in range(groups):
        parts += [vt[g * HEAD_DIM:(g + 1) * HEAD_DIM, :], ones]
    return jnp.concatenate(parts, axis=0)


def _for_tiles_pairwise(n_tiles, start, finish):
    def pair(i, carry):
        first = start(2 * i)
        second = start(2 * i + 1)
        finish(2 * i, first)
        finish(2 * i + 1, second)
        return carry

    lax.fori_loop(0, lax.shift_right_logical(n_tiles, 1), pair, 0)

    @pl.when((n_tiles & 1) == 1)
    def _():
        finish(n_tiles - 1, start(n_tiles - 1))


def _stack_heads(q, first, count, scale):
    return jnp.concatenate([q[:, (first + hi) * HEAD_DIM:(first + hi + 1) * HEAD_DIM] * scale
                            for hi in range(count)], axis=0)


def _store_heads_t(o_ref, acc_ref, l_ref, first, count, gate_t):
    for hi in range(count):
        h = first + hi
        sl = slice(hi * QB, (hi + 1) * QB)
        o = acc_ref[:, sl] / l_ref[:, sl]
        if gate_t is not None:
            o = o * (1.0 / (1.0 + jnp.exp(-gate_t[h:h + 1, :])))
        o_ref[:, h * HEAD_DIM:(h + 1) * HEAD_DIM] = o.T.astype(o_ref.dtype)


def _band_kernel(*refs, n_tiles, kvh, g, has_sinks, has_gate):
    it = iter(refs)
    q_ref, k_ref, v_ref, bias_ref = next(it), next(it), next(it), next(it)
    sink_ref = next(it) if has_sinks else None
    gate_ref = next(it) if has_gate else None
    o_ref, vt_s, m_s, l_s, acc_s = next(it), next(it), next(it), next(it), next(it)
    n = pl.program_id(1)
    nq = vt_s.shape[0]

    @pl.when(n == 0)
    def _():
        for kt in range(nq):
            vt_s[kt] = _value_tile_t(v_ref[kt * QB:(kt + 1) * QB, :], kvh)

    q = q_ref[...]
    qst = [_stack_heads(q, kv * g, g, HEAD_DIM ** -0.5 * LOG2E) for kv in range(kvh)]
    if has_sinks:
        m_s[...] = sink_ref[...] * LOG2E
        l_s[...] = jnp.ones(l_s.shape, F32)
    else:
        m_s[...] = jnp.full(m_s.shape, NEG_INF, F32)
        l_s[...] = jnp.zeros(l_s.shape, F32)
    acc_s[...] = jnp.zeros(acc_s.shape, F32)
    vrows = HEAD_DIM + ONES_ROWS

    def logits(d):
        k_t = k_ref[pl.ds(pl.multiple_of((n - d) * QB, QB), QB), :]
        return [_dot_nt(k_t[:, kv * HEAD_DIM:(kv + 1) * HEAD_DIM], qst[kv]) for kv in range(kvh)]

    def update(d, s):
        vt_t = vt_s[n - d]
        for kv in range(kvh):
            _flash_tile(s[kv], lambda hi, kv=kv: bias_ref[kv * g + hi, d], m_s.at[kv], l_s.at[kv],
                        acc_s.at[kv], vt_t[kv * vrows:(kv + 1) * vrows, :], g)

    @pl.when(n >= n_tiles - 1)
    def _():
        s_next = logits(0)
        for d in range(n_tiles):
            s_cur = s_next
            if d + 1 < n_tiles:
                s_next = logits(d + 1)
            update(d, s_cur)

    @pl.when(n < n_tiles - 1)
    def _():
        for d in range(n_tiles - 1):
            @pl.when(n >= d)
            def _(d=d):
                update(d, logits(d))

    gate_t = gate_ref[...].T if has_gate else None
    for kv in range(kvh):
        _store_heads_t(o_ref, acc_s.at[kv], l_s.at[kv], kv * g, g, gate_t)


def banded_attention(proj, q_col, k_col, v_col, n_heads, kvh, tbl, window, batch, seq,
                     sinks=None, gates=None, gate_col=0):
    g = n_heads // kvh
    kw = kvh * HEAD_DIM
    qw = n_heads * HEAD_DIM
    nq = seq // QB
    n_tiles = window // QB + 1
    args = [proj, proj, proj, _tile_bias_masked_t(tbl, n_tiles, window)]
    in_specs = [pl.BlockSpec((QB, qw), lambda b, n: (b * nq + n, q_col)),
                pl.BlockSpec((seq, kw), lambda b, n: (b, k_col)),
                pl.BlockSpec((seq, kw), lambda b, n: (b, v_col)),
                pl.BlockSpec((n_heads, n_tiles, QB, QB), lambda b, n: (0, 0, 0, 0))]
    if sinks is not None:
        args.append(jnp.repeat(sinks.astype(F32).reshape(kvh, 1, g), QB, axis=2))
        in_specs.append(pl.BlockSpec((kvh, 1, g * QB), lambda b, n: (0, 0, 0)))
    if gates is not None:
        args.append(gates)
        in_specs.append(pl.BlockSpec((QB, LANES), lambda b, n: (b * nq + n, gate_col)))
    return pl.pallas_call(
        functools.partial(_band_kernel, n_tiles=n_tiles, kvh=kvh, g=g,
                          has_sinks=sinks is not None, has_gate=gates is not None),
        out_shape=jax.ShapeDtypeStruct((batch * seq, qw), BF16),
        grid=(batch, nq),
        in_specs=in_specs,
        out_specs=pl.BlockSpec((QB, qw), lambda b, n: (b * nq + n, 0)),
        scratch_shapes=[pltpu.VMEM((nq, kvh * (HEAD_DIM + ONES_ROWS), QB), BF16),
                        pltpu.VMEM((kvh, 1, g * QB), F32), pltpu.VMEM((kvh, 1, g * QB), F32),
                        pltpu.VMEM((kvh, HEAD_DIM, g * QB), F32)],
        compiler_params=_cparams("parallel", "arbitrary"),
        name="banded_attention",
    )(*args)


def _sort_key(x):
    bits = lax.bitcast_convert_type(x, jnp.int32)
    return bits ^ ((bits >> 31) & jnp.int32(0x7FFFFFFF))


def _dsa_kernel(qa_ref, qi_ref, ki_ref, ckv_ref, wi_ref, gain_ref, wuk_ref, wuv_ref, bias_ref,
                o_ref, c_s, ct_s, key_s, m_s, l_s, acc_s, *, topk):
    n = pl.program_id(1)
    nt = n + 1
    nq = c_s.shape[0]

    @pl.when(n == 0)
    def _():
        for kt in range(nq):
            c = ckv_ref[kt * QB:(kt + 1) * QB, :]
            y = c * lax.rsqrt(jnp.mean(c * c, axis=-1, keepdims=True) + LN_EPS) * gain_ref[...]
            c_s[kt] = y.astype(BF16)
            ct_s[kt] = jnp.concatenate([y.T.astype(BF16), jnp.ones((ONES_ROWS, QB), BF16)], axis=0)

    qa = qa_ref[...]
    qi = qi_ref[...]
    qlat = jnp.concatenate(
        [(_dot(qa[:, h * HEAD_DIM:(h + 1) * HEAD_DIM], wuk_ref[h]) * (HEAD_DIM ** -0.5 * LOG2E)).astype(BF16)
         for h in range(A_HEADS)], axis=0)
    qidx = jnp.concatenate(
        [qi[:, h * IDX_DIM:(h + 1) * IDX_DIM] * IDX_DIM ** -0.5 for h in range(IDX_HEADS)], axis=0)
    wi_t = wi_ref[...].T
    wi_row = jnp.concatenate([wi_t[h:h + 1, :] for h in range(IDX_HEADS)], axis=1) * IDX_HEADS ** -0.5

    kpos0 = lax.broadcasted_iota(jnp.int32, (QB, QB), 0)
    qpos = n * QB + lax.broadcasted_iota(jnp.int32, (QB, QB), 1)

    def index_logits(kt):
        k_t = ki_ref[pl.ds(pl.multiple_of(kt * QB, QB), QB), :][:, :IDX_DIM]
        return _dot_nt(k_t, qidx)

    def score_tile(kt, lg):
        act = jnp.maximum(lg, 0.0) * wi_row
        sc = act[:, :QB]
        for h in range(1, IDX_HEADS):
            sc = sc + act[:, h * QB:(h + 1) * QB]
        sc = jnp.where(kt * QB + kpos0 <= qpos, sc, NEG_INF)
        sc = jnp.where(sc == 0.0, 0.0, sc)
        key_s[kt] = _sort_key(sc)

    _for_tiles_pairwise(nt, index_logits, score_tile)

    def count(pred):
        def hits(kt):
            return jnp.where(pred(kt, key_s[kt]), 1.0, 0.0)

        def pair(i, acc):
            return acc + (hits(2 * i) + hits(2 * i + 1))
        acc = lax.fori_loop(0, lax.shift_right_logical(nt, 1), pair, jnp.zeros((QB, QB), F32))
        acc = acc + jnp.where((nt & 1) == 1, hits(nt - 1), 0.0)
        return jnp.sum(acc, axis=0, keepdims=True)

    def thr_step(i, lo):
        cand = lo + (jnp.int32(1) << (31 - i))
        cnt = count(lambda kt, key: key >= cand)
        return jnp.where(cnt >= topk, cand, lo)

    thr = lax.fori_loop(0, 32, thr_step, jnp.full((1, QB), -2 ** 31, jnp.int32))
    need = topk - count(lambda kt, key: key > thr)

    def tie_step(i, j0):
        cand = j0 + (jnp.int32(1) << (10 - i))
        cnt = count(lambda kt, key: (key == thr) & (kt * QB + kpos0 < cand))
        return jnp.where(cnt < need, cand, j0)

    causal_ties = count(lambda kt, key: (key == thr) & (kt * QB + kpos0 <= qpos))
    must_search = jnp.max(jnp.where(causal_ties > need, 1.0, 0.0)) > 0.5
    j0 = lax.cond(must_search,
                  lambda: lax.fori_loop(0, 11, tie_step, jnp.zeros((1, QB), jnp.int32)),
                  lambda: jnp.full((1, QB), nq * QB, jnp.int32))

    m_s[...] = jnp.full(m_s.shape, NEG_INF, F32)
    l_s[...] = jnp.zeros(l_s.shape, F32)
    acc_s[...] = jnp.zeros(acc_s.shape, F32)

    def attend_logits(kt):
        return _dot_nt(c_s[kt], qlat)

    def attend_update(kt, s):
        key = key_s[kt]
        kpos = kt * QB + kpos0
        ok = ((key > thr) | ((key == thr) & (kpos <= j0))) & (kpos <= qpos)
        d = jnp.minimum(n - kt, 2)
        scale, probs = [], []
        for h in range(A_HEADS):
            sl = slice(h * QB, (h + 1) * QB)
            sh = jnp.where(ok, s[:, sl] + bias_ref[h, d], NEG_INF)
            m_old = m_s[:, sl]
            m_new = jnp.maximum(m_old, jnp.max(sh, axis=0, keepdims=True))
            scale.append(jnp.exp2(m_old - m_new))
            probs.append(jnp.where(ok, jnp.exp2(sh - m_new), 0.0).astype(BF16))
            m_s[:, sl] = m_new
        pv = _dot(ct_s[kt], jnp.concatenate(probs, axis=1))
        for h in range(A_HEADS):
            sl = slice(h * QB, (h + 1) * QB)
            acc_s[:, sl] = scale[h] * acc_s[:, sl] + pv[:A_LATENT, sl]
            l_s[:, sl] = scale[h] * l_s[:, sl] + pv[A_LATENT:A_LATENT + 1, sl]

    _for_tiles_pairwise(nt, attend_logits, attend_update)

    for h in range(A_HEADS):
        sl = slice(h * QB, (h + 1) * QB)
        o_lat = (acc_s[:, sl] / l_s[:, sl]).T
        o_ref[:, h * HEAD_DIM:(h + 1) * HEAD_DIM] = _dot(o_lat.astype(BF16), wuv_ref[h]).astype(o_ref.dtype)


def dsa_attention(proj, aux, gain, w_uk, w_uv, tbl, batch, seq):
    nq = seq // QB
    topk = min(A_TOPK_MAX, seq // 4)
    qw = A_HEADS * HEAD_DIM
    return pl.pallas_call(
        functools.partial(_dsa_kernel, topk=topk),
        out_shape=jax.ShapeDtypeStruct((batch * seq, qw), BF16),
        grid=(batch, nq),
        in_specs=[
            pl.BlockSpec((QB, qw), lambda b, n: (b * nq + n, 0)),
            pl.BlockSpec((QB, qw), lambda b, n: (b * nq + n, 1)),
            pl.BlockSpec((seq, LANES), lambda b, n: (b, 12)),
            pl.BlockSpec((seq, A_LATENT), lambda b, n: (b, 0)),
            pl.BlockSpec((QB, LANES), lambda b, n: (b * nq + n, 1)),
            pl.BlockSpec((1, A_LATENT), lambda b, n: (0, 0)),
            pl.BlockSpec((A_HEADS, HEAD_DIM, A_LATENT), lambda b, n: (0, 0, 0)),
            pl.BlockSpec((A_HEADS, A_LATENT, HEAD_DIM), lambda b, n: (0, 0, 0)),
            pl.BlockSpec((A_HEADS, 3, QB, QB), lambda b, n: (0, 0, 0, 0)),
        ],
        out_specs=pl.BlockSpec((QB, qw), lambda b, n: (b * nq + n, 0)),
        scratch_shapes=[
            pltpu.VMEM((nq, QB, A_LATENT), BF16),
            pltpu.VMEM((nq, A_LATENT + ONES_ROWS, QB), BF16),
            pltpu.VMEM((nq, QB, QB), jnp.int32),
            pltpu.VMEM((1, A_HEADS * QB), F32),
            pltpu.VMEM((1, A_HEADS * QB), F32),
            pltpu.VMEM((A_LATENT, A_HEADS * QB), F32),
        ],
        compiler_params=_cparams("parallel", "arbitrary"),
        name="dsa_attention",
    )(proj, proj, proj, aux, aux, gain.reshape(1, A_LATENT).astype(F32),
      w_uk.astype(BF16), w_uv.astype(BF16), _tile_bias(tbl).swapaxes(2, 3) * LOG2E)


def _compress_kernel(x_ref, pe_ref, w1_ref, w2_ref, o_ref):
    half = CMP_STRIDE * HEAD_DIM
    x = x_ref[0]
    w1 = w1_ref[...]
    u = _dot(x, w1[:half])
    v = _dot(x, w1[half:])
    pre = u + pltpu.roll(v, v.shape[0] - 1, 0) + _dot(pe_ref[...], w1)[:1]
    h = 0.5 * pre * (1.0 + jnp.tanh(math.sqrt(2.0 / math.pi) * (pre + 0.044715 * pre * pre * pre)))
    o_ref[0] = _dot(h.astype(BF16), w2_ref[...]).astype(o_ref.dtype)


def compress_blocks(kv, pe, w1, w2, batch, seq):
    G = C_KV_HEADS
    nch = seq // CMP_STRIDE
    half = CMP_STRIDE * HEAD_DIM
    x = kv.reshape(batch, nch, CMP_STRIDE, G, HEAD_DIM).transpose(0, 3, 1, 2, 4)
    x = x.reshape(batch * G, nch, half)
    pe_flat = jnp.broadcast_to(pe.reshape(1, CMP_BLOCK * HEAD_DIM), (8, CMP_BLOCK * HEAD_DIM))
    return pl.pallas_call(
        _compress_kernel,
        out_shape=jax.ShapeDtypeStruct((batch * G, nch, HEAD_DIM), BF16),
        grid=(batch * G,),
        in_specs=[pl.BlockSpec((1, nch, half), lambda i: (i, 0, 0)),
                  pl.BlockSpec((8, 2 * half), lambda i: (0, 0)),
                  pl.BlockSpec((2 * half, CMP_HIDDEN), lambda i: (0, 0)),
                  pl.BlockSpec((CMP_HIDDEN, HEAD_DIM), lambda i: (0, 0))],
        out_specs=pl.BlockSpec((1, nch, HEAD_DIM), lambda i: (i, 0, 0)),
        compiler_params=_cparams("parallel"),
        name="compress_blocks",
    )(x, pe_flat.astype(BF16), w1.astype(BF16), w2.astype(BF16))


def _cmp_kernel(q_ref, kc_ref, vc_ref, bias_ref, gate_ref, ovt_ref, o_ref, pick_ref, *, n_cmp, n_slc, n_sel):
    n = pl.program_id(1)
    G = C_KV_HEADS
    hpg = C_HEADS // G
    nch = kc_ref.shape[1]
    c = lax.broadcasted_iota(jnp.int32, (nch, QB), 0)
    t = n * QB + lax.broadcasted_iota(jnp.int32, (nch, QB), 1)
    valid = (t - (c * CMP_STRIDE + CMP_BLOCK - 1) >= 0) & (c < n_cmp)
    jj = lax.broadcasted_iota(jnp.int32, (n_slc, QB), 0)
    ts = n * QB + lax.broadcasted_iota(jnp.int32, (n_slc, QB), 1)
    blk = ts // SLC_BLOCK
    bonus = FORCE_BONUS * jnp.where((jj == 0) | (jj == blk) | (jj == blk - 1), 1.0, 0.0)
    admissible = jj * SLC_BLOCK <= ts
    q = q_ref[...]
    gate_t = gate_ref[...].T
    for g in range(G):
        s = _dot_nt(kc_ref[g], _stack_heads(q, g * hpg, hpg, HEAD_DIM ** -0.5))
        vct = vc_ref[g].astype(F32).T.astype(BF16)
        psum = jnp.zeros((nch, QB), F32)
        probs = []
        for hi in range(hpg):
            sl = slice(hi * QB, (hi + 1) * QB)
            sh = jnp.where(valid, s[:, sl] + bias_ref[g * hpg + hi], NEG_INF)
            e = jnp.where(valid, jnp.exp(sh - jnp.max(sh, axis=0, keepdims=True)), 0.0)
            den = jnp.sum(e, axis=0, keepdims=True)
            p = e * (1.0 / jnp.where(den > 0.0, den, 1.0))
            psum = psum + p
            probs.append(p.astype(BF16))
        pv = _dot(vct, jnp.concatenate(probs, axis=1))
        for hi in range(hpg):
            h = g * hpg + hi
            o = pv[:, hi * QB:(hi + 1) * QB] * (1.0 / (1.0 + jnp.exp(-gate_t[h:h + 1, :])))
            o_ref[:, h * HEAD_DIM:(h + 1) * HEAD_DIM] = o.T.astype(o_ref.dtype)
        p_hi = psum.astype(BF16)
        p_lo = (psum - p_hi.astype(F32)).astype(BF16)
        p_slc = _dot(ovt_ref[...], p_hi) + _dot(ovt_ref[...], p_lo)
        score = jnp.where(admissible, p_slc + bonus, NEG_INF)
        rank = jnp.zeros((n_slc, QB), F32)
        for i in range(n_slc):
            si = score[i:i + 1, :]
            rank = rank + jnp.where(si > score, 1.0, 0.0) + jnp.where((si == score) & (i < jj), 1.0, 0.0)
        pick_ref[0, g] = jnp.where(rank < n_sel, 0.0, -MASK_BIG).astype(pick_ref.dtype)


def cmp_attention(proj, kc, vc, gates, tbl, batch, seq):
    nq = seq // QB
    G = C_KV_HEADS
    nch = seq // CMP_STRIDE
    n_cmp = (seq - CMP_BLOCK) // CMP_STRIDE + 1
    n_slc = seq // SLC_BLOCK
    n_sel = min(SLC_COUNT, n_slc)
    qw = C_HEADS * HEAD_DIM
    cmp_end = np.arange(nch) * CMP_STRIDE + CMP_BLOCK - 1
    dist_c = jnp.asarray(np.arange(seq)[None, :] - cmp_end[:, None])
    bias_c = _bias_lookup(tbl, dist_c).transpose(2, 0, 1)
    cs = np.arange(nch)[None, :] * CMP_STRIDE
    ss = np.arange(n_slc)[:, None] * SLC_BLOCK
    overlap_t = ((cs < ss + SLC_BLOCK) & (cs + CMP_BLOCK > ss) & (np.arange(nch)[None, :] < n_cmp))
    return pl.pallas_call(
        functools.partial(_cmp_kernel, n_cmp=n_cmp, n_slc=n_slc, n_sel=n_sel),
        out_shape=(jax.ShapeDtypeStruct((batch * seq, qw), BF16),
                   jax.ShapeDtypeStruct((batch, G, n_slc, seq), BF16)),
        grid=(batch, nq),
        in_specs=[
            pl.BlockSpec((QB, qw), lambda b, n: (b * nq + n, 0)),
            pl.BlockSpec((G, nch, HEAD_DIM), lambda b, n: (b, 0, 0)),
            pl.BlockSpec((G, nch, HEAD_DIM), lambda b, n: (b, 0, 0)),
            pl.BlockSpec((C_HEADS, nch, QB), lambda b, n: (0, 0, n)),
            pl.BlockSpec((QB, LANES), lambda b, n: (b * nq + n, 0)),
            pl.BlockSpec((n_slc, nch), lambda b, n: (0, 0)),
        ],
        out_specs=(pl.BlockSpec((QB, qw), lambda b, n: (b * nq + n, 0)),
                   pl.BlockSpec((1, G, n_slc, QB), lambda b, n: (b, 0, 0, n))),
        compiler_params=_cparams("parallel", "parallel"),
        name="cmp_attention",
    )(proj, kc, vc, bias_c, gates, jnp.asarray(overlap_t.astype(np.float32)).astype(BF16))


def _slc_kernel(q_ref, k_ref, v_ref, pick_ref, bias_ref, gate_ref, o_ref, vt_s, m_s, l_s, acc_s, *, n_slc):
    n = pl.program_id(1)
    G = C_KV_HEADS
    hpg = C_HEADS // G
    nq = vt_s.shape[0]

    @pl.when(n == 0)
    def _():
        for kt in range(nq):
            vt_s[kt] = _value_tile_t(v_ref[kt * QB:(kt + 1) * QB, :], G)

    q = q_ref[...]
    qst = [_stack_heads(q, g * hpg, hpg, HEAD_DIM ** -0.5 * LOG2E) for g in range(G)]
    pickst = [jnp.concatenate([pick_ref[0, g]] * hpg, axis=1) for g in range(G)]
    ej = lax.broadcasted_iota(jnp.int32, (QB, n_slc), 0)
    eb = lax.broadcasted_iota(jnp.int32, (QB, n_slc), 1)
    m_s[...] = jnp.full(m_s.shape, NEG_INF, F32)
    l_s[...] = jnp.zeros(l_s.shape, F32)
    acc_s[...] = jnp.zeros(acc_s.shape, F32)
    vrows = HEAD_DIM + ONES_ROWS

    def logits(kt):
        k_t = k_ref[pl.ds(pl.multiple_of(kt * QB, QB), QB), :]
        expand = jnp.where(eb == (kt * QB + ej) // SLC_BLOCK, 1.0, 0.0).astype(BF16)
        return [_dot_nt(k_t[:, g * HEAD_DIM:(g + 1) * HEAD_DIM], qst[g]) + _dot(expand, pickst[g])
                for g in range(G)]

    def update(kt, s):
        vt_t = vt_s[kt]
        d = jnp.minimum(n - kt, 2)
        for g in range(G):
            _flash_tile(s[g], lambda hi, g=g: bias_ref[g * hpg + hi, d], m_s.at[g], l_s.at[g], acc_s.at[g],
                        vt_t[g * vrows:(g + 1) * vrows, :], hpg)

    _for_tiles_pairwise(n + 1, logits, update)

    gate_t = gate_ref[...].T
    for g in range(G):
        _store_heads_t(o_ref, acc_s.at[g], l_s.at[g], g * hpg, hpg, gate_t)


def slc_attention(proj, pick, gates, tbl, batch, seq):
    nq = seq // QB
    G = C_KV_HEADS
    n_slc = seq // SLC_BLOCK
    qw = C_HEADS * HEAD_DIM
    return pl.pallas_call(
        functools.partial(_slc_kernel, n_slc=n_slc),
        out_shape=jax.ShapeDtypeStruct((batch * seq, qw), BF16),
        grid=(batch, nq),
        in_specs=[
            pl.BlockSpec((QB, qw), lambda b, n: (b * nq + n, 0)),
            pl.BlockSpec((seq, LANES), lambda b, n: (b, 10)),
            pl.BlockSpec((seq, LANES), lambda b, n: (b, 11)),
            pl.BlockSpec((1, G, n_slc, QB), lambda b, n: (b, 0, 0, n)),
            pl.BlockSpec((C_HEADS, 3, QB, QB), lambda b, n: (0, 0, 0, 0)),
            pl.BlockSpec((QB, LANES), lambda b, n: (b * nq + n, 1)),
        ],
        out_specs=pl.BlockSpec((QB, qw), lambda b, n: (b * nq + n, 0)),
        scratch_shapes=[pltpu.VMEM((nq, G * (HEAD_DIM + ONES_ROWS), QB), BF16),
                        pltpu.VMEM((G, 1, C_HEADS // G * QB), F32),
                        pltpu.VMEM((G, 1, C_HEADS // G * QB), F32),
                        pltpu.VMEM((G, HEAD_DIM, C_HEADS // G * QB), F32)],
        compiler_params=_cparams("parallel", "arbitrary"),
        name="slc_attention",
    )(proj, proj, proj, pick, _tile_bias_masked_t(tbl, 3), gates)


def _xattn_kernel(h_ref, wq_ref, k_ref, v_ref, wo_ref, g_ref, b_ref, o_ref):
    x = h_ref[...]
    q = _dot(x.astype(BF16), wq_ref[...]).astype(BF16)
    k = k_ref[...]
    v = v_ref[...]
    outs = []
    for hd in range(X_HEADS):
        sl = slice(hd * X_HEAD_DIM, (hd + 1) * X_HEAD_DIM)
        lg = _dot_nt(q[:, sl], k[:, sl]) * X_HEAD_DIM ** -0.5
        e = jnp.exp(lg - jnp.max(lg, axis=-1, keepdims=True))
        den = jnp.sum(e, axis=-1, keepdims=True)
        outs.append((_dot(e.astype(BF16), v[:, sl]) / den).astype(BF16))
    o = jnp.concatenate(outs, axis=-1)
    y = ALPHA * x + _dot(o, wo_ref[...])
    o_ref[...] = _layer_norm_rows(y, g_ref[...], b_ref[...])


def cross_attention_ln(h, mem, w_q, w_k, w_v, w_o, g, b, batch, seq, tm=256):
    M = mem.shape[0] // batch
    XW = X_HEADS * X_HEAD_DIM
    D = D_MODEL
    kv = matmul(mem, jnp.concatenate([w_k, w_v], axis=1).astype(BF16), BF16, tm=512)
    nt = seq // tm
    fixed = lambda bb, i: (0, 0)
    return pl.pallas_call(
        _xattn_kernel,
        out_shape=jax.ShapeDtypeStruct((batch * seq, D), F32),
        grid=(batch, nt),
        in_specs=[pl.BlockSpec((tm, D), lambda bb, i: (bb * nt + i, 0)),
                  pl.BlockSpec((D, XW), fixed),
                  pl.BlockSpec((M, XW), lambda bb, i: (bb, 0)),
                  pl.BlockSpec((M, XW), lambda bb, i: (bb, 1)),
                  pl.BlockSpec((XW, D), fixed),
                  pl.BlockSpec((1, D), fixed), pl.BlockSpec((1, D), fixed)],
        out_specs=pl.BlockSpec((tm, D), lambda bb, i: (bb * nt + i, 0)),
        compiler_params=_cparams("parallel", "parallel"),
        name="cross_attention_ln",
    )(h, w_q.astype(BF16), kv, kv, w_o.astype(BF16), g.reshape(1, D), b.reshape(1, D))


def _silu(a):
    return a * (1.0 / (1.0 + jnp.exp(-a)))


def _swiglu_kernel(x_ref, wg_ref, wu_ref, wd_ref, g_ref, b_ref, o_ref):
    x = x_ref[...]
    xb = x.astype(BF16)
    hmid = (_silu(_dot(xb, wg_ref[...])) * _dot(xb, wu_ref[...])).astype(BF16)
    y = ALPHA * x + _dot(hmid, wd_ref[...])
    o_ref[...] = _layer_norm_rows(y, g_ref[...], b_ref[...])


def swiglu_ln(h, w_gate, w_up, w_down, g, b, tm=256):
    T, D = h.shape
    F = w_gate.shape[1]
    fixed = lambda i: (0, 0)
    once = pl.Buffered(1)
    return pl.pallas_call(
        _swiglu_kernel,
        out_shape=jax.ShapeDtypeStruct((T, D), F32),
        grid=(T // tm,),
        in_specs=[pl.BlockSpec((tm, D), lambda i: (i, 0)),
                  pl.BlockSpec((D, F), fixed, pipeline_mode=once),
                  pl.BlockSpec((D, F), fixed, pipeline_mode=once),
                  pl.BlockSpec((F, D), fixed, pipeline_mode=once),
                  pl.BlockSpec((1, D), fixed), pl.BlockSpec((1, D), fixed)],
        out_specs=pl.BlockSpec((tm, D), lambda i: (i, 0)),
        compiler_params=_cparams("parallel"),
        name="swiglu_ln",
    )(h, w_gate.astype(BF16), w_up.astype(BF16), w_down.astype(BF16), g.reshape(1, D), b.reshape(1, D))


def _router_kernel(x_ref, w_ref, idx_ref, wgt_ref, xlo_ref, xhi_ref):
    x = x_ref[...]
    w = w_ref[...]
    x_hi = x.astype(BF16)
    x_lo = (x - x_hi.astype(F32)).astype(BF16)
    w_hi = w.astype(BF16)
    w_lo = (w - w_hi.astype(F32)).astype(BF16)
    lg = _dot(x_hi, w_hi) + (_dot(x_hi, w_lo) + _dot(x_lo, w_hi))
    e_iota = lax.broadcasted_iota(jnp.int32, lg.shape, 1).astype(F32)
    m1 = jnp.max(lg, axis=-1, keepdims=True)
    i1 = jnp.min(jnp.where(lg == m1, e_iota, float(N_EXPERTS)), axis=-1, keepdims=True)
    first = e_iota == i1
    rest = jnp.where(first, -jnp.inf, lg)
    m2 = jnp.max(rest, axis=-1, keepdims=True)
    i2 = jnp.min(jnp.where(rest == m2, e_iota, float(N_EXPERTS)), axis=-1, keepdims=True)
    w2 = jnp.exp(m2 - m1)
    den = 1.0 + w2
    k_iota = lax.broadcasted_iota(jnp.int32, idx_ref.shape, 1)
    idx_ref[...] = jnp.where(k_iota == 0, i1, i2).astype(jnp.int32)
    wgt_ref[...] = jnp.where(k_iota == 0, 1.0 / den, w2 / den)
    half = x.shape[1] // 2
    xlo_ref[...] = _pack_bf16_pairs(x[:, :half])
    xhi_ref[...] = _pack_bf16_pairs(x[:, half:])


def _pack_bf16_pairs(x):
    w = x.shape[1] // 2
    lo = lax.bitcast_convert_type(x[:, :w].astype(BF16).astype(F32), jnp.uint32)
    hi = lax.bitcast_convert_type(x[:, w:].astype(BF16).astype(F32), jnp.uint32)
    return (lo >> 16) | (hi & jnp.uint32(0xFFFF0000))


def _unpack_bf16_pairs(words):
    lo = lax.bitcast_convert_type(words << 16, F32).astype(BF16)
    hi = lax.bitcast_convert_type(words & jnp.uint32(0xFFFF0000), F32).astype(BF16)
    return jnp.concatenate([lo, hi], axis=1)


def moe_route(h, w_router, tm=1024):
    T, D = h.shape
    E = w_router.shape[1]
    row = lambda i: (i, 0)
    return pl.pallas_call(
        _router_kernel,
        out_shape=(jax.ShapeDtypeStruct((T, TOP_K), jnp.int32), jax.ShapeDtypeStruct((T, TOP_K), F32),
                   jax.ShapeDtypeStruct((T, D // 4), jnp.uint32), jax.ShapeDtypeStruct((T, D // 4), jnp.uint32)),
        grid=(T // tm,),
        in_specs=[pl.BlockSpec((tm, D), row), pl.BlockSpec((D, E), lambda i: (0, 0))],
        out_specs=(pl.BlockSpec((tm, TOP_K), row), pl.BlockSpec((tm, TOP_K), row),
                   pl.BlockSpec((tm, D // 4), row), pl.BlockSpec((tm, D // 4), row)),
        compiler_params=_cparams("parallel"),
        name="moe_route",
    )(h, w_router)


def sc_gather_rows(xs, idx, window):
    n_idx = idx.shape[0]
    n_arr = len(xs)
    mesh = plsc.VectorSubcoreMesh(core_axis_name="core", subcore_axis_name="subcore")

    @pl.kernel(out_type=[jax.ShapeDtypeStruct((n_idx, x.shape[1]), x.dtype) for x in xs], mesh=mesh,
               scratch_types=[], name="sc_gather_rows")
    def gather(*refs):
        x_hbms, i_hbm, o_hbms = refs[:n_arr], refs[n_arr], refs[n_arr + 1:]
        for x_hbm, o_hbm in zip(x_hbms, o_hbms):
            def body(i_vmem, o_vmem, x_hbm=x_hbm):
                pltpu.sync_copy(x_hbm.at[i_vmem.at[0]], o_vmem)

            pltpu.emit_pipeline(
                body,
                grid=(n_idx // window,),
                in_specs=[pl.BlockSpec((1, window), lambda i: (0, i))],
                out_specs=[pl.BlockSpec((window, x_hbm.shape[1]), lambda i: (i, 0))],
                core_axis_name=("core", "subcore"),
                dimension_semantics=(pltpu.PARALLEL,),
            )(i_hbm, o_hbm)

    return gather(*xs, idx.reshape(1, n_idx))


def _moe_plan(top_idx, tm):
    T = top_idx.shape[0]
    n_asg = T * TOP_K
    n_tiles = n_asg // tm + N_EXPERTS
    flat_e = top_idx.reshape(n_asg)
    onehot = (flat_e[:, None] == jnp.arange(N_EXPERTS, dtype=jnp.int32)[None, :]).astype(jnp.int32)
    incl = jnp.cumsum(onehot, axis=0)
    counts = incl[-1]
    rank = jnp.sum((incl - onehot) * onehot, axis=1)
    tiles_e = (counts + tm - 1) // tm
    tile_end = jnp.cumsum(tiles_e)
    row_start_p = (tile_end - tiles_e) * tm
    row_start = jnp.cumsum(counts) - counts
    pos = (jnp.sum(onehot * row_start_p[None, :], axis=1) + rank).reshape(T, TOP_K).T.reshape(n_asg)
    tile_ids = jnp.arange(n_tiles, dtype=jnp.int32)
    tile_expert = jnp.minimum(jnp.searchsorted(tile_end, tile_ids, side="right"), N_EXPERTS - 1).astype(jnp.int32)
    n_rows = n_tiles * tm
    sorted_tok = jnp.argsort(flat_e, stable=True).astype(jnp.int32) // TOP_K
    window_src = jnp.concatenate([jnp.zeros((n_rows,), jnp.int32), sorted_tok, jnp.zeros((n_rows,), jnp.int32)])
    r = jnp.arange(n_rows, dtype=jnp.int32)
    src_tok = jnp.zeros((n_rows,), jnp.int32)
    for e in range(N_EXPERTS):
        shifted = lax.dynamic_slice(window_src, (n_rows - (row_start_p[e] - row_start[e]),), (n_rows,))
        in_run = (r >= row_start_p[e]) & (r < row_start_p[e] + counts[e])
        src_tok = jnp.where(in_run, shifted, src_tok)
    return src_tok, pos.astype(jnp.int32), tile_expert, tile_end[-1:].astype(jnp.int32)


def _moe_ffn_kernel(te_ref, nu_ref, xlo_ref, xhi_ref, wg_ref, wu_ref, wd_ref, olo_ref, ohi_ref, acc_s):
    i = pl.program_id(0)
    c = pl.program_id(1)
    last = pl.num_programs(1) - 1
    used = i < nu_ref[0]
    half = acc_s.shape[1] // 2

    @pl.when(used & (c == 0))
    def _():
        acc_s[...] = jnp.zeros(acc_s.shape, F32)

    @pl.when(used)
    def _():
        x = jnp.concatenate([_unpack_bf16_pairs(xlo_ref[...]), _unpack_bf16_pairs(xhi_ref[...])], axis=1)
        hmid = _silu(_dot(x, wg_ref[0])) * _dot(x, wu_ref[0])
        acc_s[...] += _dot(hmid.astype(BF16), wd_ref[0])

    @pl.when(used & (c == last))
    def _():
        olo_ref[...] = _pack_bf16_pairs(acc_s[:, :half])
        ohi_ref[...] = _pack_bf16_pairs(acc_s[:, half:])

    @pl.when(jnp.logical_not(used) & (c == last))
    def _():
        olo_ref[...] = jnp.zeros(olo_ref.shape, olo_ref.dtype)
        ohi_ref[...] = jnp.zeros(ohi_ref.shape, ohi_ref.dtype)


def moe_expert_ffn(xs_lo, xs_hi, tile_expert, n_used, w_gate, w_up, w_down, tm, fc=896):
    R, half = xs_lo.shape
    D = 4 * half
    E, _, F = w_gate.shape
    rows = lambda i, c, te, nu: (i, 0)
    return pl.pallas_call(
        _moe_ffn_kernel,
        out_shape=(jax.ShapeDtypeStruct((R, half), jnp.uint32), jax.ShapeDtypeStruct((R, half), jnp.uint32)),
        grid_spec=pltpu.PrefetchScalarGridSpec(
            num_scalar_prefetch=2,
            grid=(R // tm, F // fc),
            in_specs=[pl.BlockSpec((tm, half), rows), pl.BlockSpec((tm, half), rows),
                      pl.BlockSpec((1, D, fc), lambda i, c, te, nu: (te[i], 0, c)),
                      pl.BlockSpec((1, D, fc), lambda i, c, te, nu: (te[i], 0, c)),
                      pl.BlockSpec((1, fc, D), lambda i, c, te, nu: (te[i], c, 0))],
            out_specs=(pl.BlockSpec((tm, half), rows), pl.BlockSpec((tm, half), rows)),
            scratch_shapes=[pltpu.VMEM((tm, D), F32)]),
        compiler_params=_cparams("parallel", "arbitrary"),
        name="moe_expert_ffn",
    )(tile_expert, n_used, xs_lo, xs_hi, w_gate.astype(BF16), w_up.astype(BF16), w_down.astype(BF16))


def _moe_combine_kernel(h_ref, lo0_ref, hi0_ref, lo1_ref, hi1_ref, w_ref, g_ref, b_ref, o_ref):
    w = w_ref[...]
    y0 = jnp.concatenate([_unpack_bf16_pairs(lo0_ref[...]), _unpack_bf16_pairs(hi0_ref[...])], axis=1).astype(F32)
    y1 = jnp.concatenate([_unpack_bf16_pairs(lo1_ref[...]), _unpack_bf16_pairs(hi1_ref[...])], axis=1).astype(F32)
    ff = w[:, 0:1] * y0 + w[:, 1:2] * y1
    o_ref[...] = _layer_norm_rows(ALPHA * h_ref[...] + ff, g_ref[...], b_ref[...])


def moe_combine_ln(h, y_lo, y_hi, top_w, g, b, tm=512):
    T, D = h.shape
    half = D // 4
    nt = T // tm
    row = lambda i: (i, 0)
    second = lambda i: (nt + i, 0)
    fixed = lambda i: (0, 0)
    return pl.pallas_call(
        _moe_combine_kernel,
        out_shape=jax.ShapeDtypeStruct((T, D), F32),
        grid=(nt,),
        in_specs=[pl.BlockSpec((tm, D), row),
                  pl.BlockSpec((tm, half), row), pl.BlockSpec((tm, half), row),
                  pl.BlockSpec((tm, half), second), pl.BlockSpec((tm, half), second),
                  pl.BlockSpec((tm, TOP_K), row),
                  pl.BlockSpec((1, D), fixed), pl.BlockSpec((1, D), fixed)],
        out_specs=pl.BlockSpec((tm, D), row),
        compiler_params=_cparams("parallel"),
        name="moe_combine_ln",
    )(h, y_lo, y_hi, y_lo, y_hi, top_w, g.reshape(1, D), b.reshape(1, D))


def moe_ln(h, w_router, w_gate, w_up, w_down, g, b, tm=512):
    top_idx, top_w, hb_lo, hb_hi = moe_route(h, w_router)
    src_tok, pos, tile_expert, n_used = _moe_plan(top_idx, tm)
    xs_lo, xs_hi = sc_gather_rows([hb_lo, hb_hi], src_tok, SC_GATHER_WINDOW)
    ys_lo, ys_hi = moe_expert_ffn(xs_lo, xs_hi, tile_expert, n_used, w_gate, w_up, w_down, tm)
    y_lo, y_hi = sc_gather_rows([ys_lo, ys_hi], pos, SC_GATHER_WINDOW)
    return moe_combine_ln(h, y_lo, y_hi, top_w, g, b)


def _pad_cols(w, width):
    return jnp.pad(w, ((0, 0), (0, width - w.shape[1])))


def even_layer_mixer(h, w_in, ckv_gain, w_uk, w_uv, sinks, w_out, rel_table, ln_g, ln_b, batch, seq):
    cuts = np.cumsum((A_HEADS * HEAD_DIM, A_LATENT, IDX_HEADS * IDX_DIM, IDX_DIM, IDX_HEADS,
                      B_HEADS * HEAD_DIM, B_KV_HEADS * HEAD_DIM))
    w_qa, w_ckv, w_qi, w_ki, w_wi, w_qb, w_kb, w_vb = jnp.split(w_in, [int(c) for c in cuts], axis=1)
    w_main = jnp.concatenate([w_qa, w_qi, w_qb, _pad_cols(w_ki, LANES), w_kb, w_vb], axis=1).astype(BF16)
    w_aux = jnp.concatenate([w_ckv, _pad_cols(w_wi, LANES)], axis=1).astype(BF16)
    proj = matmul(h, w_main, BF16)
    aux = matmul(h, w_aux, F32)
    o_a = dsa_attention(proj, aux, ckv_gain, w_uk, w_uv, rel_table[:, :A_HEADS], batch, seq)
    o_b = banded_attention(proj, 2, 13, 14, B_HEADS, B_KV_HEADS,
                           rel_table[:, A_HEADS:A_HEADS + B_HEADS], B_WINDOW,
                           batch, seq, sinks=sinks)
    o = jnp.concatenate([o_a, o_b], axis=1)
    return matmul_residual_ln([o], w_out.astype(BF16), h, ln_g, ln_b)


def odd_layer_mixer(h, w_in, pe_k, w1_k, w2_k, pe_v, w1_v, w2_v, w_out, rel_table, ln_g, ln_b, batch, seq):
    kvw = C_KV_HEADS * HEAD_DIM
    qw = C_HEADS * HEAD_DIM
    w_main = w_in[:, :qw + 6 * kvw].astype(BF16)
    w_gl = w_in[:, qw + 6 * kvw:].reshape(D_MODEL, C_HEADS, 3)
    w_gate = jnp.concatenate([_pad_cols(w_gl[:, :, r], LANES) for r in range(3)], axis=1).astype(BF16)
    proj = matmul(h, w_main, BF16)
    gates = matmul(h, w_gate, F32)
    tbl = rel_table[:, :C_HEADS]
    kc = compress_blocks(proj[:, qw:qw + kvw], pe_k, w1_k, w2_k, batch, seq)
    vc = compress_blocks(proj[:, qw + kvw:qw + 2 * kvw], pe_v, w1_v, w2_v, batch, seq)
    o_cmp, pick = cmp_attention(proj, kc, vc, gates, tbl, batch, seq)
    o_slc = slc_attention(proj, pick, gates, tbl, batch, seq)
    o_win = banded_attention(proj, 0, 12, 13, C_HEADS, C_KV_HEADS, tbl, C_WINDOW,
                             batch, seq, gates=gates, gate_col=2)
    return matmul_residual_ln([o_cmp, o_slc, o_win], w_out.astype(BF16), h, ln_g, ln_b)


def kernel(x, mem, rel_table, ev_w_in, ev_ckv_gain, ev_w_uk, ev_w_uv, ev_sinks, ev_w_out, od_w_in, od_pe_k, od_w1_k, od_w2_k, od_pe_v, od_w1_v, od_w2_v, od_w_out, xa_w_q, xa_w_k, xa_w_v, xa_w_o, ff_w_gate, ff_w_up, ff_w_down, moe_w_router, moe_w_gate, moe_w_up, moe_w_down, ln_g, ln_b):
    batch, seq, D = x.shape
    h = x.reshape(batch * seq, D)
    mem2 = mem.reshape(-1, D)
    for i in range(DEPTH):
        j = i // 2
        if i % 2 == 0:
            h = even_layer_mixer(h, ev_w_in[j], ev_ckv_gain[j], ev_w_uk[j], ev_w_uv[j], ev_sinks[j],
                                 ev_w_out[j], rel_table, ln_g[i, 0], ln_b[i, 0], batch, seq)
        else:
            h = odd_layer_mixer(h, od_w_in[j], od_pe_k[j], od_w1_k[j], od_w2_k[j], od_pe_v[j],
                                od_w1_v[j], od_w2_v[j], od_w_out[j], rel_table, ln_g[i, 0], ln_b[i, 0],
                                batch, seq)
        h = cross_attention_ln(h, mem2, xa_w_q[i], xa_w_k[i], xa_w_v[i], xa_w_o[i],
                               ln_g[i, 1], ln_b[i, 1], batch, seq)
        if i % 2 == 0:
            h = swiglu_ln(h, ff_w_gate[j], ff_w_up[j], ff_w_down[j], ln_g[i, 2], ln_b[i, 2])
        else:
            h = moe_ln(h, moe_w_router[j], moe_w_gate[j], moe_w_up[j], moe_w_down[j], ln_g[i, 2], ln_b[i, 2])
    return h.reshape(batch, seq, D)
```

```python
import functools
import math

import numpy as np
import jax
import jax.numpy as jnp
from jax import lax
from jax.experimental import pallas as pl
from jax.experimental.pallas import tpu as pltpu
from jax.experimental.pallas import tpu_sc as plsc

D_MODEL = 1024
DEPTH = 2
HEAD_DIM = 64
N_SLOTS = D_MODEL // HEAD_DIM
A_HEADS = 8
A_LATENT = 128
IDX_HEADS = 8
IDX_DIM = 64
A_TOPK_MAX = 256
B_HEADS = 8
B_KV_HEADS = 2
B_WINDOW = 128
C_HEADS = 16
C_KV_HEADS = 2
CMP_BLOCK = 32
CMP_STRIDE = 16
CMP_HIDDEN = 128
SLC_BLOCK = 64
SLC_COUNT = 8
C_WINDOW = 512
X_HEADS = 4
X_HEAD_DIM = 128
D_FF = 2816
N_EXPERTS = 8
TOP_K = 2
D_FF_EXPERT = 3584
REL_BUCKETS = 32
REL_MAX_DIST = 128
LN_EPS = 1e-5
NEG_INF = -1e30
FORCE_BONUS = 1e6
ALPHA = (2 * DEPTH) ** 0.25
MASK_BIG = 2.0 ** 100
LOG2E = math.log2(math.e)
ONES_ROWS = 16

LANES = 128
QB = 128
VMEM_LIMIT_BYTES = 56 * 1024 * 1024
SC_GATHER_WINDOW = 128

F32 = jnp.float32
BF16 = jnp.bfloat16


def _cparams(*sem):
    return pltpu.CompilerParams(dimension_semantics=sem, vmem_limit_bytes=VMEM_LIMIT_BYTES)


def _dot(a, b):
    return jnp.dot(a, b, preferred_element_type=F32)


def _dot_nt(a, b):
    return lax.dot_general(a, b, (((1,), (1,)), ((), ())), preferred_element_type=F32)


def _layer_norm_rows(y, g, b):
    mu = jnp.mean(y, axis=-1, keepdims=True)
    yc = y - mu
    var = jnp.mean(yc * yc, axis=-1, keepdims=True)
    return yc * lax.rsqrt(var + LN_EPS) * g + b


def _mm_kernel(x_ref, w_ref, o_ref):
    o_ref[...] = _dot(x_ref[...].astype(BF16), w_ref[...]).astype(o_ref.dtype)


def matmul(x, w, out_dtype, tm=512):
    M, K = x.shape
    N = w.shape[1]
    tm = min(tm, M)
    return pl.pallas_call(
        _mm_kernel,
        out_shape=jax.ShapeDtypeStruct((M, N), out_dtype),
        grid=(M // tm,),
        in_specs=[pl.BlockSpec((tm, K), lambda i: (i, 0)),
                  pl.BlockSpec((K, N), lambda i: (0, 0))],
        out_specs=pl.BlockSpec((tm, N), lambda i: (i, 0)),
        compiler_params=_cparams("parallel"),
        name="matmul",
    )(x, w)


def _cast_kernel(x_ref, o_ref):
    o_ref[...] = x_ref[...].astype(o_ref.dtype)


def cast_bf16(w, block_bytes=8 * 1024 * 1024):
    shape = w.shape
    C = shape[-1]
    M = int(np.prod(shape[:-1]))
    tm = M
    while tm * C * 4 > block_bytes and tm % 32 == 0:
        tm //= 2
    out = pl.pallas_call(
        _cast_kernel,
        out_shape=jax.ShapeDtypeStruct((M, C), BF16),
        grid=(M // tm,),
        in_specs=[pl.BlockSpec((tm, C), lambda i: (i, 0))],
        out_specs=pl.BlockSpec((tm, C), lambda i: (i, 0)),
        compiler_params=_cparams("parallel"),
        name="cast_bf16",
    )(w.reshape(M, C))
    return out.reshape(shape)


def _mm_res_ln_kernel(*refs, n_in):
    a_refs = refs[:n_in]
    w_ref, h_ref, g_ref, b_ref, o_ref = refs[n_in:]
    a = a_refs[0][...]
    if n_in > 1:
        a = a.astype(F32)
        for r in a_refs[1:]:
            a = a + r[...].astype(F32)
    y = ALPHA * h_ref[...] + _dot(a.astype(BF16), w_ref[...])
    o_ref[...] = _layer_norm_rows(y, g_ref[...], b_ref[...])


def matmul_residual_ln(a_list, w, h, g, b, tm=512):
    M, K = a_list[0].shape
    N = w.shape[1]
    n_in = len(a_list)
    row = lambda i: (i, 0)
    fixed = lambda i: (0, 0)
    return pl.pallas_call(
        functools.partial(_mm_res_ln_kernel, n_in=n_in),
        out_shape=jax.ShapeDtypeStruct((M, N), F32),
        grid=(M // tm,),
        in_specs=[pl.BlockSpec((tm, K), row)] * n_in + [
            pl.BlockSpec((K, N), fixed), pl.BlockSpec((tm, N), row),
            pl.BlockSpec((1, N), fixed), pl.BlockSpec((1, N), fixed)],
        out_specs=pl.BlockSpec((tm, N), row),
        compiler_params=_cparams("parallel"),
        name="matmul_residual_ln",
    )(*a_list, w, h, g.reshape(1, N), b.reshape(1, N))


def _rel_bucket(dist):
    n = jnp.maximum(dist, 0)
    max_exact = REL_BUCKETS // 2
    nf = jnp.maximum(n, 1).astype(F32)
    log_b = max_exact + (jnp.log(nf / max_exact) / math.log(REL_MAX_DIST / max_exact)
                         * (REL_BUCKETS - max_exact)).astype(jnp.int32)
    return jnp.where(n < max_exact, n, jnp.minimum(log_b, REL_BUCKETS - 1))


def _bias_lookup(tbl, dist):
    onehot = (_rel_bucket(dist)[..., None] == jnp.arange(REL_BUCKETS, dtype=jnp.int32)).astype(F32)
    return jnp.einsum("...r,rh->...h", onehot, tbl.astype(F32), precision=lax.Precision.HIGHEST)


def _tile_bias_masked_t(tbl, n_tiles, window=None):
    j = np.arange(QB)[:, None]
    i = np.arange(QB)[None, :]
    dist = np.stack([d * QB + i - j for d in range(n_tiles)])
    valid = dist >= 0
    if window is not None:
        valid &= dist < window
    rel = _bias_lookup(tbl, jnp.asarray(dist)) - _far_bias(tbl)
    return jnp.where(jnp.asarray(valid)[..., None], rel * LOG2E, NEG_INF).transpose(3, 0, 1, 2)


def _far_bias(tbl):
    return tbl.astype(F32)[REL_BUCKETS - 1]


def _flash_tile(s, add, m_ref, l_ref, acc_ref, vt_ones, hpg):
    dv = acc_ref.shape[0]
    scale, probs = [], []
    for hi in range(hpg):
        sl = slice(hi * QB, (hi + 1) * QB)
        sh = s[:, sl] if add is None else s[:, sl] + add(hi)
        m_old = m_ref[:, sl]
        m_new = jnp.maximum(m_old, jnp.max(sh, axis=0, keepdims=True))
        scale.append(jnp.exp2(m_old - m_new))
        probs.append(jnp.exp2(sh - m_new).astype(BF16))
        m_ref[:, sl] = m_new
    pv = _dot(vt_ones, jnp.concatenate(probs, axis=1))
    for hi in range(hpg):
        sl = slice(hi * QB, (hi + 1) * QB)
        acc_ref[:, sl] = scale[hi] * acc_ref[:, sl] + pv[:dv, sl]
        l_ref[:, sl] = scale[hi] * l_ref[:, sl] + pv[dv:dv + 1, sl]


def _value_tile_t(v_tile, groups):
    vt = v_tile.astype(F32).T.astype(BF16)
    ones = jnp.ones((ONES_ROWS, v_tile.shape[0]), BF16)
    parts = []
    for g in range(groups):
        parts += [vt[g * HEAD_DIM:(g + 1) * HEAD_DIM, :], ones]
    return jnp.concatenate(parts, axis=0)


def _for_tiles_pairwise(n_tiles, start, finish):
    def pair(i, carry):
        first = start(2 * i)
        second = start(2 * i + 1)
        finish(2 * i, first)
        finish(2 * i + 1, second)
        return carry

    lax.fori_loop(0, lax.shift_right_logical(n_tiles, 1), pair, 0)

    @pl.when((n_tiles & 1) == 1)
    def _():
        finish(n_tiles - 1, start(n_tiles - 1))


def _stack_heads(q, first, count, scale):
    return jnp.concatenate([q[:, (first + hi) * HEAD_DIM:(first + hi + 1) * HEAD_DIM] * scale
                            for hi in range(count)], axis=0)


def _store_heads_t(o_ref, acc_ref, l_ref, first, count, gate_t):
    for hi in range(count):
        h = first + hi
        sl = slice(hi * QB, (hi + 1) * QB)
        o = acc_ref[:, sl] / l_ref[:, sl]
        if gate_t is not None:
            o = o * (1.0 / (1.0 + jnp.exp(-gate_t[h:h + 1, :])))
        o_ref[:, h * HEAD_DIM:(h + 1) * HEAD_DIM] = o.T.astype(o_ref.dtype)


def _band_kernel(*refs, n_tiles, kvh, g, has_sinks, has_gate):
    it = iter(refs)
    q_ref, k_ref, v_ref, bias_ref = next(it), next(it), next(it), next(it)
    sink_ref = next(it) if has_sinks else None
    gate_ref = next(it) if has_gate else None
    o_ref, vt_s, m_s, l_s, acc_s = next(it), next(it), next(it), next(it), next(it)
    n = pl.program_id(1)
    nq = vt_s.shape[0]

    @pl.when(n == 0)
    def _():
        for kt in range(nq):
            vt_s[kt] = _value_tile_t(v_ref[kt * QB:(kt + 1) * QB, :], kvh)

    q = q_ref[...]
    qst = [_stack_heads(q, kv * g, g, HEAD_DIM ** -0.5 * LOG2E) for kv in range(kvh)]
    if has_sinks:
        m_s[...] = sink_ref[...] * LOG2E
        l_s[...] = jnp.ones(l_s.shape, F32)
    else:
        m_s[...] = jnp.full(m_s.shape, NEG_INF, F32)
        l_s[...] = jnp.zeros(l_s.shape, F32)
    acc_s[...] = jnp.zeros(acc_s.shape, F32)
    vrows = HEAD_DIM + ONES_ROWS

    def logits(d):
        k_t = k_ref[pl.ds(pl.multiple_of((n - d) * QB, QB), QB), :]
        return [_dot_nt(k_t[:, kv * HEAD_DIM:(kv + 1) * HEAD_DIM], qst[kv]) for kv in range(kvh)]

    def update(d, s):
        vt_t = vt_s[n - d]
        unbiased = 2 <= d <= n_tiles - 2
        for kv in range(kvh):
            _flash_tile(s[kv], None if unbiased else (lambda hi, kv=kv: bias_ref[kv * g + hi, d]),
                        m_s.at[kv], l_s.at[kv], acc_s.at[kv], vt_t[kv * vrows:(kv + 1) * vrows, :], g)

    @pl.when(n >= n_tiles - 1)
    def _():
        s_next = logits(0)
        for d in range(n_tiles):
            s_cur = s_next
            if d + 1 < n_tiles:
                s_next = logits(d + 1)
            update(d, s_cur)

    @pl.when(n < n_tiles - 1)
    def _():
        for d in range(n_tiles - 1):
            @pl.when(n >= d)
            def _(d=d):
                update(d, logits(d))

    gate_t = gate_ref[...].T if has_gate else None
    for kv in range(kvh):
        _store_heads_t(o_ref, acc_s.at[kv], l_s.at[kv], kv * g, g, gate_t)


def banded_attention(proj, q_col, k_col, v_col, n_heads, kvh, tbl, window, batch, seq,
                     sinks=None, gates=None, gate_col=0):
    g = n_heads // kvh
    kw = kvh * HEAD_DIM
    qw = n_heads * HEAD_DIM
    nq = seq // QB
    n_tiles = window // QB + 1
    args = [proj, proj, proj, _tile_bias_masked_t(tbl, n_tiles, window)]
    in_specs = [pl.BlockSpec((QB, qw), lambda b, n: (b * nq + n, q_col)),
                pl.BlockSpec((seq, kw), lambda b, n: (b, k_col)),
                pl.BlockSpec((seq, kw), lambda b, n: (b, v_col)),
                pl.BlockSpec((n_heads, n_tiles, QB, QB), lambda b, n: (0, 0, 0, 0))]
    if sinks is not None:
        args.append(jnp.repeat((sinks.astype(F32) - _far_bias(tbl)).reshape(kvh, 1, g), QB, axis=2))
        in_specs.append(pl.BlockSpec((kvh, 1, g * QB), lambda b, n: (0, 0, 0)))
    if gates is not None:
        args.append(gates)
        in_specs.append(pl.BlockSpec((QB, LANES), lambda b, n: (b * nq + n, gate_col)))
    return pl.pallas_call(
        functools.partial(_band_kernel, n_tiles=n_tiles, kvh=kvh, g=g,
                          has_sinks=sinks is not None, has_gate=gates is not None),
        out_shape=jax.ShapeDtypeStruct((batch * seq, qw), BF16),
        grid=(batch, nq),
        in_specs=in_specs,
        out_specs=pl.BlockSpec((QB, qw), lambda b, n: (b * nq + n, 0)),
        scratch_shapes=[pltpu.VMEM((nq, kvh * (HEAD_DIM + ONES_ROWS), QB), BF16),
                        pltpu.VMEM((kvh, 1, g * QB), F32), pltpu.VMEM((kvh, 1, g * QB), F32),
                        pltpu.VMEM((kvh, HEAD_DIM, g * QB), F32)],
        compiler_params=_cparams("parallel", "arbitrary"),
        name="banded_attention",
    )(*args)


def _sort_key(x):
    bits = lax.bitcast_convert_type(x, jnp.int32)
    return bits ^ ((bits >> 31) & jnp.int32(0x7FFFFFFF))


def _dsa_kernel(qa_ref, qi_ref, ki_ref, ckv_ref, wi_ref, gain_ref, wuk_ref, wuv_ref, bias_ref,
                o_ref, c_s, ct_s, key_s, m_s, l_s, acc_s, *, topk):
    n = pl.program_id(1)
    nt = n + 1
    nq = c_s.shape[0]

    @pl.when(n == 0)
    def _():
        for kt in range(nq):
            c = ckv_ref[kt * QB:(kt + 1) * QB, :]
            y = c * lax.rsqrt(jnp.mean(c * c, axis=-1, keepdims=True) + LN_EPS) * gain_ref[...]
            c_s[kt] = y.astype(BF16)
            ct_s[kt] = jnp.concatenate([y.T.astype(BF16), jnp.ones((ONES_ROWS, QB), BF16)], axis=0)

    qa = qa_ref[...]
    qi = qi_ref[...]
    qlat = jnp.concatenate(
        [(_dot(qa[:, h * HEAD_DIM:(h + 1) * HEAD_DIM], wuk_ref[h]) * (HEAD_DIM ** -0.5 * LOG2E)).astype(BF16)
         for h in range(A_HEADS)], axis=0)
    qidx = jnp.concatenate(
        [qi[:, h * IDX_DIM:(h + 1) * IDX_DIM] * IDX_DIM ** -0.5 for h in range(IDX_HEADS)], axis=0)
    wi_t = wi_ref[...].T
    wi_row = jnp.concatenate([wi_t[h:h + 1, :] for h in range(IDX_HEADS)], axis=1) * IDX_HEADS ** -0.5

    kpos0 = lax.broadcasted_iota(jnp.int32, (QB, QB), 0)
    qpos = n * QB + lax.broadcasted_iota(jnp.int32, (QB, QB), 1)

    def index_logits(kt):
        k_t = ki_ref[pl.ds(pl.multiple_of(kt * QB, QB), QB), :][:, :IDX_DIM]
        return _dot_nt(k_t, qidx)

    def score_tile(kt, lg):
        act = jnp.maximum(lg, 0.0) * wi_row
        sc = act[:, :QB]
        for h in range(1, IDX_HEADS):
            sc = sc + act[:, h * QB:(h + 1) * QB]
        sc = jnp.where(kt * QB + kpos0 <= qpos, sc, NEG_INF)
        sc = jnp.where(sc == 0.0, 0.0, sc)
        key_s[kt] = _sort_key(sc)

    _for_tiles_pairwise(nt, index_logits, score_tile)

    def count(pred):
        def hits(kt):
            return jnp.where(pred(kt, key_s[kt]), 1.0, 0.0)

        def pair(i, acc):
            return acc + (hits(2 * i) + hits(2 * i + 1))
        acc = lax.fori_loop(0, lax.shift_right_logical(nt, 1), pair, jnp.zeros((QB, QB), F32))
        acc = acc + jnp.where((nt & 1) == 1, hits(nt - 1), 0.0)
        return jnp.sum(acc, axis=0, keepdims=True)

    def thr_step(i, lo):
        cand = lo + (jnp.int32(1) << (31 - i))
        cnt = count(lambda kt, key: key >= cand)
        return jnp.where(cnt >= topk, cand, lo)

    thr = lax.fori_loop(0, 32, thr_step, jnp.full((1, QB), -2 ** 31, jnp.int32))
    need = topk - count(lambda kt, key: key > thr)

    def tie_step(i, j0):
        cand = j0 + (jnp.int32(1) << (10 - i))
        cnt = count(lambda kt, key: (key == thr) & (kt * QB + kpos0 < cand))
        return jnp.where(cnt < need, cand, j0)

    causal_ties = count(lambda kt, key: (key == thr) & (kt * QB + kpos0 <= qpos))
    must_search = jnp.max(jnp.where(causal_ties > need, 1.0, 0.0)) > 0.5
    j0 = lax.cond(must_search,
                  lambda: lax.fori_loop(0, 11, tie_step, jnp.zeros((1, QB), jnp.int32)),
                  lambda: jnp.full((1, QB), nq * QB, jnp.int32))

    m_s[...] = jnp.full(m_s.shape, NEG_INF, F32)
    l_s[...] = jnp.zeros(l_s.shape, F32)
    acc_s[...] = jnp.zeros(acc_s.shape, F32)

    def attend_logits(kt):
        return _dot_nt(c_s[kt], qlat)

    def attend_update(kt, s, d):
        key = key_s[kt]
        kpos = kt * QB + kpos0
        ok = ((key > thr) | ((key == thr) & (kpos <= j0))) & (kpos <= qpos)
        scale, probs = [], []
        for h in range(A_HEADS):
            sl = slice(h * QB, (h + 1) * QB)
            sh = jnp.where(ok, s[:, sl] if d is None else s[:, sl] + bias_ref[h, d], NEG_INF)
            m_old = m_s[:, sl]
            m_new = jnp.maximum(m_old, jnp.max(sh, axis=0, keepdims=True))
            scale.append(jnp.exp2(m_old - m_new))
            probs.append(jnp.where(ok, jnp.exp2(sh - m_new), 0.0).astype(BF16))
            m_s[:, sl] = m_new
        pv = _dot(ct_s[kt], jnp.concatenate(probs, axis=1))
        for h in range(A_HEADS):
            sl = slice(h * QB, (h + 1) * QB)
            acc_s[:, sl] = scale[h] * acc_s[:, sl] + pv[:A_LATENT, sl]
            l_s[:, sl] = scale[h] * l_s[:, sl] + pv[A_LATENT:A_LATENT + 1, sl]

    _for_tiles_pairwise(jnp.maximum(n - 1, 0), attend_logits, lambda kt, s: attend_update(kt, s, None))

    @pl.when(n >= 1)
    def _():
        s_prev = attend_logits(n - 1)
        s_diag = attend_logits(n)
        attend_update(n - 1, s_prev, 1)
        attend_update(n, s_diag, 0)

    @pl.when(n == 0)
    def _():
        attend_update(n, attend_logits(n), 0)

    for h in range(A_HEADS):
        sl = slice(h * QB, (h + 1) * QB)
        o_lat = (acc_s[:, sl] / l_s[:, sl]).T
        o_ref[:, h * HEAD_DIM:(h + 1) * HEAD_DIM] = _dot(o_lat.astype(BF16), wuv_ref[h]).astype(o_ref.dtype)


def dsa_attention(proj, aux, gain, w_uk, w_uv, tbl, batch, seq):
    nq = seq // QB
    topk = min(A_TOPK_MAX, seq // 4)
    qw = A_HEADS * HEAD_DIM
    return pl.pallas_call(
        functools.partial(_dsa_kernel, topk=topk),
        out_shape=jax.ShapeDtypeStruct((batch * seq, qw), BF16),
        grid=(batch, nq),
        in_specs=[
            pl.BlockSpec((QB, qw), lambda b, n: (b * nq + n, 0)),
            pl.BlockSpec((QB, qw), lambda b, n: (b * nq + n, 1)),
            pl.BlockSpec((seq, LANES), lambda b, n: (b, 12)),
            pl.BlockSpec((seq, A_LATENT), lambda b, n: (b, 0)),
            pl.BlockSpec((QB, LANES), lambda b, n: (b * nq + n, 1)),
            pl.BlockSpec((1, A_LATENT), lambda b, n: (0, 0)),
            pl.BlockSpec((A_HEADS, HEAD_DIM, A_LATENT), lambda b, n: (0, 0, 0)),
            pl.BlockSpec((A_HEADS, A_LATENT, HEAD_DIM), lambda b, n: (0, 0, 0)),
            pl.BlockSpec((A_HEADS, 2, QB, QB), lambda b, n: (0, 0, 0, 0)),
        ],
        out_specs=pl.BlockSpec((QB, qw), lambda b, n: (b * nq + n, 0)),
        scratch_shapes=[
            pltpu.VMEM((nq, QB, A_LATENT), BF16),
            pltpu.VMEM((nq, A_LATENT + ONES_ROWS, QB), BF16),
            pltpu.VMEM((nq, QB, QB), jnp.int32),
            pltpu.VMEM((1, A_HEADS * QB), F32),
            pltpu.VMEM((1, A_HEADS * QB), F32),
            pltpu.VMEM((A_LATENT, A_HEADS * QB), F32),
        ],
        compiler_params=_cparams("parallel", "arbitrary"),
        name="dsa_attention",
    )(proj, proj, proj, aux, aux, gain.reshape(1, A_LATENT).astype(F32),
      w_uk.astype(BF16), w_uv.astype(BF16), _tile_bias_masked_t(tbl, 2))


def _compress_kernel(x_ref, pe_ref, w1_ref, w2_ref, o_ref):
    half = CMP_STRIDE * HEAD_DIM
    x = x_ref[0]
    w1 = w1_ref[...]
    u = _dot(x, w1[:half])
    v = _dot(x, w1[half:])
    pre = u + pltpu.roll(v, v.shape[0] - 1, 0) + _dot(pe_ref[...], w1)[:1]
    h = 0.5 * pre * (1.0 + jnp.tanh(math.sqrt(2.0 / math.pi) * (pre + 0.044715 * pre * pre * pre)))
    o_ref[0] = _dot(h.astype(BF16), w2_ref[...]).astype(o_ref.dtype)


def compress_blocks(kv, pe, w1, w2, batch, seq):
    G = C_KV_HEADS
    nch = seq // CMP_STRIDE
    half = CMP_STRIDE * HEAD_DIM
    x = kv.reshape(batch, nch, CMP_STRIDE, G, HEAD_DIM).transpose(0, 3, 1, 2, 4)
    x = x.reshape(batch * G, nch, half)
    pe_flat = jnp.broadcast_to(pe.reshape(1, CMP_BLOCK * HEAD_DIM), (8, CMP_BLOCK * HEAD_DIM))
    return pl.pallas_call(
        _compress_kernel,
        out_shape=jax.ShapeDtypeStruct((batch * G, nch, HEAD_DIM), BF16),
        grid=(batch * G,),
        in_specs=[pl.BlockSpec((1, nch, half), lambda i: (i, 0, 0)),
                  pl.BlockSpec((8, 2 * half), lambda i: (0, 0)),
                  pl.BlockSpec((2 * half, CMP_HIDDEN), lambda i: (0, 0)),
                  pl.BlockSpec((CMP_HIDDEN, HEAD_DIM), lambda i: (0, 0))],
        out_specs=pl.BlockSpec((1, nch, HEAD_DIM), lambda i: (i, 0, 0)),
        compiler_params=_cparams("parallel"),
        name="compress_blocks",
    )(x, pe_flat.astype(BF16), w1.astype(BF16), w2.astype(BF16))


def _cmp_kernel(q_ref, kc_ref, vc_ref, bias_ref, gate_ref, ovt_ref, o_ref, pick_ref, *, n_cmp, n_slc, n_sel):
    n = pl.program_id(1)
    G = C_KV_HEADS
    hpg = C_HEADS // G
    nch = kc_ref.shape[1]
    c = lax.broadcasted_iota(jnp.int32, (nch, QB), 0)
    t = n * QB + lax.broadcasted_iota(jnp.int32, (nch, QB), 1)
    valid = (t - (c * CMP_STRIDE + CMP_BLOCK - 1) >= 0) & (c < n_cmp)
    jj = lax.broadcasted_iota(jnp.int32, (n_slc, QB), 0)
    ts = n * QB + lax.broadcasted_iota(jnp.int32, (n_slc, QB), 1)
    blk = ts // SLC_BLOCK
    bonus = FORCE_BONUS * jnp.where((jj == 0) | (jj == blk) | (jj == blk - 1), 1.0, 0.0)
    admissible = jj * SLC_BLOCK <= ts
    q = q_ref[...]
    gate_t = gate_ref[...].T
    for g in range(G):
        s = _dot_nt(kc_ref[g], _stack_heads(q, g * hpg, hpg, HEAD_DIM ** -0.5))
        vct = vc_ref[g].astype(F32).T.astype(BF16)
        psum = jnp.zeros((nch, QB), F32)
        probs = []
        for hi in range(hpg):
            sl = slice(hi * QB, (hi + 1) * QB)
            sh = jnp.where(valid, s[:, sl] + bias_ref[g * hpg + hi], NEG_INF)
            e = jnp.where(valid, jnp.exp(sh - jnp.max(sh, axis=0, keepdims=True)), 0.0)
            den = jnp.sum(e, axis=0, keepdims=True)
            p = e * (1.0 / jnp.where(den > 0.0, den, 1.0))
            psum = psum + p
            probs.append(p.astype(BF16))
        pv = _dot(vct, jnp.concatenate(probs, axis=1))
        for hi in range(hpg):
            h = g * hpg + hi
            o = pv[:, hi * QB:(hi + 1) * QB] * (1.0 / (1.0 + jnp.exp(-gate_t[h:h + 1, :])))
            o_ref[:, h * HEAD_DIM:(h + 1) * HEAD_DIM] = o.T.astype(o_ref.dtype)
        p_hi = psum.astype(BF16)
        p_lo = (psum - p_hi.astype(F32)).astype(BF16)
        p_slc = _dot(ovt_ref[...], p_hi) + _dot(ovt_ref[...], p_lo)
        score = jnp.where(admissible, p_slc + bonus, NEG_INF)
        rank = jnp.zeros((n_slc, QB), F32)
        for i in range(n_slc):
            si = score[i:i + 1, :]
            rank = rank + jnp.where(si > score, 1.0, 0.0) + jnp.where((si == score) & (i < jj), 1.0, 0.0)
        pick_ref[0, g] = jnp.where(rank < n_sel, 0.0, -MASK_BIG).astype(pick_ref.dtype)


def cmp_attention(proj, kc, vc, gates, tbl, batch, seq):
    nq = seq // QB
    G = C_KV_HEADS
    nch = seq // CMP_STRIDE
    n_cmp = (seq - CMP_BLOCK) // CMP_STRIDE + 1
    n_slc = seq // SLC_BLOCK
    n_sel = min(SLC_COUNT, n_slc)
    qw = C_HEADS * HEAD_DIM
    cmp_end = np.arange(nch) * CMP_STRIDE + CMP_BLOCK - 1
    dist_c = jnp.asarray(np.arange(seq)[None, :] - cmp_end[:, None])
    bias_c = _bias_lookup(tbl, dist_c).transpose(2, 0, 1)
    cs = np.arange(nch)[None, :] * CMP_STRIDE
    ss = np.arange(n_slc)[:, None] * SLC_BLOCK
    overlap_t = ((cs < ss + SLC_BLOCK) & (cs + CMP_BLOCK > ss) & (np.arange(nch)[None, :] < n_cmp))
    return pl.pallas_call(
        functools.partial(_cmp_kernel, n_cmp=n_cmp, n_slc=n_slc, n_sel=n_sel),
        out_shape=(jax.ShapeDtypeStruct((batch * seq, qw), BF16),
                   jax.ShapeDtypeStruct((batch, G, n_slc, seq), BF16)),
        grid=(batch, nq),
        in_specs=[
            pl.BlockSpec((QB, qw), lambda b, n: (b * nq + n, 0)),
            pl.BlockSpec((G, nch, HEAD_DIM), lambda b, n: (b, 0, 0)),
            pl.BlockSpec((G, nch, HEAD_DIM), lambda b, n: (b, 0, 0)),
            pl.BlockSpec((C_HEADS, nch, QB), lambda b, n: (0, 0, n)),
            pl.BlockSpec((QB, LANES), lambda b, n: (b * nq + n, 0)),
            pl.BlockSpec((n_slc, nch), lambda b, n: (0, 0)),
        ],
        out_specs=(pl.BlockSpec((QB, qw), lambda b, n: (b * nq + n, 0)),
                   pl.BlockSpec((1, G, n_slc, QB), lambda b, n: (b, 0, 0, n))),
        compiler_params=_cparams("parallel", "parallel"),
        name="cmp_attention",
    )(proj, kc, vc, bias_c, gates, jnp.asarray(overlap_t.astype(np.float32)).astype(BF16))


def _slc_kernel(q_ref, k_ref, v_ref, pick_ref, bias_ref, gate_ref, o_ref, vt_s, m_s, l_s, acc_s, *, n_slc):
    n = pl.program_id(1)
    G = C_KV_HEADS
    hpg = C_HEADS // G
    nq = vt_s.shape[0]

    @pl.when(n == 0)
    def _():
        for kt in range(nq):
            vt_s[kt] = _value_tile_t(v_ref[kt * QB:(kt + 1) * QB, :], G)

    q = q_ref[...].astype(F32)
    rhs = []
    for g in range(G):
        qt = jnp.concatenate(
            [(q[:, (g * hpg + hi) * HEAD_DIM:(g * hpg + hi + 1) * HEAD_DIM].T * (HEAD_DIM ** -0.5 * LOG2E)).astype(BF16)
             for hi in range(hpg)], axis=1)
        rhs.append(jnp.concatenate([qt, jnp.concatenate([pick_ref[0, g]] * hpg, axis=1)], axis=0))
    ej = lax.broadcasted_iota(jnp.int32, (QB, n_slc), 0)
    eb = lax.broadcasted_iota(jnp.int32, (QB, n_slc), 1)
    m_s[...] = jnp.full(m_s.shape, NEG_INF, F32)
    l_s[...] = jnp.zeros(l_s.shape, F32)
    acc_s[...] = jnp.zeros(acc_s.shape, F32)
    vrows = HEAD_DIM + ONES_ROWS

    def logits(kt):
        k_t = k_ref[pl.ds(pl.multiple_of(kt * QB, QB), QB), :]
        expand = jnp.where(eb == (kt * QB + ej) // SLC_BLOCK, 1.0, 0.0).astype(BF16)
        return [_dot(jnp.concatenate([k_t[:, g * HEAD_DIM:(g + 1) * HEAD_DIM], expand], axis=1), rhs[g])
                for g in range(G)]

    def update(kt, s, d):
        vt_t = vt_s[kt]
        for g in range(G):
            _flash_tile(s[g], None if d is None else (lambda hi, g=g: bias_ref[g * hpg + hi, d]),
                        m_s.at[g], l_s.at[g], acc_s.at[g], vt_t[g * vrows:(g + 1) * vrows, :], hpg)

    _for_tiles_pairwise(jnp.maximum(n - 1, 0), logits, lambda kt, s: update(kt, s, None))

    @pl.when(n >= 1)
    def _():
        s_prev = logits(n - 1)
        s_diag = logits(n)
        update(n - 1, s_prev, 1)
        update(n, s_diag, 0)

    @pl.when(n == 0)
    def _():
        update(n, logits(n), 0)

    gate_t = gate_ref[...].T
    for g in range(G):
        _store_heads_t(o_ref, acc_s.at[g], l_s.at[g], g * hpg, hpg, gate_t)


def slc_attention(proj, pick, gates, tbl, batch, seq):
    nq = seq // QB
    G = C_KV_HEADS
    n_slc = seq // SLC_BLOCK
    qw = C_HEADS * HEAD_DIM
    return pl.pallas_call(
        functools.partial(_slc_kernel, n_slc=n_slc),
        out_shape=jax.ShapeDtypeStruct((batch * seq, qw), BF16),
        grid=(batch, nq),
        in_specs=[
            pl.BlockSpec((QB, qw), lambda b, n: (b * nq + n, 0)),
            pl.BlockSpec((seq, LANES), lambda b, n: (b, 10)),
            pl.BlockSpec((seq, LANES), lambda b, n: (b, 11)),
            pl.BlockSpec((1, G, n_slc, QB), lambda b, n: (b, 0, 0, n)),
            pl.BlockSpec((C_HEADS, 2, QB, QB), lambda b, n: (0, 0, 0, 0)),
            pl.BlockSpec((QB, LANES), lambda b, n: (b * nq + n, 1)),
        ],
        out_specs=pl.BlockSpec((QB, qw), lambda b, n: (b * nq + n, 0)),
        scratch_shapes=[pltpu.VMEM((nq, G * (HEAD_DIM + ONES_ROWS), QB), BF16),
                        pltpu.VMEM((G, 1, C_HEADS // G * QB), F32),
                        pltpu.VMEM((G, 1, C_HEADS // G * QB), F32),
                        pltpu.VMEM((G, HEAD_DIM, C_HEADS // G * QB), F32)],
        compiler_params=_cparams("parallel", "arbitrary"),
        name="slc_attention",
    )(proj, proj, proj, pick, _tile_bias_masked_t(tbl, 2), gates)


def _xattn_kernel(h_ref, wq_ref, k_ref, v_ref, wo_ref, g_ref, b_ref, o_ref):
    x = h_ref[...]
    q = _dot(x.astype(BF16), wq_ref[...]).astype(BF16)
    k = k_ref[...]
    v = v_ref[...]
    outs = []
    for hd in range(X_HEADS):
        sl = slice(hd * X_HEAD_DIM, (hd + 1) * X_HEAD_DIM)
        lg = _dot_nt(q[:, sl], k[:, sl]) * X_HEAD_DIM ** -0.5
        e = jnp.exp(lg - jnp.max(lg, axis=-1, keepdims=True))
        den = jnp.sum(e, axis=-1, keepdims=True)
        outs.append((_dot(e.astype(BF16), v[:, sl]) / den).astype(BF16))
    o = jnp.concatenate(outs, axis=-1)
    y = ALPHA * x + _dot(o, wo_ref[...])
    o_ref[...] = _layer_norm_rows(y, g_ref[...], b_ref[...])


def cross_attention_ln(h, mem, w_q, w_k, w_v, w_o, g, b, batch, seq, tm=256):
    M = mem.shape[0] // batch
    XW = X_HEADS * X_HEAD_DIM
    D = D_MODEL
    kv = matmul(mem, jnp.concatenate([w_k, w_v], axis=1).astype(BF16), BF16, tm=512)
    nt = seq // tm
    fixed = lambda bb, i: (0, 0)
    return pl.pallas_call(
        _xattn_kernel,
        out_shape=jax.ShapeDtypeStruct((batch * seq, D), F32),
        grid=(batch, nt),
        in_specs=[pl.BlockSpec((tm, D), lambda bb, i: (bb * nt + i, 0)),
                  pl.BlockSpec((D, XW), fixed),
                  pl.BlockSpec((M, XW), lambda bb, i: (bb, 0)),
                  pl.BlockSpec((M, XW), lambda bb, i: (bb, 1)),
                  pl.BlockSpec((XW, D), fixed),
                  pl.BlockSpec((1, D), fixed), pl.BlockSpec((1, D), fixed)],
        out_specs=pl.BlockSpec((tm, D), lambda bb, i: (bb * nt + i, 0)),
        compiler_params=_cparams("parallel", "parallel"),
        name="cross_attention_ln",
    )(h, w_q.astype(BF16), kv, kv, w_o.astype(BF16), g.reshape(1, D), b.reshape(1, D))


def _silu(a):
    return a * (1.0 / (1.0 + jnp.exp(-a)))


def _swiglu_kernel(x_ref, wg_ref, wu_ref, wd_ref, g_ref, b_ref, o_ref):
    x = x_ref[...]
    xb = x.astype(BF16)
    hmid = (_silu(_dot(xb, wg_ref[...])) * _dot(xb, wu_ref[...])).astype(BF16)
    y = ALPHA * x + _dot(hmid, wd_ref[...])
    o_ref[...] = _layer_norm_rows(y, g_ref[...], b_ref[...])


def swiglu_ln(h, w_gate, w_up, w_down, g, b, tm=256):
    T, D = h.shape
    F = w_gate.shape[1]
    fixed = lambda i: (0, 0)
    once = pl.Buffered(1)
    return pl.pallas_call(
        _swiglu_kernel,
        out_shape=jax.ShapeDtypeStruct((T, D), F32),
        grid=(T // tm,),
        in_specs=[pl.BlockSpec((tm, D), lambda i: (i, 0)),
                  pl.BlockSpec((D, F), fixed, pipeline_mode=once),
                  pl.BlockSpec((D, F), fixed, pipeline_mode=once),
                  pl.BlockSpec((F, D), fixed, pipeline_mode=once),
                  pl.BlockSpec((1, D), fixed), pl.BlockSpec((1, D), fixed)],
        out_specs=pl.BlockSpec((tm, D), lambda i: (i, 0)),
        compiler_params=_cparams("parallel"),
        name="swiglu_ln",
    )(h, cast_bf16(w_gate), cast_bf16(w_up), cast_bf16(w_down), g.reshape(1, D), b.reshape(1, D))


def _router_kernel(x_ref, w_ref, idx_ref, wgt_ref, xlo_ref, xhi_ref):
    x = x_ref[...]
    w = w_ref[...]
    x_hi = x.astype(BF16)
    x_lo = (x - x_hi.astype(F32)).astype(BF16)
    w_hi = w.astype(BF16)
    w_lo = (w - w_hi.astype(F32)).astype(BF16)
    lg = _dot(x_hi, w_hi) + (_dot(x_hi, w_lo) + _dot(x_lo, w_hi))
    e_iota = lax.broadcasted_iota(jnp.int32, lg.shape, 1).astype(F32)
    m1 = jnp.max(lg, axis=-1, keepdims=True)
    i1 = jnp.min(jnp.where(lg == m1, e_iota, float(N_EXPERTS)), axis=-1, keepdims=True)
    first = e_iota == i1
    rest = jnp.where(first, -jnp.inf, lg)
    m2 = jnp.max(rest, axis=-1, keepdims=True)
    i2 = jnp.min(jnp.where(rest == m2, e_iota, float(N_EXPERTS)), axis=-1, keepdims=True)
    w2 = jnp.exp(m2 - m1)
    den = 1.0 + w2
    k_iota = lax.broadcasted_iota(jnp.int32, idx_ref.shape, 1)
    idx_ref[...] = jnp.where(k_iota == 0, i1, i2).astype(jnp.int32)
    wgt_ref[...] = jnp.where(k_iota == 0, 1.0 / den, w2 / den)
    half = x.shape[1] // 2
    xlo_ref[...] = _pack_bf16_pairs(x[:, :half])
    xhi_ref[...] = _pack_bf16_pairs(x[:, half:])


def _pack_bf16_pairs(x):
    w = x.shape[1] // 2
    lo = lax.bitcast_convert_type(x[:, :w].astype(BF16).astype(F32), jnp.uint32)
    hi = lax.bitcast_convert_type(x[:, w:].astype(BF16).astype(F32), jnp.uint32)
    return (lo >> 16) | (hi & jnp.uint32(0xFFFF0000))


def _unpack_bf16_pairs(words):
    lo = lax.bitcast_convert_type(words << 16, F32).astype(BF16)
    hi = lax.bitcast_convert_type(words & jnp.uint32(0xFFFF0000), F32).astype(BF16)
    return jnp.concatenate([lo, hi], axis=1)


def moe_route(h, w_router, tm=1024):
    T, D = h.shape
    E = w_router.shape[1]
    row = lambda i: (i, 0)
    return pl.pallas_call(
        _router_kernel,
        out_shape=(jax.ShapeDtypeStruct((T, TOP_K), jnp.int32), jax.ShapeDtypeStruct((T, TOP_K), F32),
                   jax.ShapeDtypeStruct((T, D // 4), jnp.uint32), jax.ShapeDtypeStruct((T, D // 4), jnp.uint32)),
        grid=(T // tm,),
        in_specs=[pl.BlockSpec((tm, D), row), pl.BlockSpec((D, E), lambda i: (0, 0))],
        out_specs=(pl.BlockSpec((tm, TOP_K), row), pl.BlockSpec((tm, TOP_K), row),
                   pl.BlockSpec((tm, D // 4), row), pl.BlockSpec((tm, D // 4), row)),
        compiler_params=_cparams("parallel"),
        name="moe_route",
    )(h, w_router)


def sc_gather_rows(xs, idx, window):
    n_idx = idx.shape[0]
    n_arr = len(xs)
    mesh = plsc.VectorSubcoreMesh(core_axis_name="core", subcore_axis_name="subcore")

    @pl.kernel(out_type=[jax.ShapeDtypeStruct((n_idx, x.shape[1]), x.dtype) for x in xs], mesh=mesh,
               scratch_types=[], name="sc_gather_rows")
    def gather(*refs):
        x_hbms, i_hbm, o_hbms = refs[:n_arr], refs[n_arr], refs[n_arr + 1:]
        for x_hbm, o_hbm in zip(x_hbms, o_hbms):
            def body(i_vmem, o_vmem, x_hbm=x_hbm):
                pltpu.sync_copy(x_hbm.at[i_vmem.at[0]], o_vmem)

            pltpu.emit_pipeline(
                body,
                grid=(n_idx // window,),
                in_specs=[pl.BlockSpec((1, window), lambda i: (0, i))],
                out_specs=[pl.BlockSpec((window, x_hbm.shape[1]), lambda i: (i, 0))],
                core_axis_name=("core", "subcore"),
                dimension_semantics=(pltpu.PARALLEL,),
            )(i_hbm, o_hbm)

    return gather(*xs, idx.reshape(1, n_idx))


def _moe_plan(top_idx, tm):
    T = top_idx.shape[0]
    n_asg = T * TOP_K
    n_tiles = n_asg // tm + N_EXPERTS
    flat_e = top_idx.reshape(n_asg)
    onehot = (flat_e[:, None] == jnp.arange(N_EXPERTS, dtype=jnp.int32)[None, :]).astype(jnp.int32)
    incl = jnp.cumsum(onehot, axis=0)
    counts = incl[-1]
    rank = jnp.sum((incl - onehot) * onehot, axis=1)
    tiles_e = (counts + tm - 1) // tm
    tile_end = jnp.cumsum(tiles_e)
    row_start_p = (tile_end - tiles_e) * tm
    row_start = jnp.cumsum(counts) - counts
    pos = (jnp.sum(onehot * row_start_p[None, :], axis=1) + rank).reshape(T, TOP_K).T.reshape(n_asg)
    tile_ids = jnp.arange(n_tiles, dtype=jnp.int32)
    tile_expert = jnp.minimum(jnp.searchsorted(tile_end, tile_ids, side="right"), N_EXPERTS - 1).astype(jnp.int32)
    n_rows = n_tiles * tm
    sorted_tok = jnp.argsort(flat_e, stable=True).astype(jnp.int32) // TOP_K
    window_src = jnp.concatenate([jnp.zeros((n_rows,), jnp.int32), sorted_tok, jnp.zeros((n_rows,), jnp.int32)])
    r = jnp.arange(n_rows, dtype=jnp.int32)
    src_tok = jnp.zeros((n_rows,), jnp.int32)
    for e in range(N_EXPERTS):
        shifted = lax.dynamic_slice(window_src, (n_rows - (row_start_p[e] - row_start[e]),), (n_rows,))
        in_run = (r >= row_start_p[e]) & (r < row_start_p[e] + counts[e])
        src_tok = jnp.where(in_run, shifted, src_tok)
    return src_tok, pos.astype(jnp.int32), tile_expert, tile_end[-1:].astype(jnp.int32)


def _moe_ffn_kernel(te_ref, nu_ref, xlo_ref, xhi_ref, wg_ref, wu_ref, wd_ref, olo_ref, ohi_ref, acc_s):
    i = pl.program_id(0)
    c = pl.program_id(1)
    last = pl.num_programs(1) - 1
    used = i < nu_ref[0]
    half = acc_s.shape[1] // 2

    @pl.when(used & (c == 0))
    def _():
        acc_s[...] = jnp.zeros(acc_s.shape, F32)

    @pl.when(used)
    def _():
        x = jnp.concatenate([_unpack_bf16_pairs(xlo_ref[...]), _unpack_bf16_pairs(xhi_ref[...])], axis=1)
        hmid = _silu(_dot(x, wg_ref[0])) * _dot(x, wu_ref[0])
        acc_s[...] += _dot(hmid.astype(BF16), wd_ref[0])

    @pl.when(used & (c == last))
    def _():
        olo_ref[...] = _pack_bf16_pairs(acc_s[:, :half])
        ohi_ref[...] = _pack_bf16_pairs(acc_s[:, half:])

    @pl.when(jnp.logical_not(used) & (c == last))
    def _():
        olo_ref[...] = jnp.zeros(olo_ref.shape, olo_ref.dtype)
        ohi_ref[...] = jnp.zeros(ohi_ref.shape, ohi_ref.dtype)


def moe_expert_ffn(xs_lo, xs_hi, tile_expert, n_used, w_gate, w_up, w_down, tm, fc=1792):
    R, half = xs_lo.shape
    D = 4 * half
    E, _, F = w_gate.shape
    rows = lambda i, c, te, nu: (i, 0)
    return pl.pallas_call(
        _moe_ffn_kernel,
        out_shape=(jax.ShapeDtypeStruct((R, half), jnp.uint32), jax.ShapeDtypeStruct((R, half), jnp.uint32)),
        grid_spec=pltpu.PrefetchScalarGridSpec(
            num_scalar_prefetch=2,
            grid=(R // tm, F // fc),
            in_specs=[pl.BlockSpec((tm, half), rows), pl.BlockSpec((tm, half), rows),
                      pl.BlockSpec((1, D, fc), lambda i, c, te, nu: (te[i], 0, c)),
                      pl.BlockSpec((1, D, fc), lambda i, c, te, nu: (te[i], 0, c)),
                      pl.BlockSpec((1, fc, D), lambda i, c, te, nu: (te[i], c, 0))],
            out_specs=(pl.BlockSpec((tm, half), rows), pl.BlockSpec((tm, half), rows)),
            scratch_shapes=[pltpu.VMEM((tm, D), F32)]),
        compiler_params=_cparams("parallel", "arbitrary"),
        name="moe_expert_ffn",
    )(tile_expert, n_used, xs_lo, xs_hi, cast_bf16(w_gate), cast_bf16(w_up), cast_bf16(w_down))


def _moe_combine_kernel(h_ref, lo0_ref, hi0_ref, lo1_ref, hi1_ref, w_ref, g_ref, b_ref, o_ref):
    w = w_ref[...]
    y0 = jnp.concatenate([_unpack_bf16_pairs(lo0_ref[...]), _unpack_bf16_pairs(hi0_ref[...])], axis=1).astype(F32)
    y1 = jnp.concatenate([_unpack_bf16_pairs(lo1_ref[...]), _unpack_bf16_pairs(hi1_ref[...])], axis=1).astype(F32)
    ff = w[:, 0:1] * y0 + w[:, 1:2] * y1
    o_ref[...] = _layer_norm_rows(ALPHA * h_ref[...] + ff, g_ref[...], b_ref[...])


def moe_combine_ln(h, y_lo, y_hi, top_w, g, b, tm=512):
    T, D = h.shape
    half = D // 4
    nt = T // tm
    row = lambda i: (i, 0)
    second = lambda i: (nt + i, 0)
    fixed = lambda i: (0, 0)
    return pl.pallas_call(
        _moe_combine_kernel,
        out_shape=jax.ShapeDtypeStruct((T, D), F32),
        grid=(nt,),
        in_specs=[pl.BlockSpec((tm, D), row),
                  pl.BlockSpec((tm, half), row), pl.BlockSpec((tm, half), row),
                  pl.BlockSpec((tm, half), second), pl.BlockSpec((tm, half), second),
                  pl.BlockSpec((tm, TOP_K), row),
                  pl.BlockSpec((1, D), fixed), pl.BlockSpec((1, D), fixed)],
        out_specs=pl.BlockSpec((tm, D), row),
        compiler_params=_cparams("parallel"),
        name="moe_combine_ln",
    )(h, y_lo, y_hi, y_lo, y_hi, top_w, g.reshape(1, D), b.reshape(1, D))


def moe_ln(h, w_router, w_gate, w_up, w_down, g, b, tm=512):
    top_idx, top_w, hb_lo, hb_hi = moe_route(h, w_router)
    src_tok, pos, tile_expert, n_used = _moe_plan(top_idx, tm)
    xs_lo, xs_hi = sc_gather_rows([hb_lo, hb_hi], src_tok, SC_GATHER_WINDOW)
    ys_lo, ys_hi = moe_expert_ffn(xs_lo, xs_hi, tile_expert, n_used, w_gate, w_up, w_down, tm)
    y_lo, y_hi = sc_gather_rows([ys_lo, ys_hi], pos, SC_GATHER_WINDOW)
    return moe_combine_ln(h, y_lo, y_hi, top_w, g, b)


def _pad_cols(w, width):
    return jnp.pad(w, ((0, 0), (0, width - w.shape[1])))


def even_layer_mixer(h, w_in, ckv_gain, w_uk, w_uv, sinks, w_out, rel_table, ln_g, ln_b, batch, seq):
    cuts = np.cumsum((A_HEADS * HEAD_DIM, A_LATENT, IDX_HEADS * IDX_DIM, IDX_DIM, IDX_HEADS,
                      B_HEADS * HEAD_DIM, B_KV_HEADS * HEAD_DIM))
    w_qa, w_ckv, w_qi, w_ki, w_wi, w_qb, w_kb, w_vb = jnp.split(w_in, [int(c) for c in cuts], axis=1)
    w_main = jnp.concatenate([w_qa, w_qi, w_qb, _pad_cols(w_ki, LANES), w_kb, w_vb], axis=1).astype(BF16)
    w_aux = jnp.concatenate([w_ckv, _pad_cols(w_wi, LANES)], axis=1).astype(BF16)
    proj = matmul(h, w_main, BF16)
    aux = matmul(h, w_aux, F32)
    o_a = dsa_attention(proj, aux, ckv_gain, w_uk, w_uv, rel_table[:, :A_HEADS], batch, seq)
    o_b = banded_attention(proj, 2, 13, 14, B_HEADS, B_KV_HEADS,
                           rel_table[:, A_HEADS:A_HEADS + B_HEADS], B_WINDOW,
                           batch, seq, sinks=sinks)
    o = jnp.concatenate([o_a, o_b], axis=1)
    return matmul_residual_ln([o], w_out.astype(BF16), h, ln_g, ln_b)


def odd_layer_mixer(h, w_in, pe_k, w1_k, w2_k, pe_v, w1_v, w2_v, w_out, rel_table, ln_g, ln_b, batch, seq):
    kvw = C_KV_HEADS * HEAD_DIM
    qw = C_HEADS * HEAD_DIM
    w_main = w_in[:, :qw + 6 * kvw].astype(BF16)
    w_gl = w_in[:, qw + 6 * kvw:].reshape(D_MODEL, C_HEADS, 3)
    w_gate = jnp.concatenate([_pad_cols(w_gl[:, :, r], LANES) for r in range(3)], axis=1).astype(BF16)
    proj = matmul(h, w_main, BF16)
    gates = matmul(h, w_gate, F32)
    tbl = rel_table[:, :C_HEADS]
    kc = compress_blocks(proj[:, qw:qw + kvw], pe_k, w1_k, w2_k, batch, seq)
    vc = compress_blocks(proj[:, qw + kvw:qw + 2 * kvw], pe_v, w1_v, w2_v, batch, seq)
    o_cmp, pick = cmp_attention(proj, kc, vc, gates, tbl, batch, seq)
    o_slc = slc_attention(proj, pick, gates, tbl, batch, seq)
    o_win = banded_attention(proj, 0, 12, 13, C_HEADS, C_KV_HEADS, tbl, C_WINDOW,
                             batch, seq, gates=gates, gate_col=2)
    return matmul_residual_ln([o_cmp, o_slc, o_win], w_out.astype(BF16), h, ln_g, ln_b)


def kernel(x, mem, rel_table, ev_w_in, ev_ckv_gain, ev_w_uk, ev_w_uv, ev_sinks, ev_w_out, od_w_in, od_pe_k, od_w1_k, od_w2_k, od_pe_v, od_w1_v, od_w2_v, od_w_out, xa_w_q, xa_w_k, xa_w_v, xa_w_o, ff_w_gate, ff_w_up, ff_w_down, moe_w_router, moe_w_gate, moe_w_up, moe_w_down, ln_g, ln_b):
    batch, seq, D = x.shape
    h = x.reshape(batch * seq, D)
    mem2 = mem.reshape(-1, D)
    for i in range(DEPTH):
        j = i // 2
        if i % 2 == 0:
            h = even_layer_mixer(h, ev_w_in[j], ev_ckv_gain[j], ev_w_uk[j], ev_w_uv[j], ev_sinks[j],
                                 ev_w_out[j], rel_table, ln_g[i, 0], ln_b[i, 0], batch, seq)
        else:
            h = odd_layer_mixer(h, od_w_in[j], od_pe_k[j], od_w1_k[j], od_w2_k[j], od_pe_v[j],
                                od_w1_v[j], od_w2_v[j], od_w_out[j], rel_table, ln_g[i, 0], ln_b[i, 0],
                                batch, seq)
        h = cross_attention_ln(h, mem2, xa_w_q[i], xa_w_k[i], xa_w_v[i], xa_w_o[i],
                               ln_g[i, 1], ln_b[i, 1], batch, seq)
        if i % 2 == 0:
            h = swiglu_ln(h, ff_w_gate[j], ff_w_up[j], ff_w_down[j], ln_g[i, 2], ln_b[i, 2])
        else:
            h = moe_ln(h, moe_w_router[j], moe_w_gate[j], moe_w_up[j], moe_w_down[j], ln_g[i, 2], ln_b[i, 2])
    return h.reshape(batch, seq, D)
```

```python
import functools
import math

import numpy as np
import jax
import jax.numpy as jnp
from jax import lax
from jax.experimental import pallas as pl
from jax.experimental.pallas import tpu as pltpu
from jax.experimental.pallas import tpu_sc as plsc

D_MODEL = 1024
DEPTH = 2
HEAD_DIM = 64
N_SLOTS = D_MODEL // HEAD_DIM
A_HEADS = 8
A_LATENT = 128
IDX_HEADS = 8
IDX_DIM = 64
A_TOPK_MAX = 256
B_HEADS = 8
B_KV_HEADS = 2
B_WINDOW = 128
C_HEADS = 16
C_KV_HEADS = 2
CMP_BLOCK = 32
CMP_STRIDE = 16
CMP_HIDDEN = 128
SLC_BLOCK = 64
SLC_COUNT = 8
C_WINDOW = 512
X_HEADS = 4
X_HEAD_DIM = 128
D_FF = 2816
N_EXPERTS = 8
TOP_K = 2
D_FF_EXPERT = 3584
REL_BUCKETS = 32
REL_MAX_DIST = 128
LN_EPS = 1e-5
NEG_INF = -1e30
FORCE_BONUS = 1e6
ALPHA = (2 * DEPTH) ** 0.25
MASK_BIG = 2.0 ** 100
LOG2E = math.log2(math.e)
ONES_ROWS = 16

LANES = 128
QB = 128
VMEM_LIMIT_BYTES = 56 * 1024 * 1024
SC_GATHER_WINDOW = 128

F32 = jnp.float32
BF16 = jnp.bfloat16


def _cparams(*sem):
    return pltpu.CompilerParams(dimension_semantics=sem, vmem_limit_bytes=VMEM_LIMIT_BYTES)


def _dot(a, b):
    return jnp.dot(a, b, preferred_element_type=F32)


def _dot_nt(a, b):
    return lax.dot_general(a, b, (((1,), (1,)), ((), ())), preferred_element_type=F32)


def _layer_norm_rows(y, g, b):
    mu = jnp.mean(y, axis=-1, keepdims=True)
    yc = y - mu
    var = jnp.mean(yc * yc, axis=-1, keepdims=True)
    return yc * lax.rsqrt(var + LN_EPS) * g + b


def _mm_kernel(x_ref, w_ref, o_ref):
    o_ref[...] = _dot(x_ref[...].astype(BF16), w_ref[...]).astype(o_ref.dtype)


def matmul(x, w, out_dtype, tm=512):
    M, K = x.shape
    N = w.shape[1]
    tm = min(tm, M)
    return pl.pallas_call(
        _mm_kernel,
        out_shape=jax.ShapeDtypeStruct((M, N), out_dtype),
        grid=(M // tm,),
        in_specs=[pl.BlockSpec((tm, K), lambda i: (i, 0)),
                  pl.BlockSpec((K, N), lambda i: (0, 0))],
        out_specs=pl.BlockSpec((tm, N), lambda i: (i, 0)),
        compiler_params=_cparams("parallel"),
        name="matmul",
    )(x, w)


def _cast_kernel(*refs):
    o_ref = refs[-1]
    cw = refs[0].shape[1]
    for j, x_ref in enumerate(refs[:-1]):
        o_ref[:, j * cw:(j + 1) * cw] = x_ref[...].astype(o_ref.dtype)


def cast_bf16(w, block_bytes=8 * 1024 * 1024):
    shape = w.shape
    C = shape[-1]
    M = int(np.prod(shape[:-1]))
    tm = M
    while tm * C * 4 > block_bytes and tm % 32 == 0:
        tm //= 2
    n_slab = next(k for k in (4, 2, 1) if C % (k * LANES) == 0)
    cw = C // n_slab
    w2 = w.reshape(M, C)
    out = pl.pallas_call(
        _cast_kernel,
        out_shape=jax.ShapeDtypeStruct((M, C), BF16),
        grid=(M // tm,),
        in_specs=[pl.BlockSpec((tm, cw), lambda i, j=j: (i, j)) for j in range(n_slab)],
        out_specs=pl.BlockSpec((tm, C), lambda i: (i, 0)),
        compiler_params=_cparams("parallel"),
        name="cast_bf16",
    )(*([w2] * n_slab))
    return out.reshape(shape)


def _mm_res_ln_kernel(*refs, n_in):
    a_refs = refs[:n_in]
    w_ref, h_ref, g_ref, b_ref, o_ref = refs[n_in:]
    a = a_refs[0][...]
    if n_in > 1:
        a = a.astype(F32)
        for r in a_refs[1:]:
            a = a + r[...].astype(F32)
    y = ALPHA * h_ref[...] + _dot(a.astype(BF16), w_ref[...])
    o_ref[...] = _layer_norm_rows(y, g_ref[...], b_ref[...])


def matmul_residual_ln(a_list, w, h, g, b, tm=512):
    M, K = a_list[0].shape
    N = w.shape[1]
    n_in = len(a_list)
    row = lambda i: (i, 0)
    fixed = lambda i: (0, 0)
    return pl.pallas_call(
        functools.partial(_mm_res_ln_kernel, n_in=n_in),
        out_shape=jax.ShapeDtypeStruct((M, N), F32),
        grid=(M // tm,),
        in_specs=[pl.BlockSpec((tm, K), row)] * n_in + [
            pl.BlockSpec((K, N), fixed), pl.BlockSpec((tm, N), row),
            pl.BlockSpec((1, N), fixed), pl.BlockSpec((1, N), fixed)],
        out_specs=pl.BlockSpec((tm, N), row),
        compiler_params=_cparams("parallel"),
        name="matmul_residual_ln",
    )(*a_list, w, h, g.reshape(1, N), b.reshape(1, N))


def _rel_bucket(dist):
    n = jnp.maximum(dist, 0)
    max_exact = REL_BUCKETS // 2
    nf = jnp.maximum(n, 1).astype(F32)
    log_b = max_exact + (jnp.log(nf / max_exact) / math.log(REL_MAX_DIST / max_exact)
                         * (REL_BUCKETS - max_exact)).astype(jnp.int32)
    return jnp.where(n < max_exact, n, jnp.minimum(log_b, REL_BUCKETS - 1))


def _bias_lookup(tbl, dist):
    onehot = (_rel_bucket(dist)[..., None] == jnp.arange(REL_BUCKETS, dtype=jnp.int32)).astype(F32)
    return jnp.einsum("...r,rh->...h", onehot, tbl.astype(F32), precision=lax.Precision.HIGHEST)


def _tile_bias_masked_t(tbl, n_tiles, window=None):
    j = np.arange(QB)[:, None]
    i = np.arange(QB)[None, :]
    dist = np.stack([d * QB + i - j for d in range(n_tiles)])
    valid = dist >= 0
    if window is not None:
        valid &= dist < window
    rel = _bias_lookup(tbl, jnp.asarray(dist)) - _far_bias(tbl)
    return jnp.where(jnp.asarray(valid)[..., None], rel * LOG2E, NEG_INF).transpose(3, 0, 1, 2)


def _far_bias(tbl):
    return tbl.astype(F32)[REL_BUCKETS - 1]


def _flash_tile(s, add, m_ref, l_ref, acc_ref, vt_ones, hpg):
    dv = acc_ref.shape[0]
    scale, probs = [], []
    for hi in range(hpg):
        sl = slice(hi * QB, (hi + 1) * QB)
        sh = s[:, sl] if add is None else s[:, sl] + add(hi)
        m_old = m_ref[:, sl]
        m_new = jnp.maximum(m_old, jnp.max(sh, axis=0, keepdims=True))
        scale.append(jnp.exp2(m_old - m_new))
        probs.append(jnp.exp2(sh - m_new).astype(BF16))
        m_ref[:, sl] = m_new
    pv = _dot(vt_ones, jnp.concatenate(probs, axis=1))
    for hi in range(hpg):
        sl = slice(hi * QB, (hi + 1) * QB)
        acc_ref[:, sl] = scale[hi] * acc_ref[:, sl] + pv[:dv, sl]
        l_ref[:, sl] = scale[hi] * l_ref[:, sl] + pv[dv:dv + 1, sl]


def _value_tile_t(v_tile, groups):
    vt = v_tile.astype(F32).T.astype(BF16)
    ones = jnp.ones((ONES_ROWS, v_tile.shape[0]), BF16)
    parts = []
    for g in range(groups):
        parts += [vt[g * HEAD_DIM:(g + 1) * HEAD_DIM, :], ones]
    return jnp.concatenate(parts, axis=0)


def _for_tiles_pairwise(n_tiles, start, finish):
    def pair(i, carry):
        first = start(2 * i)
        second = start(2 * i + 1)
        finish(2 * i, first)
        finish(2 * i + 1, second)
        return carry

    lax.fori_loop(0, lax.shift_right_logical(n_tiles, 1), pair, 0)

    @pl.when((n_tiles & 1) == 1)
    def _():
        finish(n_tiles - 1, start(n_tiles - 1))


def _stack_heads(q, first, count, scale):
    return jnp.concatenate([q[:, (first + hi) * HEAD_DIM:(first + hi + 1) * HEAD_DIM] * scale
                            for hi in range(count)], axis=0)


def _store_heads_t(o_ref, acc_ref, l_ref, first, count, gate_t):
    for hi in range(count):
        h = first + hi
        sl = slice(hi * QB, (hi + 1) * QB)
        o = acc_ref[:, sl] / l_ref[:, sl]
        if gate_t is not None:
            o = o * (1.0 / (1.0 + jnp.exp(-gate_t[h:h + 1, :])))
        o_ref[:, h * HEAD_DIM:(h + 1) * HEAD_DIM] = o.T.astype(o_ref.dtype)


def _band_kernel(*refs, n_tiles, kvh, g, has_sinks, has_gate):
    it = iter(refs)
    q_ref, k_ref, v_ref, bias_ref = next(it), next(it), next(it), next(it)
    sink_ref = next(it) if has_sinks else None
    gate_ref = next(it) if has_gate else None
    o_ref, vt_s, m_s, l_s, acc_s = next(it), next(it), next(it), next(it), next(it)
    n = pl.program_id(1)
    nq = vt_s.shape[0]

    @pl.when(n == 0)
    def _():
        for kt in range(nq):
            vt_s[kt] = _value_tile_t(v_ref[kt * QB:(kt + 1) * QB, :], kvh)

    q = q_ref[...]
    qst = [_stack_heads(q, kv * g, g, HEAD_DIM ** -0.5 * LOG2E) for kv in range(kvh)]
    if has_sinks:
        m_s[...] = sink_ref[...] * LOG2E
        l_s[...] = jnp.ones(l_s.shape, F32)
    else:
        m_s[...] = jnp.full(m_s.shape, NEG_INF, F32)
        l_s[...] = jnp.zeros(l_s.shape, F32)
    acc_s[...] = jnp.zeros(acc_s.shape, F32)
    vrows = HEAD_DIM + ONES_ROWS

    def logits(d):
        k_t = k_ref[pl.ds(pl.multiple_of((n - d) * QB, QB), QB), :]
        return [_dot_nt(k_t[:, kv * HEAD_DIM:(kv + 1) * HEAD_DIM], qst[kv]) for kv in range(kvh)]

    def update(d, s):
        vt_t = vt_s[n - d]
        unbiased = 2 <= d <= n_tiles - 2
        for kv in range(kvh):
            _flash_tile(s[kv], None if unbiased else (lambda hi, kv=kv: bias_ref[kv * g + hi, d]),
                        m_s.at[kv], l_s.at[kv], acc_s.at[kv], vt_t[kv * vrows:(kv + 1) * vrows, :], g)

    @pl.when(n >= n_tiles - 1)
    def _():
        s_next = logits(0)
        for d in range(n_tiles):
            s_cur = s_next
            if d + 1 < n_tiles:
                s_next = logits(d + 1)
            update(d, s_cur)

    @pl.when(n < n_tiles - 1)
    def _():
        for d in range(n_tiles - 1):
            @pl.when(n >= d)
            def _(d=d):
                update(d, logits(d))

    gate_t = gate_ref[...].T if has_gate else None
    for kv in range(kvh):
        _store_heads_t(o_ref, acc_s.at[kv], l_s.at[kv], kv * g, g, gate_t)


def banded_attention(proj, q_col, k_col, v_col, n_heads, kvh, tbl, window, batch, seq,
                     sinks=None, gates=None, gate_col=0):
    g = n_heads // kvh
    kw = kvh * HEAD_DIM
    qw = n_heads * HEAD_DIM
    nq = seq // QB
    n_tiles = window // QB + 1
    args = [proj, proj, proj, _tile_bias_masked_t(tbl, n_tiles, window)]
    in_specs = [pl.BlockSpec((QB, qw), lambda b, n: (b * nq + n, q_col)),
                pl.BlockSpec((seq, kw), lambda b, n: (b, k_col)),
                pl.BlockSpec((seq, kw), lambda b, n: (b, v_col)),
                pl.BlockSpec((n_heads, n_tiles, QB, QB), lambda b, n: (0, 0, 0, 0))]
    if sinks is not None:
        args.append(jnp.repeat((sinks.astype(F32) - _far_bias(tbl)).reshape(kvh, 1, g), QB, axis=2))
        in_specs.append(pl.BlockSpec((kvh, 1, g * QB), lambda b, n: (0, 0, 0)))
    if gates is not None:
        args.append(gates)
        in_specs.append(pl.BlockSpec((QB, LANES), lambda b, n: (b * nq + n, gate_col)))
    return pl.pallas_call(
        functools.partial(_band_kernel, n_tiles=n_tiles, kvh=kvh, g=g,
                          has_sinks=sinks is not None, has_gate=gates is not None),
        out_shape=jax.ShapeDtypeStruct((batch * seq, qw), BF16),
        grid=(batch, nq),
        in_specs=in_specs,
        out_specs=pl.BlockSpec((QB, qw), lambda b, n: (b * nq + n, 0)),
        scratch_shapes=[pltpu.VMEM((nq, kvh * (HEAD_DIM + ONES_ROWS), QB), BF16),
                        pltpu.VMEM((kvh, 1, g * QB), F32), pltpu.VMEM((kvh, 1, g * QB), F32),
                        pltpu.VMEM((kvh, HEAD_DIM, g * QB), F32)],
        compiler_params=_cparams("parallel", "arbitrary"),
        name="banded_attention",
    )(*args)


def _sort_key(x):
    bits = lax.bitcast_convert_type(x, jnp.int32)
    return bits ^ ((bits >> 31) & jnp.int32(0x7FFFFFFF))


def _dsa_kernel(qa_ref, qi_ref, ki_ref, ckv_ref, wi_ref, gain_ref, wuk_ref, wuv_ref, bias_ref,
                o_ref, c_s, ct_s, key_s, m_s, l_s, acc_s, *, topk):
    n = pl.program_id(1)
    nt = n + 1
    nq = c_s.shape[0]

    @pl.when(n == 0)
    def _():
        for kt in range(nq):
            c = ckv_ref[kt * QB:(kt + 1) * QB, :]
            y = c * lax.rsqrt(jnp.mean(c * c, axis=-1, keepdims=True) + LN_EPS) * gain_ref[...]
            c_s[kt] = y.astype(BF16)
            ct_s[kt] = jnp.concatenate([y.T.astype(BF16), jnp.ones((ONES_ROWS, QB), BF16)], axis=0)

    qa = qa_ref[...]
    qi = qi_ref[...]
    qlat = jnp.concatenate(
        [(_dot(qa[:, h * HEAD_DIM:(h + 1) * HEAD_DIM], wuk_ref[h]) * (HEAD_DIM ** -0.5 * LOG2E)).astype(BF16)
         for h in range(A_HEADS)], axis=0)
    qidx = jnp.concatenate(
        [qi[:, h * IDX_DIM:(h + 1) * IDX_DIM] * IDX_DIM ** -0.5 for h in range(IDX_HEADS)], axis=0)
    wi_t = wi_ref[...].T
    wi_row = jnp.concatenate([wi_t[h:h + 1, :] for h in range(IDX_HEADS)], axis=1) * IDX_HEADS ** -0.5

    kpos0 = lax.broadcasted_iota(jnp.int32, (QB, QB), 0)
    qpos = n * QB + lax.broadcasted_iota(jnp.int32, (QB, QB), 1)

    def index_logits(kt):
        k_t = ki_ref[pl.ds(pl.multiple_of(kt * QB, QB), QB), :][:, :IDX_DIM]
        return _dot_nt(k_t, qidx)

    def score_tile(kt, lg):
        act = jnp.maximum(lg, 0.0) * wi_row
        sc = act[:, :QB]
        for h in range(1, IDX_HEADS):
            sc = sc + act[:, h * QB:(h + 1) * QB]
        sc = jnp.where(kt * QB + kpos0 <= qpos, sc, NEG_INF)
        sc = jnp.where(sc == 0.0, 0.0, sc)
        key_s[kt] = _sort_key(sc)

    _for_tiles_pairwise(nt, index_logits, score_tile)

    def count(pred):
        def hits(kt):
            return jnp.where(pred(kt, key_s[kt]), 1.0, 0.0)

        def pair(i, acc):
            return acc + (hits(2 * i) + hits(2 * i + 1))
        acc = lax.fori_loop(0, lax.shift_right_logical(nt, 1), pair, jnp.zeros((QB, QB), F32))
        acc = acc + jnp.where((nt & 1) == 1, hits(nt - 1), 0.0)
        return jnp.sum(acc, axis=0, keepdims=True)

    def thr_step(i, lo):
        cand = lo + (jnp.int32(1) << (31 - i))
        cnt = count(lambda kt, key: key >= cand)
        return jnp.where(cnt >= topk, cand, lo)

    thr = lax.fori_loop(0, 32, thr_step, jnp.full((1, QB), -2 ** 31, jnp.int32))
    need = topk - count(lambda kt, key: key > thr)

    def tie_step(i, j0):
        cand = j0 + (jnp.int32(1) << (10 - i))
        cnt = count(lambda kt, key: (key == thr) & (kt * QB + kpos0 < cand))
        return jnp.where(cnt < need, cand, j0)

    causal_ties = count(lambda kt, key: (key == thr) & (kt * QB + kpos0 <= qpos))
    must_search = jnp.max(jnp.where(causal_ties > need, 1.0, 0.0)) > 0.5
    j0 = lax.cond(must_search,
                  lambda: lax.fori_loop(0, 11, tie_step, jnp.zeros((1, QB), jnp.int32)),
                  lambda: jnp.full((1, QB), nq * QB, jnp.int32))

    m_s[...] = jnp.full(m_s.shape, NEG_INF, F32)
    l_s[...] = jnp.zeros(l_s.shape, F32)
    acc_s[...] = jnp.zeros(acc_s.shape, F32)

    def attend_logits(kt):
        return _dot_nt(c_s[kt], qlat)

    def attend_update(kt, s, d):
        key = key_s[kt]
        kpos = kt * QB + kpos0
        ok = ((key > thr) | ((key == thr) & (kpos <= j0))) & (kpos <= qpos)
        scale, probs = [], []
        for h in range(A_HEADS):
            sl = slice(h * QB, (h + 1) * QB)
            sh = jnp.where(ok, s[:, sl] if d is None else s[:, sl] + bias_ref[h, d], NEG_INF)
            m_old = m_s[:, sl]
            m_new = jnp.maximum(m_old, jnp.max(sh, axis=0, keepdims=True))
            scale.append(jnp.exp2(m_old - m_new))
            probs.append(jnp.where(ok, jnp.exp2(sh - m_new), 0.0).astype(BF16))
            m_s[:, sl] = m_new
        pv = _dot(ct_s[kt], jnp.concatenate(probs, axis=1))
        for h in range(A_HEADS):
            sl = slice(h * QB, (h + 1) * QB)
            acc_s[:, sl] = scale[h] * acc_s[:, sl] + pv[:A_LATENT, sl]
            l_s[:, sl] = scale[h] * l_s[:, sl] + pv[A_LATENT:A_LATENT + 1, sl]

    _for_tiles_pairwise(jnp.maximum(n - 1, 0), attend_logits, lambda kt, s: attend_update(kt, s, None))

    @pl.when(n >= 1)
    def _():
        s_prev = attend_logits(n - 1)
        s_diag = attend_logits(n)
        attend_update(n - 1, s_prev, 1)
        attend_update(n, s_diag, 0)

    @pl.when(n == 0)
    def _():
        attend_update(n, attend_logits(n), 0)

    for h in range(A_HEADS):
        sl = slice(h * QB, (h + 1) * QB)
        o_lat = (acc_s[:, sl] / l_s[:, sl]).T
        o_ref[:, h * HEAD_DIM:(h + 1) * HEAD_DIM] = _dot(o_lat.astype(BF16), wuv_ref[h]).astype(o_ref.dtype)


def dsa_attention(proj, aux, gain, w_uk, w_uv, tbl, batch, seq):
    nq = seq // QB
    topk = min(A_TOPK_MAX, seq // 4)
    qw = A_HEADS * HEAD_DIM
    return pl.pallas_call(
        functools.partial(_dsa_kernel, topk=topk),
        out_shape=jax.ShapeDtypeStruct((batch * seq, qw), BF16),
        grid=(batch, nq),
        in_specs=[
            pl.BlockSpec((QB, qw), lambda b, n: (b * nq + n, 0)),
            pl.BlockSpec((QB, qw), lambda b, n: (b * nq + n, 1)),
            pl.BlockSpec((seq, LANES), lambda b, n: (b, 12)),
            pl.BlockSpec((seq, A_LATENT), lambda b, n: (b, 0)),
            pl.BlockSpec((QB, LANES), lambda b, n: (b * nq + n, 1)),
            pl.BlockSpec((1, A_LATENT), lambda b, n: (0, 0)),
            pl.BlockSpec((A_HEADS, HEAD_DIM, A_LATENT), lambda b, n: (0, 0, 0)),
            pl.BlockSpec((A_HEADS, A_LATENT, HEAD_DIM), lambda b, n: (0, 0, 0)),
            pl.BlockSpec((A_HEADS, 2, QB, QB), lambda b, n: (0, 0, 0, 0)),
        ],
        out_specs=pl.BlockSpec((QB, qw), lambda b, n: (b * nq + n, 0)),
        scratch_shapes=[
            pltpu.VMEM((nq, QB, A_LATENT), BF16),
            pltpu.VMEM((nq, A_LATENT + ONES_ROWS, QB), BF16),
            pltpu.VMEM((nq, QB, QB), jnp.int32),
            pltpu.VMEM((1, A_HEADS * QB), F32),
            pltpu.VMEM((1, A_HEADS * QB), F32),
            pltpu.VMEM((A_LATENT, A_HEADS * QB), F32),
        ],
        compiler_params=_cparams("parallel", "arbitrary"),
        name="dsa_attention",
    )(proj, proj, proj, aux, aux, gain.reshape(1, A_LATENT).astype(F32),
      w_uk.astype(BF16), w_uv.astype(BF16), _tile_bias_masked_t(tbl, 2))


def _compress_kernel(x_ref, pe_ref, w1_ref, w2_ref, o_ref):
    half = CMP_STRIDE * HEAD_DIM
    x = x_ref[0]
    w1 = w1_ref[...]
    u = _dot(x, w1[:half])
    v = _dot(x, w1[half:])
    pre = u + pltpu.roll(v, v.shape[0] - 1, 0) + _dot(pe_ref[...], w1)[:1]
    h = 0.5 * pre * (1.0 + jnp.tanh(math.sqrt(2.0 / math.pi) * (pre + 0.044715 * pre * pre * pre)))
    o_ref[0] = _dot(h.astype(BF16), w2_ref[...]).astype(o_ref.dtype)


def compress_blocks(kv, pe, w1, w2, batch, seq):
    G = C_KV_HEADS
    nch = seq // CMP_STRIDE
    half = CMP_STRIDE * HEAD_DIM
    x = kv.reshape(batch, nch, CMP_STRIDE, G, HEAD_DIM).transpose(0, 3, 1, 2, 4)
    x = x.reshape(batch * G, nch, half)
    pe_flat = jnp.broadcast_to(pe.reshape(1, CMP_BLOCK * HEAD_DIM), (8, CMP_BLOCK * HEAD_DIM))
    return pl.pallas_call(
        _compress_kernel,
        out_shape=jax.ShapeDtypeStruct((batch * G, nch, HEAD_DIM), BF16),
        grid=(batch * G,),
        in_specs=[pl.BlockSpec((1, nch, half), lambda i: (i, 0, 0)),
                  pl.BlockSpec((8, 2 * half), lambda i: (0, 0)),
                  pl.BlockSpec((2 * half, CMP_HIDDEN), lambda i: (0, 0)),
                  pl.BlockSpec((CMP_HIDDEN, HEAD_DIM), lambda i: (0, 0))],
        out_specs=pl.BlockSpec((1, nch, HEAD_DIM), lambda i: (i, 0, 0)),
        compiler_params=_cparams("parallel"),
        name="compress_blocks",
    )(x, pe_flat.astype(BF16), w1.astype(BF16), w2.astype(BF16))


def _cmp_kernel(q_ref, kc_ref, vc_ref, bias_ref, gate_ref, ovt_ref, o_ref, pick_ref, *, n_cmp, n_slc, n_sel):
    n = pl.program_id(1)
    G = C_KV_HEADS
    hpg = C_HEADS // G
    nch = kc_ref.shape[1]
    c = lax.broadcasted_iota(jnp.int32, (nch, QB), 0)
    t = n * QB + lax.broadcasted_iota(jnp.int32, (nch, QB), 1)
    valid = (t - (c * CMP_STRIDE + CMP_BLOCK - 1) >= 0) & (c < n_cmp)
    jj = lax.broadcasted_iota(jnp.int32, (n_slc, QB), 0)
    ts = n * QB + lax.broadcasted_iota(jnp.int32, (n_slc, QB), 1)
    blk = ts // SLC_BLOCK
    bonus = FORCE_BONUS * jnp.where((jj == 0) | (jj == blk) | (jj == blk - 1), 1.0, 0.0)
    admissible = jj * SLC_BLOCK <= ts
    q = q_ref[...]
    gate_t = gate_ref[...].T
    for g in range(G):
        s = _dot_nt(kc_ref[g], _stack_heads(q, g * hpg, hpg, HEAD_DIM ** -0.5))
        vct = vc_ref[g].astype(F32).T.astype(BF16)
        psum = jnp.zeros((nch, QB), F32)
        probs = []
        for hi in range(hpg):
            sl = slice(hi * QB, (hi + 1) * QB)
            sh = jnp.where(valid, s[:, sl] + bias_ref[g * hpg + hi], NEG_INF)
            e = jnp.where(valid, jnp.exp(sh - jnp.max(sh, axis=0, keepdims=True)), 0.0)
            den = jnp.sum(e, axis=0, keepdims=True)
            p = e * (1.0 / jnp.where(den > 0.0, den, 1.0))
            psum = psum + p
            probs.append(p.astype(BF16))
        pv = _dot(vct, jnp.concatenate(probs, axis=1))
        for hi in range(hpg):
            h = g * hpg + hi
            o = pv[:, hi * QB:(hi + 1) * QB] * (1.0 / (1.0 + jnp.exp(-gate_t[h:h + 1, :])))
            o_ref[:, h * HEAD_DIM:(h + 1) * HEAD_DIM] = o.T.astype(o_ref.dtype)
        p_hi = psum.astype(BF16)
        p_lo = (psum - p_hi.astype(F32)).astype(BF16)
        p_slc = _dot(ovt_ref[...], p_hi) + _dot(ovt_ref[...], p_lo)
        score = jnp.where(admissible, p_slc + bonus, NEG_INF)
        rank = jnp.zeros((n_slc, QB), F32)
        for i in range(n_slc):
            si = score[i:i + 1, :]
            rank = rank + jnp.where(si > score, 1.0, 0.0) + jnp.where((si == score) & (i < jj), 1.0, 0.0)
        pick_ref[0, g] = jnp.where(rank < n_sel, 0.0, -MASK_BIG).astype(pick_ref.dtype)


def cmp_attention(proj, kc, vc, gates, tbl, batch, seq):
    nq = seq // QB
    G = C_KV_HEADS
    nch = seq // CMP_STRIDE
    n_cmp = (seq - CMP_BLOCK) // CMP_STRIDE + 1
    n_slc = seq // SLC_BLOCK
    n_sel = min(SLC_COUNT, n_slc)
    qw = C_HEADS * HEAD_DIM
    cmp_end = np.arange(nch) * CMP_STRIDE + CMP_BLOCK - 1
    dist_c = jnp.asarray(np.arange(seq)[None, :] - cmp_end[:, None])
    bias_c = _bias_lookup(tbl, dist_c).transpose(2, 0, 1)
    cs = np.arange(nch)[None, :] * CMP_STRIDE
    ss = np.arange(n_slc)[:, None] * SLC_BLOCK
    overlap_t = ((cs < ss + SLC_BLOCK) & (cs + CMP_BLOCK > ss) & (np.arange(nch)[None, :] < n_cmp))
    return pl.pallas_call(
        functools.partial(_cmp_kernel, n_cmp=n_cmp, n_slc=n_slc, n_sel=n_sel),
        out_shape=(jax.ShapeDtypeStruct((batch * seq, qw), BF16),
                   jax.ShapeDtypeStruct((batch, G, n_slc, seq), BF16)),
        grid=(batch, nq),
        in_specs=[
            pl.BlockSpec((QB, qw), lambda b, n: (b * nq + n, 0)),
            pl.BlockSpec((G, nch, HEAD_DIM), lambda b, n: (b, 0, 0)),
            pl.BlockSpec((G, nch, HEAD_DIM), lambda b, n: (b, 0, 0)),
            pl.BlockSpec((C_HEADS, nch, QB), lambda b, n: (0, 0, n)),
            pl.BlockSpec((QB, LANES), lambda b, n: (b * nq + n, 0)),
            pl.BlockSpec((n_slc, nch), lambda b, n: (0, 0)),
        ],
        out_specs=(pl.BlockSpec((QB, qw), lambda b, n: (b * nq + n, 0)),
                   pl.BlockSpec((1, G, n_slc, QB), lambda b, n: (b, 0, 0, n))),
        compiler_params=_cparams("parallel", "parallel"),
        name="cmp_attention",
    )(proj, kc, vc, bias_c, gates, jnp.asarray(overlap_t.astype(np.float32)).astype(BF16))


def _slc_kernel(q_ref, k_ref, v_ref, pick_ref, bias_ref, gate_ref, o_ref, vt_s, m_s, l_s, acc_s, *, n_slc):
    n = pl.program_id(1)
    G = C_KV_HEADS
    hpg = C_HEADS // G
    nq = vt_s.shape[0]

    @pl.when(n == 0)
    def _():
        for kt in range(nq):
            vt_s[kt] = _value_tile_t(v_ref[kt * QB:(kt + 1) * QB, :], G)

    q = q_ref[...].astype(F32)
    rhs = []
    for g in range(G):
        qt = jnp.concatenate(
            [(q[:, (g * hpg + hi) * HEAD_DIM:(g * hpg + hi + 1) * HEAD_DIM].T * (HEAD_DIM ** -0.5 * LOG2E)).astype(BF16)
             for hi in range(hpg)], axis=1)
        rhs.append(jnp.concatenate([qt, jnp.concatenate([pick_ref[0, g]] * hpg, axis=1)], axis=0))
    ej = lax.broadcasted_iota(jnp.int32, (QB, n_slc), 0)
    eb = lax.broadcasted_iota(jnp.int32, (QB, n_slc), 1)
    m_s[...] = jnp.full(m_s.shape, NEG_INF, F32)
    l_s[...] = jnp.zeros(l_s.shape, F32)
    acc_s[...] = jnp.zeros(acc_s.shape, F32)
    vrows = HEAD_DIM + ONES_ROWS

    def logits(kt):
        k_t = k_ref[pl.ds(pl.multiple_of(kt * QB, QB), QB), :]
        expand = jnp.where(eb == (kt * QB + ej) // SLC_BLOCK, 1.0, 0.0).astype(BF16)
        return [_dot(jnp.concatenate([k_t[:, g * HEAD_DIM:(g + 1) * HEAD_DIM], expand], axis=1), rhs[g])
                for g in range(G)]

    def update(kt, s, d):
        vt_t = vt_s[kt]
        for g in range(G):
            _flash_tile(s[g], None if d is None else (lambda hi, g=g: bias_ref[g * hpg + hi, d]),
                        m_s.at[g], l_s.at[g], acc_s.at[g], vt_t[g * vrows:(g + 1) * vrows, :], hpg)

    _for_tiles_pairwise(jnp.maximum(n - 1, 0), logits, lambda kt, s: update(kt, s, None))

    @pl.when(n >= 1)
    def _():
        s_prev = logits(n - 1)
        s_diag = logits(n)
        update(n - 1, s_prev, 1)
        update(n, s_diag, 0)

    @pl.when(n == 0)
    def _():
        update(n, logits(n), 0)

    gate_t = gate_ref[...].T
    for g in range(G):
        _store_heads_t(o_ref, acc_s.at[g], l_s.at[g], g * hpg, hpg, gate_t)


def slc_attention(proj, pick, gates, tbl, batch, seq):
    nq = seq // QB
    G = C_KV_HEADS
    n_slc = seq // SLC_BLOCK
    qw = C_HEADS * HEAD_DIM
    return pl.pallas_call(
        functools.partial(_slc_kernel, n_slc=n_slc),
        out_shape=jax.ShapeDtypeStruct((batch * seq, qw), BF16),
        grid=(batch, nq),
        in_specs=[
            pl.BlockSpec((QB, qw), lambda b, n: (b * nq + n, 0)),
            pl.BlockSpec((seq, LANES), lambda b, n: (b, 10)),
            pl.BlockSpec((seq, LANES), lambda b, n: (b, 11)),
            pl.BlockSpec((1, G, n_slc, QB), lambda b, n: (b, 0, 0, n)),
            pl.BlockSpec((C_HEADS, 2, QB, QB), lambda b, n: (0, 0, 0, 0)),
            pl.BlockSpec((QB, LANES), lambda b, n: (b * nq + n, 1)),
        ],
        out_specs=pl.BlockSpec((QB, qw), lambda b, n: (b * nq + n, 0)),
        scratch_shapes=[pltpu.VMEM((nq, G * (HEAD_DIM + ONES_ROWS), QB), BF16),
                        pltpu.VMEM((G, 1, C_HEADS // G * QB), F32),
                        pltpu.VMEM((G, 1, C_HEADS // G * QB), F32),
                        pltpu.VMEM((G, HEAD_DIM, C_HEADS // G * QB), F32)],
        compiler_params=_cparams("parallel", "arbitrary"),
        name="slc_attention",
    )(proj, proj, proj, pick, _tile_bias_masked_t(tbl, 2), gates)


def _xattn_kernel(h_ref, wq_ref, k_ref, v_ref, wo_ref, g_ref, b_ref, o_ref):
    x = h_ref[...]
    q = _dot(x.astype(BF16), wq_ref[...]).astype(BF16)
    k = k_ref[...]
    v = v_ref[...]
    outs = []
    for hd in range(X_HEADS):
        sl = slice(hd * X_HEAD_DIM, (hd + 1) * X_HEAD_DIM)
        lg = _dot_nt(q[:, sl], k[:, sl]) * X_HEAD_DIM ** -0.5
        e = jnp.exp(lg - jnp.max(lg, axis=-1, keepdims=True))
        den = jnp.sum(e, axis=-1, keepdims=True)
        outs.append((_dot(e.astype(BF16), v[:, sl]) / den).astype(BF16))
    o = jnp.concatenate(outs, axis=-1)
    y = ALPHA * x + _dot(o, wo_ref[...])
    o_ref[...] = _layer_norm_rows(y, g_ref[...], b_ref[...])


def cross_attention_ln(h, mem, w_q, w_k, w_v, w_o, g, b, batch, seq, tm=256):
    M = mem.shape[0] // batch
    XW = X_HEADS * X_HEAD_DIM
    D = D_MODEL
    kv = matmul(mem, jnp.concatenate([w_k, w_v], axis=1).astype(BF16), BF16, tm=512)
    nt = seq // tm
    fixed = lambda bb, i: (0, 0)
    return pl.pallas_call(
        _xattn_kernel,
        out_shape=jax.ShapeDtypeStruct((batch * seq, D), F32),
        grid=(batch, nt),
        in_specs=[pl.BlockSpec((tm, D), lambda bb, i: (bb * nt + i, 0)),
                  pl.BlockSpec((D, XW), fixed),
                  pl.BlockSpec((M, XW), lambda bb, i: (bb, 0)),
                  pl.BlockSpec((M, XW), lambda bb, i: (bb, 1)),
                  pl.BlockSpec((XW, D), fixed),
                  pl.BlockSpec((1, D), fixed), pl.BlockSpec((1, D), fixed)],
        out_specs=pl.BlockSpec((tm, D), lambda bb, i: (bb * nt + i, 0)),
        compiler_params=_cparams("parallel", "parallel"),
        name="cross_attention_ln",
    )(h, w_q.astype(BF16), kv, kv, w_o.astype(BF16), g.reshape(1, D), b.reshape(1, D))


def _silu(a):
    return a * (1.0 / (1.0 + jnp.exp(-a)))


def _swiglu_kernel(*refs, n_cast):
    x_ref, wg_ref, wu_ref, wd_ref, g_ref, b_ref = refs[:6]
    cast_in = refs[6:6 + n_cast]
    o_ref = refs[6 + n_cast]
    cast_out = refs[7 + n_cast:]
    x = x_ref[...]
    xb = x.astype(BF16)
    hmid = (_silu(_dot(xb, wg_ref[...])) * _dot(xb, wu_ref[...])).astype(BF16)
    y = ALPHA * x + _dot(hmid, wd_ref[...])
    o_ref[...] = _layer_norm_rows(y, g_ref[...], b_ref[...])
    for src, dst in zip(cast_in, cast_out):
        dst[...] = src[...].astype(dst.dtype)


def swiglu_ln(h, w_gate, w_up, w_down, g, b, cast_also=(), tm=256):
    T, D = h.shape
    F = w_gate.shape[1]
    steps = T // tm
    fixed = lambda i: (0, 0)
    row = lambda i: (i, 0)
    once = pl.Buffered(1)
    flat = [w.reshape(-1, w.shape[-1]) for w in cast_also]
    slabs = [pl.BlockSpec((w.shape[0] // steps, w.shape[1]), row) for w in flat]
    outs = pl.pallas_call(
        functools.partial(_swiglu_kernel, n_cast=len(flat)),
        out_shape=[jax.ShapeDtypeStruct((T, D), F32)] + [jax.ShapeDtypeStruct(w.shape, BF16) for w in flat],
        grid=(steps,),
        in_specs=[pl.BlockSpec((tm, D), row),
                  pl.BlockSpec((D, F), fixed, pipeline_mode=once),
                  pl.BlockSpec((D, F), fixed, pipeline_mode=once),
                  pl.BlockSpec((F, D), fixed, pipeline_mode=once),
                  pl.BlockSpec((1, D), fixed), pl.BlockSpec((1, D), fixed)] + slabs,
        out_specs=[pl.BlockSpec((tm, D), row)] + slabs,
        compiler_params=_cparams("parallel"),
        name="swiglu_ln",
    )(h, cast_bf16(w_gate), cast_bf16(w_up), cast_bf16(w_down), g.reshape(1, D), b.reshape(1, D), *flat)
    return outs[0], [o.reshape(w.shape) for o, w in zip(outs[1:], cast_also)]


def _router_kernel(x_ref, w_ref, idx_ref, wgt_ref, xlo_ref, xhi_ref):
    x = x_ref[...]
    w = w_ref[...]
    x_hi = x.astype(BF16)
    x_lo = (x - x_hi.astype(F32)).astype(BF16)
    w_hi = w.astype(BF16)
    w_lo = (w - w_hi.astype(F32)).astype(BF16)
    lg = _dot(x_hi, w_hi) + (_dot(x_hi, w_lo) + _dot(x_lo, w_hi))
    e_iota = lax.broadcasted_iota(jnp.int32, lg.shape, 1).astype(F32)
    m1 = jnp.max(lg, axis=-1, keepdims=True)
    i1 = jnp.min(jnp.where(lg == m1, e_iota, float(N_EXPERTS)), axis=-1, keepdims=True)
    first = e_iota == i1
    rest = jnp.where(first, -jnp.inf, lg)
    m2 = jnp.max(rest, axis=-1, keepdims=True)
    i2 = jnp.min(jnp.where(rest == m2, e_iota, float(N_EXPERTS)), axis=-1, keepdims=True)
    w2 = jnp.exp(m2 - m1)
    den = 1.0 + w2
    k_iota = lax.broadcasted_iota(jnp.int32, idx_ref.shape, 1)
    idx_ref[...] = jnp.where(k_iota == 0, i1, i2).astype(jnp.int32)
    wgt_ref[...] = jnp.where(k_iota == 0, 1.0 / den, w2 / den)
    half = x.shape[1] // 2
    xlo_ref[...] = _pack_bf16_pairs(x[:, :half])
    xhi_ref[...] = _pack_bf16_pairs(x[:, half:])


def _pack_bf16_pairs(x):
    w = x.shape[1] // 2
    lo = lax.bitcast_convert_type(x[:, :w].astype(BF16).astype(F32), jnp.uint32)
    hi = lax.bitcast_convert_type(x[:, w:].astype(BF16).astype(F32), jnp.uint32)
    return (lo >> 16) | (hi & jnp.uint32(0xFFFF0000))


def _unpack_bf16_pairs(words):
    lo = lax.bitcast_convert_type(words << 16, F32).astype(BF16)
    hi = lax.bitcast_convert_type(words & jnp.uint32(0xFFFF0000), F32).astype(BF16)
    return jnp.concatenate([lo, hi], axis=1)


def moe_route(h, w_router, tm=1024):
    T, D = h.shape
    E = w_router.shape[1]
    row = lambda i: (i, 0)
    return pl.pallas_call(
        _router_kernel,
        out_shape=(jax.ShapeDtypeStruct((T, TOP_K), jnp.int32), jax.ShapeDtypeStruct((T, TOP_K), F32),
                   jax.ShapeDtypeStruct((T, D // 4), jnp.uint32), jax.ShapeDtypeStruct((T, D // 4), jnp.uint32)),
        grid=(T // tm,),
        in_specs=[pl.BlockSpec((tm, D), row), pl.BlockSpec((D, E), lambda i: (0, 0))],
        out_specs=(pl.BlockSpec((tm, TOP_K), row), pl.BlockSpec((tm, TOP_K), row),
                   pl.BlockSpec((tm, D // 4), row), pl.BlockSpec((tm, D // 4), row)),
        compiler_params=_cparams("parallel"),
        name="moe_route",
    )(h, w_router)


def sc_gather_rows(xs, idx, window):
    n_idx = idx.shape[0]
    n_arr = len(xs)
    mesh = plsc.VectorSubcoreMesh(core_axis_name="core", subcore_axis_name="subcore")

    @pl.kernel(out_type=[jax.ShapeDtypeStruct((n_idx, x.shape[1]), x.dtype) for x in xs], mesh=mesh,
               scratch_types=[], name="sc_gather_rows")
    def gather(*refs):
        x_hbms, i_hbm, o_hbms = refs[:n_arr], refs[n_arr], refs[n_arr + 1:]
        for x_hbm, o_hbm in zip(x_hbms, o_hbms):
            def body(i_vmem, o_vmem, x_hbm=x_hbm):
                pltpu.sync_copy(x_hbm.at[i_vmem.at[0]], o_vmem)

            pltpu.emit_pipeline(
                body,
                grid=(n_idx // window,),
                in_specs=[pl.BlockSpec((1, window), lambda i: (0, i))],
                out_specs=[pl.BlockSpec((window, x_hbm.shape[1]), lambda i: (i, 0))],
                core_axis_name=("core", "subcore"),
                dimension_semantics=(pltpu.PARALLEL,),
            )(i_hbm, o_hbm)

    return gather(*xs, idx.reshape(1, n_idx))


def _moe_plan(top_idx, tm):
    T = top_idx.shape[0]
    n_asg = T * TOP_K
    n_tiles = n_asg // tm + N_EXPERTS
    flat_e = top_idx.reshape(n_asg)
    onehot = (flat_e[:, None] == jnp.arange(N_EXPERTS, dtype=jnp.int32)[None, :]).astype(jnp.int32)
    incl = jnp.cumsum(onehot, axis=0)
    counts = incl[-1]
    rank = jnp.sum((incl - onehot) * onehot, axis=1)
    tiles_e = (counts + tm - 1) // tm
    tile_end = jnp.cumsum(tiles_e)
    row_start_p = (tile_end - tiles_e) * tm
    row_start = jnp.cumsum(counts) - counts
    pos = (jnp.sum(onehot * row_start_p[None, :], axis=1) + rank).reshape(T, TOP_K).T.reshape(n_asg)
    tile_ids = jnp.arange(n_tiles, dtype=jnp.int32)
    tile_expert = jnp.minimum(jnp.searchsorted(tile_end, tile_ids, side="right"), N_EXPERTS - 1).astype(jnp.int32)
    n_rows = n_tiles * tm
    sorted_tok = jnp.argsort(flat_e, stable=True).astype(jnp.int32) // TOP_K
    window_src = jnp.concatenate([jnp.zeros((n_rows,), jnp.int32), sorted_tok, jnp.zeros((n_rows,), jnp.int32)])
    r = jnp.arange(n_rows, dtype=jnp.int32)
    src_tok = jnp.zeros((n_rows,), jnp.int32)
    for e in range(N_EXPERTS):
        shifted = lax.dynamic_slice(window_src, (n_rows - (row_start_p[e] - row_start[e]),), (n_rows,))
        in_run = (r >= row_start_p[e]) & (r < row_start_p[e] + counts[e])
        src_tok = jnp.where(in_run, shifted, src_tok)
    return src_tok, pos.astype(jnp.int32), tile_expert, tile_end[-1:].astype(jnp.int32)


def _moe_ffn_kernel(te_ref, nu_ref, xlo_ref, xhi_ref, wg_ref, wu_ref, wd_ref, olo_ref, ohi_ref, acc_s):
    i = pl.program_id(0)
    c = pl.program_id(1)
    last = pl.num_programs(1) - 1
    used = i < nu_ref[0]
    half = acc_s.shape[1] // 2

    @pl.when(used & (c == 0))
    def _():
        acc_s[...] = jnp.zeros(acc_s.shape, F32)

    @pl.when(used)
    def _():
        x = jnp.concatenate([_unpack_bf16_pairs(xlo_ref[...]), _unpack_bf16_pairs(xhi_ref[...])], axis=1)
        hmid = _silu(_dot(x, wg_ref[0])) * _dot(x, wu_ref[0])
        acc_s[...] += _dot(hmid.astype(BF16), wd_ref[0])

    @pl.when(used & (c == last))
    def _():
        olo_ref[...] = _pack_bf16_pairs(acc_s[:, :half])
        ohi_ref[...] = _pack_bf16_pairs(acc_s[:, half:])

    @pl.when(jnp.logical_not(used) & (c == last))
    def _():
        olo_ref[...] = jnp.zeros(olo_ref.shape, olo_ref.dtype)
        ohi_ref[...] = jnp.zeros(ohi_ref.shape, ohi_ref.dtype)


def moe_expert_ffn(xs_lo, xs_hi, tile_expert, n_used, w_gate, w_up, w_down, tm, fc=1792):
    R, half = xs_lo.shape
    D = 4 * half
    E, _, F = w_gate.shape
    rows = lambda i, c, te, nu: (i, 0)
    return pl.pallas_call(
        _moe_ffn_kernel,
        out_shape=(jax.ShapeDtypeStruct((R, half), jnp.uint32), jax.ShapeDtypeStruct((R, half), jnp.uint32)),
        grid_spec=pltpu.PrefetchScalarGridSpec(
            num_scalar_prefetch=2,
            grid=(R // tm, F // fc),
            in_specs=[pl.BlockSpec((tm, half), rows), pl.BlockSpec((tm, half), rows),
                      pl.BlockSpec((1, D, fc), lambda i, c, te, nu: (te[i], 0, c)),
                      pl.BlockSpec((1, D, fc), lambda i, c, te, nu: (te[i], 0, c)),
                      pl.BlockSpec((1, fc, D), lambda i, c, te, nu: (te[i], c, 0))],
            out_specs=(pl.BlockSpec((tm, half), rows), pl.BlockSpec((tm, half), rows)),
            scratch_shapes=[pltpu.VMEM((tm, D), F32)]),
        compiler_params=_cparams("parallel", "arbitrary"),
        name="moe_expert_ffn",
    )(tile_expert, n_used, xs_lo, xs_hi, w_gate, w_up, w_down)


def _moe_combine_kernel(h_ref, lo0_ref, hi0_ref, lo1_ref, hi1_ref, w_ref, g_ref, b_ref, o_ref):
    w = w_ref[...]
    y0 = jnp.concatenate([_unpack_bf16_pairs(lo0_ref[...]), _unpack_bf16_pairs(hi0_ref[...])], axis=1).astype(F32)
    y1 = jnp.concatenate([_unpack_bf16_pairs(lo1_ref[...]), _unpack_bf16_pairs(hi1_ref[...])], axis=1).astype(F32)
    ff = w[:, 0:1] * y0 + w[:, 1:2] * y1
    o_ref[...] = _layer_norm_rows(ALPHA * h_ref[...] + ff, g_ref[...], b_ref[...])


def moe_combine_ln(h, y_lo, y_hi, top_w, g, b, tm=512):
    T, D = h.shape
    half = D // 4
    nt = T // tm
    row = lambda i: (i, 0)
    second = lambda i: (nt + i, 0)
    fixed = lambda i: (0, 0)
    return pl.pallas_call(
        _moe_combine_kernel,
        out_shape=jax.ShapeDtypeStruct((T, D), F32),
        grid=(nt,),
        in_specs=[pl.BlockSpec((tm, D), row),
                  pl.BlockSpec((tm, half), row), pl.BlockSpec((tm, half), row),
                  pl.BlockSpec((tm, half), second), pl.BlockSpec((tm, half), second),
                  pl.BlockSpec((tm, TOP_K), row),
                  pl.BlockSpec((1, D), fixed), pl.BlockSpec((1, D), fixed)],
        out_specs=pl.BlockSpec((tm, D), row),
        compiler_params=_cparams("parallel"),
        name="moe_combine_ln",
    )(h, y_lo, y_hi, y_lo, y_hi, top_w, g.reshape(1, D), b.reshape(1, D))


def moe_ln(h, w_router, w_gate, w_up, w_down, g, b, tm=512):
    top_idx, top_w, hb_lo, hb_hi = moe_route(h, w_router)
    src_tok, pos, tile_expert, n_used = _moe_plan(top_idx, tm)
    xs_lo, xs_hi = sc_gather_rows([hb_lo, hb_hi], src_tok, SC_GATHER_WINDOW)
    ys_lo, ys_hi = moe_expert_ffn(xs_lo, xs_hi, tile_expert, n_used, w_gate, w_up, w_down, tm)
    y_lo, y_hi = sc_gather_rows([ys_lo, ys_hi], pos, SC_GATHER_WINDOW)
    return moe_combine_ln(h, y_lo, y_hi, top_w, g, b)


def _pad_cols(w, width):
    return jnp.pad(w, ((0, 0), (0, width - w.shape[1])))


def even_layer_mixer(h, w_in, ckv_gain, w_uk, w_uv, sinks, w_out, rel_table, ln_g, ln_b, batch, seq):
    cuts = np.cumsum((A_HEADS * HEAD_DIM, A_LATENT, IDX_HEADS * IDX_DIM, IDX_DIM, IDX_HEADS,
                      B_HEADS * HEAD_DIM, B_KV_HEADS * HEAD_DIM))
    w_qa, w_ckv, w_qi, w_ki, w_wi, w_qb, w_kb, w_vb = jnp.split(w_in, [int(c) for c in cuts], axis=1)
    w_main = jnp.concatenate([w_qa, w_qi, w_qb, _pad_cols(w_ki, LANES), w_kb, w_vb], axis=1).astype(BF16)
    w_aux = jnp.concatenate([w_ckv, _pad_cols(w_wi, LANES)], axis=1).astype(BF16)
    proj = matmul(h, w_main, BF16)
    aux = matmul(h, w_aux, F32)
    o_a = dsa_attention(proj, aux, ckv_gain, w_uk, w_uv, rel_table[:, :A_HEADS], batch, seq)
    o_b = banded_attention(proj, 2, 13, 14, B_HEADS, B_KV_HEADS,
                           rel_table[:, A_HEADS:A_HEADS + B_HEADS], B_WINDOW,
                           batch, seq, sinks=sinks)
    o = jnp.concatenate([o_a, o_b], axis=1)
    return matmul_residual_ln([o], w_out.astype(BF16), h, ln_g, ln_b)


def odd_layer_mixer(h, w_in, pe_k, w1_k, w2_k, pe_v, w1_v, w2_v, w_out, rel_table, ln_g, ln_b, batch, seq):
    kvw = C_KV_HEADS * HEAD_DIM
    qw = C_HEADS * HEAD_DIM
    w_main = w_in[:, :qw + 6 * kvw].astype(BF16)
    w_gl = w_in[:, qw + 6 * kvw:].reshape(D_MODEL, C_HEADS, 3)
    w_gate = jnp.concatenate([_pad_cols(w_gl[:, :, r], LANES) for r in range(3)], axis=1).astype(BF16)
    proj = matmul(h, w_main, BF16)
    gates = matmul(h, w_gate, F32)
    tbl = rel_table[:, :C_HEADS]
    kc = compress_blocks(proj[:, qw:qw + kvw], pe_k, w1_k, w2_k, batch, seq)
    vc = compress_blocks(proj[:, qw + kvw:qw + 2 * kvw], pe_v, w1_v, w2_v, batch, seq)
    o_cmp, pick = cmp_attention(proj, kc, vc, gates, tbl, batch, seq)
    o_slc = slc_attention(proj, pick, gates, tbl, batch, seq)
    o_win = banded_attention(proj, 0, 12, 13, C_HEADS, C_KV_HEADS, tbl, C_WINDOW,
                             batch, seq, gates=gates, gate_col=2)
    return matmul_residual_ln([o_cmp, o_slc, o_win], w_out.astype(BF16), h, ln_g, ln_b)


def kernel(x, mem, rel_table, ev_w_in, ev_ckv_gain, ev_w_uk, ev_w_uv, ev_sinks, ev_w_out, od_w_in, od_pe_k, od_w1_k, od_w2_k, od_pe_v, od_w1_v, od_w2_v, od_w_out, xa_w_q, xa_w_k, xa_w_v, xa_w_o, ff_w_gate, ff_w_up, ff_w_down, moe_w_router, moe_w_gate, moe_w_up, moe_w_down, ln_g, ln_b):
    batch, seq, D = x.shape
    h = x.reshape(batch * seq, D)
    mem2 = mem.reshape(-1, D)
    for i in range(DEPTH):
        j = i // 2
        if i % 2 == 0:
            h = even_layer_mixer(h, ev_w_in[j], ev_ckv_gain[j], ev_w_uk[j], ev_w_uv[j], ev_sinks[j],
                                 ev_w_out[j], rel_table, ln_g[i, 0], ln_b[i, 0], batch, seq)
        else:
            h = odd_layer_mixer(h, od_w_in[j], od_pe_k[j], od_w1_k[j], od_w2_k[j], od_pe_v[j],
                                od_w1_v[j], od_w2_v[j], od_w_out[j], rel_table, ln_g[i, 0], ln_b[i, 0],
                                batch, seq)
        h = cross_attention_ln(h, mem2, xa_w_q[i], xa_w_k[i], xa_w_v[i], xa_w_o[i],
                               ln_g[i, 1], ln_b[i, 1], batch, seq)
        if i % 2 == 0:
            riders = (moe_w_gate[j], moe_w_up[j], moe_w_down[j]) if i + 1 < DEPTH else ()
            h, moe_bf16 = swiglu_ln(h, ff_w_gate[j], ff_w_up[j], ff_w_down[j], ln_g[i, 2], ln_b[i, 2], riders)
        else:
            h = moe_ln(h, moe_w_router[j], *moe_bf16, ln_g[i, 2], ln_b[i, 2])
    return h.reshape(batch, seq, D)
```

```python
import functools
import math

import numpy as np
import jax
import jax.numpy as jnp
from jax import lax
from jax.experimental import pallas as pl
from jax.experimental.pallas import tpu as pltpu
from jax.experimental.pallas import tpu_sc as plsc

D_MODEL = 1024
DEPTH = 2
HEAD_DIM = 64
N_SLOTS = D_MODEL // HEAD_DIM
A_HEADS = 8
A_LATENT = 128
IDX_HEADS = 8
IDX_DIM = 64
A_TOPK_MAX = 256
B_HEADS = 8
B_KV_HEADS = 2
B_WINDOW = 128
C_HEADS = 16
C_KV_HEADS = 2
CMP_BLOCK = 32
CMP_STRIDE = 16
CMP_HIDDEN = 128
SLC_BLOCK = 64
SLC_COUNT = 8
C_WINDOW = 512
X_HEADS = 4
X_HEAD_DIM = 128
D_FF = 2816
N_EXPERTS = 8
TOP_K = 2
D_FF_EXPERT = 3584
REL_BUCKETS = 32
REL_MAX_DIST = 128
LN_EPS = 1e-5
NEG_INF = -1e30
FORCE_BONUS = 1e6
ALPHA = (2 * DEPTH) ** 0.25
MASK_BIG = 2.0 ** 100
LOG2E = math.log2(math.e)
ONES_ROWS = 16

LANES = 128
QB = 128
VMEM_LIMIT_BYTES = 56 * 1024 * 1024
SC_GATHER_WINDOW = 128

F32 = jnp.float32
BF16 = jnp.bfloat16


def _cparams(*sem):
    return pltpu.CompilerParams(dimension_semantics=sem, vmem_limit_bytes=VMEM_LIMIT_BYTES)


def _dot(a, b):
    return jnp.dot(a, b, preferred_element_type=F32)


def _dot_nt(a, b):
    return lax.dot_general(a, b, (((1,), (1,)), ((), ())), preferred_element_type=F32)


def _layer_norm_rows(y, g, b):
    mu = jnp.mean(y, axis=-1, keepdims=True)
    yc = y - mu
    var = jnp.mean(yc * yc, axis=-1, keepdims=True)
    return yc * lax.rsqrt(var + LN_EPS) * g + b


def _mm_kernel(x_ref, w_ref, o_ref):
    o_ref[...] = _dot(x_ref[...].astype(BF16), w_ref[...]).astype(o_ref.dtype)


def matmul(x, w, out_dtype, tm=512):
    M, K = x.shape
    N = w.shape[1]
    tm = min(tm, M)
    return pl.pallas_call(
        _mm_kernel,
        out_shape=jax.ShapeDtypeStruct((M, N), out_dtype),
        grid=(M // tm,),
        in_specs=[pl.BlockSpec((tm, K), lambda i: (i, 0)),
                  pl.BlockSpec((K, N), lambda i: (0, 0))],
        out_specs=pl.BlockSpec((tm, N), lambda i: (i, 0)),
        compiler_params=_cparams("parallel"),
        name="matmul",
    )(x, w)


def _mm_pair_kernel(x_ref, w_ref, waux_ref, o_ref, oaux_ref):
    xb = x_ref[...].astype(BF16)
    o_ref[...] = _dot(xb, w_ref[...]).astype(o_ref.dtype)
    oaux_ref[...] = _dot(xb, waux_ref[...])


def matmul_pair(x, w, w_aux, tm=512):
    M, K = x.shape
    N, NA = w.shape[1], w_aux.shape[1]
    row = lambda i: (i, 0)
    fixed = lambda i: (0, 0)
    return pl.pallas_call(
        _mm_pair_kernel,
        out_shape=(jax.ShapeDtypeStruct((M, N), BF16), jax.ShapeDtypeStruct((M, NA), F32)),
        grid=(M // tm,),
        in_specs=[pl.BlockSpec((tm, K), row), pl.BlockSpec((K, N), fixed), pl.BlockSpec((K, NA), fixed)],
        out_specs=(pl.BlockSpec((tm, N), row), pl.BlockSpec((tm, NA), row)),
        compiler_params=_cparams("parallel"),
        name="matmul_pair",
    )(x, w, w_aux)


def _cast_kernel(*refs):
    o_ref = refs[-1]
    cw = refs[0].shape[1]
    for j, x_ref in enumerate(refs[:-1]):
        o_ref[:, j * cw:(j + 1) * cw] = x_ref[...].astype(o_ref.dtype)


def cast_bf16(w, block_bytes=8 * 1024 * 1024):
    shape = w.shape
    C = shape[-1]
    M = int(np.prod(shape[:-1]))
    tm = M
    while tm * C * 4 > block_bytes and tm % 32 == 0:
        tm //= 2
    n_slab = next(k for k in (4, 2, 1) if C % (k * LANES) == 0)
    cw = C // n_slab
    w2 = w.reshape(M, C)
    out = pl.pallas_call(
        _cast_kernel,
        out_shape=jax.ShapeDtypeStruct((M, C), BF16),
        grid=(M // tm,),
        in_specs=[pl.BlockSpec((tm, cw), lambda i, j=j: (i, j)) for j in range(n_slab)],
        out_specs=pl.BlockSpec((tm, C), lambda i: (i, 0)),
        compiler_params=_cparams("parallel"),
        name="cast_bf16",
    )(*([w2] * n_slab))
    return out.reshape(shape)


def _mm_res_ln_kernel(*refs, n_in, concat):
    a_refs = refs[:n_in]
    w_ref, h_ref, g_ref, b_ref, o_ref = refs[n_in:]
    if concat:
        k0 = 0
        mix = None
        for r in a_refs:
            part = _dot(r[...], w_ref[k0:k0 + r.shape[1], :])
            mix = part if mix is None else mix + part
            k0 += r.shape[1]
    else:
        a = a_refs[0][...]
        if n_in > 1:
            a = a.astype(F32)
            for r in a_refs[1:]:
                a = a + r[...].astype(F32)
        mix = _dot(a.astype(BF16), w_ref[...])
    y = ALPHA * h_ref[...] + mix
    o_ref[...] = _layer_norm_rows(y, g_ref[...], b_ref[...])


def matmul_residual_ln(a_list, w, h, g, b, concat=False, tm=512):
    M = a_list[0].shape[0]
    K, N = w.shape
    n_in = len(a_list)
    row = lambda i: (i, 0)
    fixed = lambda i: (0, 0)
    return pl.pallas_call(
        functools.partial(_mm_res_ln_kernel, n_in=n_in, concat=concat),
        out_shape=jax.ShapeDtypeStruct((M, N), F32),
        grid=(M // tm,),
        in_specs=[pl.BlockSpec((tm, a.shape[1]), row) for a in a_list] + [
            pl.BlockSpec((K, N), fixed), pl.BlockSpec((tm, N), row),
            pl.BlockSpec((1, N), fixed), pl.BlockSpec((1, N), fixed)],
        out_specs=pl.BlockSpec((tm, N), row),
        compiler_params=_cparams("parallel"),
        name="matmul_residual_ln",
    )(*a_list, w, h, g.reshape(1, N), b.reshape(1, N))


def _rel_bucket(dist):
    n = jnp.maximum(dist, 0)
    max_exact = REL_BUCKETS // 2
    nf = jnp.maximum(n, 1).astype(F32)
    log_b = max_exact + (jnp.log(nf / max_exact) / math.log(REL_MAX_DIST / max_exact)
                         * (REL_BUCKETS - max_exact)).astype(jnp.int32)
    return jnp.where(n < max_exact, n, jnp.minimum(log_b, REL_BUCKETS - 1))


def _bias_lookup(tbl, dist):
    onehot = (_rel_bucket(dist)[..., None] == jnp.arange(REL_BUCKETS, dtype=jnp.int32)).astype(F32)
    return jnp.einsum("...r,rh->...h", onehot, tbl.astype(F32), precision=lax.Precision.HIGHEST)


def _tile_bias_masked_t(tbl, n_tiles, window=None):
    j = np.arange(QB)[:, None]
    i = np.arange(QB)[None, :]
    dist = np.stack([d * QB + i - j for d in range(n_tiles)])
    valid = dist >= 0
    if window is not None:
        valid &= dist < window
    rel = _bias_lookup(tbl, jnp.asarray(dist)) - _far_bias(tbl)
    return jnp.where(jnp.asarray(valid)[..., None], rel * LOG2E, NEG_INF).transpose(3, 0, 1, 2)


def _far_bias(tbl):
    return tbl.astype(F32)[REL_BUCKETS - 1]


def _flash_tile(s, add, m_ref, l_ref, acc_ref, vt_ones, hpg):
    dv = acc_ref.shape[0]
    scale, probs = [], []
    for hi in range(hpg):
        sl = slice(hi * QB, (hi + 1) * QB)
        sh = s[:, sl] if add is None else s[:, sl] + add(hi)
        m_old = m_ref[:, sl]
        m_new = jnp.maximum(m_old, jnp.max(sh, axis=0, keepdims=True))
        scale.append(jnp.exp2(m_old - m_new))
        probs.append(jnp.exp2(sh - m_new).astype(BF16))
        m_ref[:, sl] = m_new
    pv = _dot(vt_ones, jnp.concatenate(probs, axis=1))
    for hi in range(hpg):
        sl = slice(hi * QB, (hi + 1) * QB)
        acc_ref[:, sl] = scale[hi] * acc_ref[:, sl] + pv[:dv, sl]
        l_ref[:, sl] = scale[hi] * l_ref[:, sl] + pv[dv:dv + 1, sl]


def _value_tile_t(v_tile, groups):
    vt = v_tile.astype(F32).T.astype(BF16)
    ones = jnp.ones((ONES_ROWS, v_tile.shape[0]), BF16)
    parts = []
    for g in range(groups):
        parts += [vt[g * HEAD_DIM:(g + 1) * HEAD_DIM, :], ones]
    return jnp.concatenate(parts, axis=0)


def _for_tiles_pairwise(n_tiles, start, finish):
    def pair(i, carry):
        first = start(2 * i)
        second = start(2 * i + 1)
        finish(2 * i, first)
        finish(2 * i + 1, second)
        return carry

    lax.fori_loop(0, lax.shift_right_logical(n_tiles, 1), pair, 0)

    @pl.when((n_tiles & 1) == 1)
    def _():
        finish(n_tiles - 1, start(n_tiles - 1))


def _stack_heads(q, first, count, scale):
    return jnp.concatenate([q[:, (first + hi) * HEAD_DIM:(first + hi + 1) * HEAD_DIM] * scale
                            for hi in range(count)], axis=0)


def _store_heads_t(o_ref, acc_ref, l_ref, first, count, gate_t):
    for hi in range(count):
        h = first + hi
        sl = slice(hi * QB, (hi + 1) * QB)
        o = acc_ref[:, sl] / l_ref[:, sl]
        if gate_t is not None:
            o = o * (1.0 / (1.0 + jnp.exp(-gate_t[h:h + 1, :])))
        o_ref[:, h * HEAD_DIM:(h + 1) * HEAD_DIM] = o.T.astype(o_ref.dtype)


def _band_kernel(*refs, n_tiles, kvh, g, has_sinks, has_gate):
    it = iter(refs)
    q_ref, k_ref, v_ref, bias_ref = next(it), next(it), next(it), next(it)
    sink_ref = next(it) if has_sinks else None
    gate_ref = next(it) if has_gate else None
    o_ref, vt_s, m_s, l_s, acc_s = next(it), next(it), next(it), next(it), next(it)
    n = pl.program_id(1)
    nq = vt_s.shape[0]

    @pl.when(n == 0)
    def _():
        for kt in range(nq):
            vt_s[kt] = _value_tile_t(v_ref[kt * QB:(kt + 1) * QB, :], kvh)

    q = q_ref[...]
    qst = [_stack_heads(q, kv * g, g, HEAD_DIM ** -0.5 * LOG2E) for kv in range(kvh)]
    if has_sinks:
        m_s[...] = sink_ref[...] * LOG2E
        l_s[...] = jnp.ones(l_s.shape, F32)
    else:
        m_s[...] = jnp.full(m_s.shape, NEG_INF, F32)
        l_s[...] = jnp.zeros(l_s.shape, F32)
    acc_s[...] = jnp.zeros(acc_s.shape, F32)
    vrows = HEAD_DIM + ONES_ROWS

    def logits(d):
        k_t = k_ref[pl.ds(pl.multiple_of((n - d) * QB, QB), QB), :]
        return [_dot_nt(k_t[:, kv * HEAD_DIM:(kv + 1) * HEAD_DIM], qst[kv]) for kv in range(kvh)]

    def update(d, s):
        vt_t = vt_s[n - d]
        unbiased = 2 <= d <= n_tiles - 2
        for kv in range(kvh):
            _flash_tile(s[kv], None if unbiased else (lambda hi, kv=kv: bias_ref[kv * g + hi, d]),
                        m_s.at[kv], l_s.at[kv], acc_s.at[kv], vt_t[kv * vrows:(kv + 1) * vrows, :], g)

    @pl.when(n >= n_tiles - 1)
    def _():
        s_next = logits(0)
        for d in range(n_tiles):
            s_cur = s_next
            if d + 1 < n_tiles:
                s_next = logits(d + 1)
            update(d, s_cur)

    @pl.when(n < n_tiles - 1)
    def _():
        for d in range(n_tiles - 1):
            @pl.when(n >= d)
            def _(d=d):
                update(d, logits(d))

    gate_t = gate_ref[...].T if has_gate else None
    for kv in range(kvh):
        _store_heads_t(o_ref, acc_s.at[kv], l_s.at[kv], kv * g, g, gate_t)


def banded_attention(proj, q_col, k_col, v_col, n_heads, kvh, tbl, window, batch, seq,
                     sinks=None, gates=None, gate_col=0):
    g = n_heads // kvh
    kw = kvh * HEAD_DIM
    qw = n_heads * HEAD_DIM
    nq = seq // QB
    n_tiles = window // QB + 1
    args = [proj, proj, proj, _tile_bias_masked_t(tbl, n_tiles, window)]
    in_specs = [pl.BlockSpec((QB, qw), lambda b, n: (b * nq + n, q_col)),
                pl.BlockSpec((seq, kw), lambda b, n: (b, k_col)),
                pl.BlockSpec((seq, kw), lambda b, n: (b, v_col)),
                pl.BlockSpec((n_heads, n_tiles, QB, QB), lambda b, n: (0, 0, 0, 0))]
    if sinks is not None:
        args.append(jnp.repeat((sinks.astype(F32) - _far_bias(tbl)).reshape(kvh, 1, g), QB, axis=2))
        in_specs.append(pl.BlockSpec((kvh, 1, g * QB), lambda b, n: (0, 0, 0)))
    if gates is not None:
        args.append(gates)
        in_specs.append(pl.BlockSpec((QB, LANES), lambda b, n: (b * nq + n, gate_col)))
    return pl.pallas_call(
        functools.partial(_band_kernel, n_tiles=n_tiles, kvh=kvh, g=g,
                          has_sinks=sinks is not None, has_gate=gates is not None),
        out_shape=jax.ShapeDtypeStruct((batch * seq, qw), BF16),
        grid=(batch, nq),
        in_specs=in_specs,
        out_specs=pl.BlockSpec((QB, qw), lambda b, n: (b * nq + n, 0)),
        scratch_shapes=[pltpu.VMEM((nq, kvh * (HEAD_DIM + ONES_ROWS), QB), BF16),
                        pltpu.VMEM((kvh, 1, g * QB), F32), pltpu.VMEM((kvh, 1, g * QB), F32),
                        pltpu.VMEM((kvh, HEAD_DIM, g * QB), F32)],
        compiler_params=_cparams("parallel", "arbitrary"),
        name="banded_attention",
    )(*args)


def _sort_key(x):
    bits = lax.bitcast_convert_type(x, jnp.int32)
    return bits ^ ((bits >> 31) & jnp.int32(0x7FFFFFFF))


def _dsa_kernel(qa_ref, qi_ref, ki_ref, ckv_ref, wi_ref, gain_ref, wuk_ref, wuv_ref, bias_ref,
                o_ref, c_s, ct_s, key_s, m_s, l_s, acc_s, *, topk):
    n = pl.program_id(1)
    nt = n + 1
    nq = c_s.shape[0]

    @pl.when(n == 0)
    def _():
        for kt in range(nq):
            c = ckv_ref[kt * QB:(kt + 1) * QB, :]
            y = c * lax.rsqrt(jnp.mean(c * c, axis=-1, keepdims=True) + LN_EPS) * gain_ref[...]
            c_s[kt] = y.astype(BF16)
            ct_s[kt] = jnp.concatenate([y.T.astype(BF16), jnp.ones((ONES_ROWS, QB), BF16)], axis=0)

    qa = qa_ref[...]
    qi = qi_ref[...]
    qlat = jnp.concatenate(
        [(_dot(qa[:, h * HEAD_DIM:(h + 1) * HEAD_DIM], wuk_ref[h]) * (HEAD_DIM ** -0.5 * LOG2E)).astype(BF16)
         for h in range(A_HEADS)], axis=0)
    qidx = jnp.concatenate(
        [qi[:, h * IDX_DIM:(h + 1) * IDX_DIM] * IDX_DIM ** -0.5 for h in range(IDX_HEADS)], axis=0)
    wi_t = wi_ref[...].T
    wi_row = jnp.concatenate([wi_t[h:h + 1, :] for h in range(IDX_HEADS)], axis=1) * IDX_HEADS ** -0.5

    kpos0 = lax.broadcasted_iota(jnp.int32, (QB, QB), 0)
    qpos = n * QB + lax.broadcasted_iota(jnp.int32, (QB, QB), 1)

    def index_logits(kt):
        k_t = ki_ref[pl.ds(pl.multiple_of(kt * QB, QB), QB), :][:, :IDX_DIM]
        return _dot_nt(k_t, qidx)

    def score_tile(kt, lg):
        act = jnp.maximum(lg, 0.0) * wi_row
        sc = act[:, :QB]
        for h in range(1, IDX_HEADS):
            sc = sc + act[:, h * QB:(h + 1) * QB]
        sc = jnp.where(kt * QB + kpos0 <= qpos, sc, NEG_INF)
        sc = jnp.where(sc == 0.0, 0.0, sc)
        key_s[kt] = _sort_key(sc)

    _for_tiles_pairwise(nt, index_logits, score_tile)

    def count(pred):
        def hits(kt):
            return jnp.where(pred(kt, key_s[kt]), 1.0, 0.0)

        def pair(i, acc):
            return acc + (hits(2 * i) + hits(2 * i + 1))
        acc = lax.fori_loop(0, lax.shift_right_logical(nt, 1), pair, jnp.zeros((QB, QB), F32))
        acc = acc + jnp.where((nt & 1) == 1, hits(nt - 1), 0.0)
        return jnp.sum(acc, axis=0, keepdims=True)

    def thr_step(i, lo):
        cand = lo + (jnp.int32(1) << (31 - i))
        cnt = count(lambda kt, key: key >= cand)
        return jnp.where(cnt >= topk, cand, lo)

    thr = lax.fori_loop(0, 32, thr_step, jnp.full((1, QB), -2 ** 31, jnp.int32))
    need = topk - count(lambda kt, key: key > thr)

    def tie_step(i, j0):
        cand = j0 + (jnp.int32(1) << (10 - i))
        cnt = count(lambda kt, key: (key == thr) & (kt * QB + kpos0 < cand))
        return jnp.where(cnt < need, cand, j0)

    causal_ties = count(lambda kt, key: (key == thr) & (kt * QB + kpos0 <= qpos))
    must_search = jnp.max(jnp.where(causal_ties > need, 1.0, 0.0)) > 0.5
    j0 = lax.cond(must_search,
                  lambda: lax.fori_loop(0, 11, tie_step, jnp.zeros((1, QB), jnp.int32)),
                  lambda: jnp.full((1, QB), nq * QB, jnp.int32))

    m_s[...] = jnp.full(m_s.shape, NEG_INF, F32)
    l_s[...] = jnp.zeros(l_s.shape, F32)
    acc_s[...] = jnp.zeros(acc_s.shape, F32)

    def attend_logits(kt):
        return _dot_nt(c_s[kt], qlat)

    def attend_update(kt, s, d):
        key = key_s[kt]
        kpos = kt * QB + kpos0
        ok = ((key > thr) | ((key == thr) & (kpos <= j0))) & (kpos <= qpos)
        scale, probs = [], []
        for h in range(A_HEADS):
            sl = slice(h * QB, (h + 1) * QB)
            sh = jnp.where(ok, s[:, sl] if d is None else s[:, sl] + bias_ref[h, d], NEG_INF)
            m_old = m_s[:, sl]
            m_new = jnp.maximum(m_old, jnp.max(sh, axis=0, keepdims=True))
            scale.append(jnp.exp2(m_old - m_new))
            probs.append(jnp.where(ok, jnp.exp2(sh - m_new), 0.0).astype(BF16))
            m_s[:, sl] = m_new
        pv = _dot(ct_s[kt], jnp.concatenate(probs, axis=1))
        for h in range(A_HEADS):
            sl = slice(h * QB, (h + 1) * QB)
            acc_s[:, sl] = scale[h] * acc_s[:, sl] + pv[:A_LATENT, sl]
            l_s[:, sl] = scale[h] * l_s[:, sl] + pv[A_LATENT:A_LATENT + 1, sl]

    _for_tiles_pairwise(jnp.maximum(n - 1, 0), attend_logits, lambda kt, s: attend_update(kt, s, None))

    @pl.when(n >= 1)
    def _():
        s_prev = attend_logits(n - 1)
        s_diag = attend_logits(n)
        attend_update(n - 1, s_prev, 1)
        attend_update(n, s_diag, 0)

    @pl.when(n == 0)
    def _():
        attend_update(n, attend_logits(n), 0)

    for h in range(A_HEADS):
        sl = slice(h * QB, (h + 1) * QB)
        o_lat = (acc_s[:, sl] / l_s[:, sl]).T
        o_ref[:, h * HEAD_DIM:(h + 1) * HEAD_DIM] = _dot(o_lat.astype(BF16), wuv_ref[h]).astype(o_ref.dtype)


def dsa_attention(proj, aux, gain, w_uk, w_uv, tbl, batch, seq):
    nq = seq // QB
    topk = min(A_TOPK_MAX, seq // 4)
    qw = A_HEADS * HEAD_DIM
    return pl.pallas_call(
        functools.partial(_dsa_kernel, topk=topk),
        out_shape=jax.ShapeDtypeStruct((batch * seq, qw), BF16),
        grid=(batch, nq),
        in_specs=[
            pl.BlockSpec((QB, qw), lambda b, n: (b * nq + n, 0)),
            pl.BlockSpec((QB, qw), lambda b, n: (b * nq + n, 1)),
            pl.BlockSpec((seq, LANES), lambda b, n: (b, 12)),
            pl.BlockSpec((seq, A_LATENT), lambda b, n: (b, 0)),
            pl.BlockSpec((QB, LANES), lambda b, n: (b * nq + n, 1)),
            pl.BlockSpec((1, A_LATENT), lambda b, n: (0, 0)),
            pl.BlockSpec((A_HEADS, HEAD_DIM, A_LATENT), lambda b, n: (0, 0, 0)),
            pl.BlockSpec((A_HEADS, A_LATENT, HEAD_DIM), lambda b, n: (0, 0, 0)),
            pl.BlockSpec((A_HEADS, 2, QB, QB), lambda b, n: (0, 0, 0, 0)),
        ],
        out_specs=pl.BlockSpec((QB, qw), lambda b, n: (b * nq + n, 0)),
        scratch_shapes=[
            pltpu.VMEM((nq, QB, A_LATENT), BF16),
            pltpu.VMEM((nq, A_LATENT + ONES_ROWS, QB), BF16),
            pltpu.VMEM((nq, QB, QB), jnp.int32),
            pltpu.VMEM((1, A_HEADS * QB), F32),
            pltpu.VMEM((1, A_HEADS * QB), F32),
            pltpu.VMEM((A_LATENT, A_HEADS * QB), F32),
        ],
        compiler_params=_cparams("parallel", "arbitrary"),
        name="dsa_attention",
    )(proj, proj, proj, aux, aux, gain.reshape(1, A_LATENT).astype(F32),
      w_uk.astype(BF16), w_uv.astype(BF16), _tile_bias_masked_t(tbl, 2))


def _compress_kernel(x_ref, pe_ref, w1_ref, w2_ref, o_ref):
    half = CMP_STRIDE * HEAD_DIM
    x = x_ref[0]
    w1 = w1_ref[...]
    u = _dot(x, w1[:half])
    v = _dot(x, w1[half:])
    pre = u + pltpu.roll(v, v.shape[0] - 1, 0) + _dot(pe_ref[...], w1)[:1]
    h = 0.5 * pre * (1.0 + jnp.tanh(math.sqrt(2.0 / math.pi) * (pre + 0.044715 * pre * pre * pre)))
    o_ref[0] = _dot(h.astype(BF16), w2_ref[...]).astype(o_ref.dtype)


def compress_blocks(kv, pe, w1, w2, batch, seq):
    G = C_KV_HEADS
    nch = seq // CMP_STRIDE
    half = CMP_STRIDE * HEAD_DIM
    x = kv.reshape(batch, nch, CMP_STRIDE, G, HEAD_DIM).transpose(0, 3, 1, 2, 4)
    x = x.reshape(batch * G, nch, half)
    pe_flat = jnp.broadcast_to(pe.reshape(1, CMP_BLOCK * HEAD_DIM), (8, CMP_BLOCK * HEAD_DIM))
    return pl.pallas_call(
        _compress_kernel,
        out_shape=jax.ShapeDtypeStruct((batch * G, nch, HEAD_DIM), BF16),
        grid=(batch * G,),
        in_specs=[pl.BlockSpec((1, nch, half), lambda i: (i, 0, 0)),
                  pl.BlockSpec((8, 2 * half), lambda i: (0, 0)),
                  pl.BlockSpec((2 * half, CMP_HIDDEN), lambda i: (0, 0)),
                  pl.BlockSpec((CMP_HIDDEN, HEAD_DIM), lambda i: (0, 0))],
        out_specs=pl.BlockSpec((1, nch, HEAD_DIM), lambda i: (i, 0, 0)),
        compiler_params=_cparams("parallel"),
        name="compress_blocks",
    )(x, pe_flat.astype(BF16), w1.astype(BF16), w2.astype(BF16))


def _cmp_kernel(q_ref, kc_ref, vc_ref, bias_ref, gate_ref, ovt_ref, o_ref, pick_ref, *, n_cmp, n_slc, n_sel):
    n = pl.program_id(1)
    G = C_KV_HEADS
    hpg = C_HEADS // G
    nch = kc_ref.shape[1]
    c = lax.broadcasted_iota(jnp.int32, (nch, QB), 0)
    t = n * QB + lax.broadcasted_iota(jnp.int32, (nch, QB), 1)
    valid = (t - (c * CMP_STRIDE + CMP_BLOCK - 1) >= 0) & (c < n_cmp)
    jj = lax.broadcasted_iota(jnp.int32, (n_slc, QB), 0)
    ts = n * QB + lax.broadcasted_iota(jnp.int32, (n_slc, QB), 1)
    blk = ts // SLC_BLOCK
    bonus = FORCE_BONUS * jnp.where((jj == 0) | (jj == blk) | (jj == blk - 1), 1.0, 0.0)
    admissible = jj * SLC_BLOCK <= ts
    q = q_ref[...]
    gate_t = gate_ref[...].T
    for g in range(G):
        s = _dot_nt(kc_ref[g], _stack_heads(q, g * hpg, hpg, HEAD_DIM ** -0.5))
        vct = vc_ref[g].astype(F32).T.astype(BF16)
        psum = jnp.zeros((nch, QB), F32)
        probs = []
        for hi in range(hpg):
            sl = slice(hi * QB, (hi + 1) * QB)
            sh = jnp.where(valid, s[:, sl] + bias_ref[g * hpg + hi], NEG_INF)
            e = jnp.where(valid, jnp.exp(sh - jnp.max(sh, axis=0, keepdims=True)), 0.0)
            den = jnp.sum(e, axis=0, keepdims=True)
            p = e * (1.0 / jnp.where(den > 0.0, den, 1.0))
            psum = psum + p
            probs.append(p.astype(BF16))
        pv = _dot(vct, jnp.concatenate(probs, axis=1))
        for hi in range(hpg):
            h = g * hpg + hi
            o = pv[:, hi * QB:(hi + 1) * QB] * (1.0 / (1.0 + jnp.exp(-gate_t[h:h + 1, :])))
            o_ref[:, h * HEAD_DIM:(h + 1) * HEAD_DIM] = o.T.astype(o_ref.dtype)
        p_hi = psum.astype(BF16)
        p_lo = (psum - p_hi.astype(F32)).astype(BF16)
        p_slc = _dot(ovt_ref[...], p_hi) + _dot(ovt_ref[...], p_lo)
        score = jnp.where(admissible, p_slc + bonus, NEG_INF)
        rank = jnp.zeros((n_slc, QB), F32)
        for i in range(n_slc):
            si = score[i:i + 1, :]
            rank = rank + jnp.where(si > score, 1.0, 0.0) + jnp.where((si == score) & (i < jj), 1.0, 0.0)
        pick_ref[0, g] = jnp.where(rank < n_sel, 0.0, -MASK_BIG).astype(pick_ref.dtype)


def cmp_attention(proj, kc, vc, gates, tbl, batch, seq):
    nq = seq // QB
    G = C_KV_HEADS
    nch = seq // CMP_STRIDE
    n_cmp = (seq - CMP_BLOCK) // CMP_STRIDE + 1
    n_slc = seq // SLC_BLOCK
    n_sel = min(SLC_COUNT, n_slc)
    qw = C_HEADS * HEAD_DIM
    cmp_end = np.arange(nch) * CMP_STRIDE + CMP_BLOCK - 1
    dist_c = jnp.asarray(np.arange(seq)[None, :] - cmp_end[:, None])
    bias_c = _bias_lookup(tbl, dist_c).transpose(2, 0, 1)
    cs = np.arange(nch)[None, :] * CMP_STRIDE
    ss = np.arange(n_slc)[:, None] * SLC_BLOCK
    overlap_t = ((cs < ss + SLC_BLOCK) & (cs + CMP_BLOCK > ss) & (np.arange(nch)[None, :] < n_cmp))
    return pl.pallas_call(
        functools.partial(_cmp_kernel, n_cmp=n_cmp, n_slc=n_slc, n_sel=n_sel),
        out_shape=(jax.ShapeDtypeStruct((batch * seq, qw), BF16),
                   jax.ShapeDtypeStruct((batch, G, n_slc, seq), BF16)),
        grid=(batch, nq),
        in_specs=[
            pl.BlockSpec((QB, qw), lambda b, n: (b * nq + n, 0)),
            pl.BlockSpec((G, nch, HEAD_DIM), lambda b, n: (b, 0, 0)),
            pl.BlockSpec((G, nch, HEAD_DIM), lambda b, n: (b, 0, 0)),
            pl.BlockSpec((C_HEADS, nch, QB), lambda b, n: (0, 0, n)),
            pl.BlockSpec((QB, LANES), lambda b, n: (b * nq + n, 0)),
            pl.BlockSpec((n_slc, nch), lambda b, n: (0, 0)),
        ],
        out_specs=(pl.BlockSpec((QB, qw), lambda b, n: (b * nq + n, 0)),
                   pl.BlockSpec((1, G, n_slc, QB), lambda b, n: (b, 0, 0, n))),
        compiler_params=_cparams("parallel", "parallel"),
        name="cmp_attention",
    )(proj, kc, vc, bias_c, gates, jnp.asarray(overlap_t.astype(np.float32)).astype(BF16))


def _slc_kernel(q_ref, k_ref, v_ref, pick_ref, bias_ref, gate_ref, o_ref, vt_s, m_s, l_s, acc_s, *, n_slc):
    n = pl.program_id(1)
    G = C_KV_HEADS
    hpg = C_HEADS // G
    nq = vt_s.shape[0]

    @pl.when(n == 0)
    def _():
        for kt in range(nq):
            vt_s[kt] = _value_tile_t(v_ref[kt * QB:(kt + 1) * QB, :], G)

    q = q_ref[...].astype(F32)
    rhs = []
    for g in range(G):
        qt = jnp.concatenate(
            [(q[:, (g * hpg + hi) * HEAD_DIM:(g * hpg + hi + 1) * HEAD_DIM].T * (HEAD_DIM ** -0.5 * LOG2E)).astype(BF16)
             for hi in range(hpg)], axis=1)
        rhs.append(jnp.concatenate([qt, jnp.concatenate([pick_ref[0, g]] * hpg, axis=1)], axis=0))
    ej = lax.broadcasted_iota(jnp.int32, (QB, n_slc), 0)
    eb = lax.broadcasted_iota(jnp.int32, (QB, n_slc), 1)
    m_s[...] = jnp.full(m_s.shape, NEG_INF, F32)
    l_s[...] = jnp.zeros(l_s.shape, F32)
    acc_s[...] = jnp.zeros(acc_s.shape, F32)
    vrows = HEAD_DIM + ONES_ROWS

    def logits(kt):
        k_t = k_ref[pl.ds(pl.multiple_of(kt * QB, QB), QB), :]
        expand = jnp.where(eb == (kt * QB + ej) // SLC_BLOCK, 1.0, 0.0).astype(BF16)
        return [_dot(jnp.concatenate([k_t[:, g * HEAD_DIM:(g + 1) * HEAD_DIM], expand], axis=1), rhs[g])
                for g in range(G)]

    def update(kt, s, d):
        vt_t = vt_s[kt]
        for g in range(G):
            _flash_tile(s[g], None if d is None else (lambda hi, g=g: bias_ref[g * hpg + hi, d]),
                        m_s.at[g], l_s.at[g], acc_s.at[g], vt_t[g * vrows:(g + 1) * vrows, :], hpg)

    _for_tiles_pairwise(jnp.maximum(n - 1, 0), logits, lambda kt, s: update(kt, s, None))

    @pl.when(n >= 1)
    def _():
        s_prev = logits(n - 1)
        s_diag = logits(n)
        update(n - 1, s_prev, 1)
        update(n, s_diag, 0)

    @pl.when(n == 0)
    def _():
        update(n, logits(n), 0)

    gate_t = gate_ref[...].T
    for g in range(G):
        _store_heads_t(o_ref, acc_s.at[g], l_s.at[g], g * hpg, hpg, gate_t)


def slc_attention(proj, pick, gates, tbl, batch, seq):
    nq = seq // QB
    G = C_KV_HEADS
    n_slc = seq // SLC_BLOCK
    qw = C_HEADS * HEAD_DIM
    return pl.pallas_call(
        functools.partial(_slc_kernel, n_slc=n_slc),
        out_shape=jax.ShapeDtypeStruct((batch * seq, qw), BF16),
        grid=(batch, nq),
        in_specs=[
            pl.BlockSpec((QB, qw), lambda b, n: (b * nq + n, 0)),
            pl.BlockSpec((seq, LANES), lambda b, n: (b, 10)),
            pl.BlockSpec((seq, LANES), lambda b, n: (b, 11)),
            pl.BlockSpec((1, G, n_slc, QB), lambda b, n: (b, 0, 0, n)),
            pl.BlockSpec((C_HEADS, 2, QB, QB), lambda b, n: (0, 0, 0, 0)),
            pl.BlockSpec((QB, LANES), lambda b, n: (b * nq + n, 1)),
        ],
        out_specs=pl.BlockSpec((QB, qw), lambda b, n: (b * nq + n, 0)),
        scratch_shapes=[pltpu.VMEM((nq, G * (HEAD_DIM + ONES_ROWS), QB), BF16),
                        pltpu.VMEM((G, 1, C_HEADS // G * QB), F32),
                        pltpu.VMEM((G, 1, C_HEADS // G * QB), F32),
                        pltpu.VMEM((G, HEAD_DIM, C_HEADS // G * QB), F32)],
        compiler_params=_cparams("parallel", "arbitrary"),
        name="slc_attention",
    )(proj, proj, proj, pick, _tile_bias_masked_t(tbl, 2), gates)


def _xattn_kernel(h_ref, wq_ref, k_ref, v_ref, wo_ref, g_ref, b_ref, o_ref):
    x = h_ref[...]
    q = _dot(x.astype(BF16), wq_ref[...]).astype(BF16)
    k = k_ref[...]
    v = v_ref[...]
    outs = []
    for hd in range(X_HEADS):
        sl = slice(hd * X_HEAD_DIM, (hd + 1) * X_HEAD_DIM)
        lg = _dot_nt(q[:, sl], k[:, sl]) * X_HEAD_DIM ** -0.5
        e = jnp.exp(lg - jnp.max(lg, axis=-1, keepdims=True))
        den = jnp.sum(e, axis=-1, keepdims=True)
        outs.append((_dot(e.astype(BF16), v[:, sl]) / den).astype(BF16))
    o = jnp.concatenate(outs, axis=-1)
    y = ALPHA * x + _dot(o, wo_ref[...])
    o_ref[...] = _layer_norm_rows(y, g_ref[...], b_ref[...])


def cross_attention_ln(h, mem, w_q, w_k, w_v, w_o, g, b, batch, seq, tm=256):
    M = mem.shape[0] // batch
    XW = X_HEADS * X_HEAD_DIM
    D = D_MODEL
    kv = matmul(mem, jnp.concatenate([w_k, w_v], axis=1).astype(BF16), BF16, tm=512)
    nt = seq // tm
    fixed = lambda bb, i: (0, 0)
    return pl.pallas_call(
        _xattn_kernel,
        out_shape=jax.ShapeDtypeStruct((batch * seq, D), F32),
        grid=(batch, nt),
        in_specs=[pl.BlockSpec((tm, D), lambda bb, i: (bb * nt + i, 0)),
                  pl.BlockSpec((D, XW), fixed),
                  pl.BlockSpec((M, XW), lambda bb, i: (bb, 0)),
                  pl.BlockSpec((M, XW), lambda bb, i: (bb, 1)),
                  pl.BlockSpec((XW, D), fixed),
                  pl.BlockSpec((1, D), fixed), pl.BlockSpec((1, D), fixed)],
        out_specs=pl.BlockSpec((tm, D), lambda bb, i: (bb * nt + i, 0)),
        compiler_params=_cparams("parallel", "parallel"),
        name="cross_attention_ln",
    )(h, w_q.astype(BF16), kv, kv, w_o.astype(BF16), g.reshape(1, D), b.reshape(1, D))


def _silu(a):
    return a * (1.0 / (1.0 + jnp.exp(-a)))


def _swiglu_kernel(*refs, n_cast):
    x_ref, wg_ref, wu_ref, wd_ref, g_ref, b_ref = refs[:6]
    cast_in = refs[6:6 + n_cast]
    o_ref = refs[6 + n_cast]
    cast_out = refs[7 + n_cast:]
    x = x_ref[...]
    xb = x.astype(BF16)
    hmid = (_silu(_dot(xb, wg_ref[...])) * _dot(xb, wu_ref[...])).astype(BF16)
    y = ALPHA * x + _dot(hmid, wd_ref[...])
    o_ref[...] = _layer_norm_rows(y, g_ref[...], b_ref[...])
    for src, dst in zip(cast_in, cast_out):
        dst[...] = src[...].astype(dst.dtype)


def swiglu_ln(h, w_gate, w_up, w_down, g, b, cast_also=(), tm=256):
    T, D = h.shape
    F = w_gate.shape[1]
    steps = T // tm
    fixed = lambda i: (0, 0)
    row = lambda i: (i, 0)
    once = pl.Buffered(1)
    flat = [w.reshape(-1, w.shape[-1]) for w in cast_also]
    slabs = [pl.BlockSpec((w.shape[0] // steps, w.shape[1]), row) for w in flat]
    outs = pl.pallas_call(
        functools.partial(_swiglu_kernel, n_cast=len(flat)),
        out_shape=[jax.ShapeDtypeStruct((T, D), F32)] + [jax.ShapeDtypeStruct(w.shape, BF16) for w in flat],
        grid=(steps,),
        in_specs=[pl.BlockSpec((tm, D), row),
                  pl.BlockSpec((D, F), fixed, pipeline_mode=once),
                  pl.BlockSpec((D, F), fixed, pipeline_mode=once),
                  pl.BlockSpec((F, D), fixed, pipeline_mode=once),
                  pl.BlockSpec((1, D), fixed), pl.BlockSpec((1, D), fixed)] + slabs,
        out_specs=[pl.BlockSpec((tm, D), row)] + slabs,
        compiler_params=_cparams("parallel"),
        name="swiglu_ln",
    )(h, cast_bf16(w_gate), cast_bf16(w_up), cast_bf16(w_down), g.reshape(1, D), b.reshape(1, D), *flat)
    return outs[0], [o.reshape(w.shape) for o, w in zip(outs[1:], cast_also)]


def _router_kernel(x_ref, w_ref, idx_ref, wgt_ref, xlo_ref, xhi_ref):
    x = x_ref[...]
    w = w_ref[...]
    x_hi = x.astype(BF16)
    x_lo = (x - x_hi.astype(F32)).astype(BF16)
    w_hi = w.astype(BF16)
    w_lo = (w - w_hi.astype(F32)).astype(BF16)
    lg = _dot(x_hi, w_hi) + (_dot(x_hi, w_lo) + _dot(x_lo, w_hi))
    e_iota = lax.broadcasted_iota(jnp.int32, lg.shape, 1).astype(F32)
    m1 = jnp.max(lg, axis=-1, keepdims=True)
    i1 = jnp.min(jnp.where(lg == m1, e_iota, float(N_EXPERTS)), axis=-1, keepdims=True)
    first = e_iota == i1
    rest = jnp.where(first, -jnp.inf, lg)
    m2 = jnp.max(rest, axis=-1, keepdims=True)
    i2 = jnp.min(jnp.where(rest == m2, e_iota, float(N_EXPERTS)), axis=-1, keepdims=True)
    w2 = jnp.exp(m2 - m1)
    den = 1.0 + w2
    k_iota = lax.broadcasted_iota(jnp.int32, idx_ref.shape, 1)
    idx_ref[...] = jnp.where(k_iota == 0, i1, i2).astype(jnp.int32)
    wgt_ref[...] = jnp.where(k_iota == 0, 1.0 / den, w2 / den)
    half = x.shape[1] // 2
    xlo_ref[...] = _pack_bf16_pairs(x[:, :half])
    xhi_ref[...] = _pack_bf16_pairs(x[:, half:])


def _pack_bf16_pairs(x):
    w = x.shape[1] // 2
    lo = lax.bitcast_convert_type(x[:, :w].astype(BF16).astype(F32), jnp.uint32)
    hi = lax.bitcast_convert_type(x[:, w:].astype(BF16).astype(F32), jnp.uint32)
    return (lo >> 16) | (hi & jnp.uint32(0xFFFF0000))


def _unpack_bf16_pairs(words):
    lo = lax.bitcast_convert_type(words << 16, F32).astype(BF16)
    hi = lax.bitcast_convert_type(words & jnp.uint32(0xFFFF0000), F32).astype(BF16)
    return jnp.concatenate([lo, hi], axis=1)


def moe_route(h, w_router, tm=1024):
    T, D = h.shape
    E = w_router.shape[1]
    row = lambda i: (i, 0)
    return pl.pallas_call(
        _router_kernel,
        out_shape=(jax.ShapeDtypeStruct((T, TOP_K), jnp.int32), jax.ShapeDtypeStruct((T, TOP_K), F32),
                   jax.ShapeDtypeStruct((T, D // 4), jnp.uint32), jax.ShapeDtypeStruct((T, D // 4), jnp.uint32)),
        grid=(T // tm,),
        in_specs=[pl.BlockSpec((tm, D), row), pl.BlockSpec((D, E), lambda i: (0, 0))],
        out_specs=(pl.BlockSpec((tm, TOP_K), row), pl.BlockSpec((tm, TOP_K), row),
                   pl.BlockSpec((tm, D // 4), row), pl.BlockSpec((tm, D // 4), row)),
        compiler_params=_cparams("parallel"),
        name="moe_route",
    )(h, w_router)


def sc_gather_rows(xs, idx, window):
    n_idx = idx.shape[0]
    n_arr = len(xs)
    mesh = plsc.VectorSubcoreMesh(core_axis_name="core", subcore_axis_name="subcore")

    @pl.kernel(out_type=[jax.ShapeDtypeStruct((n_idx, x.shape[1]), x.dtype) for x in xs], mesh=mesh,
               scratch_types=[], name="sc_gather_rows")
    def gather(*refs):
        x_hbms, i_hbm, o_hbms = refs[:n_arr], refs[n_arr], refs[n_arr + 1:]
        for x_hbm, o_hbm in zip(x_hbms, o_hbms):
            def body(i_vmem, o_vmem, x_hbm=x_hbm):
                pltpu.sync_copy(x_hbm.at[i_vmem.at[0]], o_vmem)

            pltpu.emit_pipeline(
                body,
                grid=(n_idx // window,),
                in_specs=[pl.BlockSpec((1, window), lambda i: (0, i))],
                out_specs=[pl.BlockSpec((window, x_hbm.shape[1]), lambda i: (i, 0))],
                core_axis_name=("core", "subcore"),
                dimension_semantics=(pltpu.PARALLEL,),
            )(i_hbm, o_hbm)

    return gather(*xs, idx.reshape(1, n_idx))


def _moe_plan(top_idx, tm):
    T = top_idx.shape[0]
    n_asg = T * TOP_K
    n_tiles = n_asg // tm + N_EXPERTS
    flat_e = top_idx.reshape(n_asg)
    onehot = (flat_e[:, None] == jnp.arange(N_EXPERTS, dtype=jnp.int32)[None, :]).astype(jnp.int32)
    incl = jnp.cumsum(onehot, axis=0)
    counts = incl[-1]
    rank = jnp.sum((incl - onehot) * onehot, axis=1)
    tiles_e = (counts + tm - 1) // tm
    tile_end = jnp.cumsum(tiles_e)
    row_start_p = (tile_end - tiles_e) * tm
    row_start = jnp.cumsum(counts) - counts
    pos = (jnp.sum(onehot * row_start_p[None, :], axis=1) + rank).reshape(T, TOP_K).T.reshape(n_asg)
    tile_ids = jnp.arange(n_tiles, dtype=jnp.int32)
    tile_expert = jnp.minimum(jnp.searchsorted(tile_end, tile_ids, side="right"), N_EXPERTS - 1).astype(jnp.int32)
    n_rows = n_tiles * tm
    sorted_tok = jnp.argsort(flat_e, stable=True).astype(jnp.int32) // TOP_K
    window_src = jnp.concatenate([jnp.zeros((n_rows,), jnp.int32), sorted_tok, jnp.zeros((n_rows,), jnp.int32)])
    r = jnp.arange(n_rows, dtype=jnp.int32)
    src_tok = r % T
    for e in range(N_EXPERTS):
        shifted = lax.dynamic_slice(window_src, (n_rows - (row_start_p[e] - row_start[e]),), (n_rows,))
        in_run = (r >= row_start_p[e]) & (r < row_start_p[e] + counts[e])
        src_tok = jnp.where(in_run, shifted, src_tok)
    return src_tok, pos.astype(jnp.int32), tile_expert, tile_end[-1:].astype(jnp.int32)


def _moe_ffn_kernel(te_ref, nu_ref, xlo_ref, xhi_ref, wg_ref, wu_ref, wd_ref, olo_ref, ohi_ref, acc_s):
    i = pl.program_id(0)
    c = pl.program_id(1)
    last = pl.num_programs(1) - 1
    used = i < nu_ref[0]
    half = acc_s.shape[1] // 2

    @pl.when(used & (c == 0))
    def _():
        acc_s[...] = jnp.zeros(acc_s.shape, F32)

    @pl.when(used)
    def _():
        x = jnp.concatenate([_unpack_bf16_pairs(xlo_ref[...]), _unpack_bf16_pairs(xhi_ref[...])], axis=1)
        hmid = _silu(_dot(x, wg_ref[0])) * _dot(x, wu_ref[0])
        acc_s[...] += _dot(hmid.astype(BF16), wd_ref[0])

    @pl.when(used & (c == last))
    def _():
        olo_ref[...] = _pack_bf16_pairs(acc_s[:, :half])
        ohi_ref[...] = _pack_bf16_pairs(acc_s[:, half:])

    @pl.when(jnp.logical_not(used) & (c == last))
    def _():
        olo_ref[...] = jnp.zeros(olo_ref.shape, olo_ref.dtype)
        ohi_ref[...] = jnp.zeros(ohi_ref.shape, ohi_ref.dtype)


def moe_expert_ffn(xs_lo, xs_hi, tile_expert, n_used, w_gate, w_up, w_down, tm, fc=1792):
    R, half = xs_lo.shape
    D = 4 * half
    E, _, F = w_gate.shape
    rows = lambda i, c, te, nu: (i, 0)
    return pl.pallas_call(
        _moe_ffn_kernel,
        out_shape=(jax.ShapeDtypeStruct((R, half), jnp.uint32), jax.ShapeDtypeStruct((R, half), jnp.uint32)),
        grid_spec=pltpu.PrefetchScalarGridSpec(
            num_scalar_prefetch=2,
            grid=(R // tm, F // fc),
            in_specs=[pl.BlockSpec((tm, half), rows), pl.BlockSpec((tm, half), rows),
                      pl.BlockSpec((1, D, fc), lambda i, c, te, nu: (te[i], 0, c)),
                      pl.BlockSpec((1, D, fc), lambda i, c, te, nu: (te[i], 0, c)),
                      pl.BlockSpec((1, fc, D), lambda i, c, te, nu: (te[i], c, 0))],
            out_specs=(pl.BlockSpec((tm, half), rows), pl.BlockSpec((tm, half), rows)),
            scratch_shapes=[pltpu.VMEM((tm, D), F32)]),
        compiler_params=_cparams("parallel", "arbitrary"),
        name="moe_expert_ffn",
    )(tile_expert, n_used, xs_lo, xs_hi, w_gate, w_up, w_down)


def _moe_combine_kernel(h_ref, lo0_ref, hi0_ref, lo1_ref, hi1_ref, w_ref, g_ref, b_ref, o_ref):
    w = w_ref[...]
    y0 = jnp.concatenate([_unpack_bf16_pairs(lo0_ref[...]), _unpack_bf16_pairs(hi0_ref[...])], axis=1).astype(F32)
    y1 = jnp.concatenate([_unpack_bf16_pairs(lo1_ref[...]), _unpack_bf16_pairs(hi1_ref[...])], axis=1).astype(F32)
    ff = w[:, 0:1] * y0 + w[:, 1:2] * y1
    o_ref[...] = _layer_norm_rows(ALPHA * h_ref[...] + ff, g_ref[...], b_ref[...])


def moe_combine_ln(h, y_lo, y_hi, top_w, g, b, tm=512):
    T, D = h.shape
    half = D // 4
    nt = T // tm
    row = lambda i: (i, 0)
    second = lambda i: (nt + i, 0)
    fixed = lambda i: (0, 0)
    return pl.pallas_call(
        _moe_combine_kernel,
        out_shape=jax.ShapeDtypeStruct((T, D), F32),
        grid=(nt,),
        in_specs=[pl.BlockSpec((tm, D), row),
                  pl.BlockSpec((tm, half), row), pl.BlockSpec((tm, half), row),
                  pl.BlockSpec((tm, half), second), pl.BlockSpec((tm, half), second),
                  pl.BlockSpec((tm, TOP_K), row),
                  pl.BlockSpec((1, D), fixed), pl.BlockSpec((1, D), fixed)],
        out_specs=pl.BlockSpec((tm, D), row),
        compiler_params=_cparams("parallel"),
        name="moe_combine_ln",
    )(h, y_lo, y_hi, y_lo, y_hi, top_w, g.reshape(1, D), b.reshape(1, D))


def moe_ln(h, w_router, w_gate, w_up, w_down, g, b, tm=512):
    top_idx, top_w, hb_lo, hb_hi = moe_route(h, w_router)
    src_tok, pos, tile_expert, n_used = _moe_plan(top_idx, tm)
    xs_lo, xs_hi = sc_gather_rows([hb_lo, hb_hi], src_tok, SC_GATHER_WINDOW)
    ys_lo, ys_hi = moe_expert_ffn(xs_lo, xs_hi, tile_expert, n_used, w_gate, w_up, w_down, tm)
    y_lo, y_hi = sc_gather_rows([ys_lo, ys_hi], pos, SC_GATHER_WINDOW)
    return moe_combine_ln(h, y_lo, y_hi, top_w, g, b)


def _pad_cols(w, width):
    return jnp.pad(w, ((0, 0), (0, width - w.shape[1])))


def even_layer_mixer(h, w_in, ckv_gain, w_uk, w_uv, sinks, w_out, rel_table, ln_g, ln_b, batch, seq):
    cuts = np.cumsum((A_HEADS * HEAD_DIM, A_LATENT, IDX_HEADS * IDX_DIM, IDX_DIM, IDX_HEADS,
                      B_HEADS * HEAD_DIM, B_KV_HEADS * HEAD_DIM))
    w_qa, w_ckv, w_qi, w_ki, w_wi, w_qb, w_kb, w_vb = jnp.split(w_in, [int(c) for c in cuts], axis=1)
    w_main = jnp.concatenate([w_qa, w_qi, w_qb, _pad_cols(w_ki, LANES), w_kb, w_vb], axis=1).astype(BF16)
    w_aux = jnp.concatenate([w_ckv, _pad_cols(w_wi, LANES)], axis=1).astype(BF16)
    proj, aux = matmul_pair(h, w_main, w_aux)
    o_a = dsa_attention(proj, aux, ckv_gain, w_uk, w_uv, rel_table[:, :A_HEADS], batch, seq)
    o_b = banded_attention(proj, 2, 13, 14, B_HEADS, B_KV_HEADS,
                           rel_table[:, A_HEADS:A_HEADS + B_HEADS], B_WINDOW,
                           batch, seq, sinks=sinks)
    return matmul_residual_ln([o_a, o_b], w_out.astype(BF16), h, ln_g, ln_b, concat=True)


def odd_layer_mixer(h, w_in, pe_k, w1_k, w2_k, pe_v, w1_v, w2_v, w_out, rel_table, ln_g, ln_b, batch, seq):
    kvw = C_KV_HEADS * HEAD_DIM
    qw = C_HEADS * HEAD_DIM
    w_main = w_in[:, :qw + 6 * kvw].astype(BF16)
    w_gl = w_in[:, qw + 6 * kvw:].reshape(D_MODEL, C_HEADS, 3)
    w_gate = jnp.concatenate([_pad_cols(w_gl[:, :, r], LANES) for r in range(3)], axis=1).astype(BF16)
    proj, gates = matmul_pair(h, w_main, w_gate)
    tbl = rel_table[:, :C_HEADS]
    kc = compress_blocks(proj[:, qw:qw + kvw], pe_k, w1_k, w2_k, batch, seq)
    vc = compress_blocks(proj[:, qw + kvw:qw + 2 * kvw], pe_v, w1_v, w2_v, batch, seq)
    o_cmp, pick = cmp_attention(proj, kc, vc, gates, tbl, batch, seq)
    o_slc = slc_attention(proj, pick, gates, tbl, batch, seq)
    o_win = banded_attention(proj, 0, 12, 13, C_HEADS, C_KV_HEADS, tbl, C_WINDOW,
                             batch, seq, gates=gates, gate_col=2)
    return matmul_residual_ln([o_cmp, o_slc, o_win], w_out.astype(BF16), h, ln_g, ln_b)


def kernel(x, mem, rel_table, ev_w_in, ev_ckv_gain, ev_w_uk, ev_w_uv, ev_sinks, ev_w_out, od_w_in, od_pe_k, od_w1_k, od_w2_k, od_pe_v, od_w1_v, od_w2_v, od_w_out, xa_w_q, xa_w_k, xa_w_v, xa_w_o, ff_w_gate, ff_w_up, ff_w_down, moe_w_router, moe_w_gate, moe_w_up, moe_w_down, ln_g, ln_b):
    batch, seq, D = x.shape
    h = x.reshape(batch * seq, D)
    mem2 = mem.reshape(-1, D)
    for i in range(DEPTH):
        j = i // 2
        if i % 2 == 0:
            h = even_layer_mixer(h, ev_w_in[j], ev_ckv_gain[j], ev_w_uk[j], ev_w_uv[j], ev_sinks[j],
                                 ev_w_out[j], rel_table, ln_g[i, 0], ln_b[i, 0], batch, seq)
        else:
            h = odd_layer_mixer(h, od_w_in[j], od_pe_k[j], od_w1_k[j], od_w2_k[j], od_pe_v[j],
                                od_w1_v[j], od_w2_v[j], od_w_out[j], rel_table, ln_g[i, 0], ln_b[i, 0],
                                batch, seq)
        h = cross_attention_ln(h, mem2, xa_w_q[i], xa_w_k[i], xa_w_v[i], xa_w_o[i],
                               ln_g[i, 1], ln_b[i, 1], batch, seq)
        if i % 2 == 0:
            riders = (moe_w_gate[j], moe_w_up[j], moe_w_down[j]) if i + 1 < DEPTH else ()
            h, moe_bf16 = swiglu_ln(h, ff_w_gate[j], ff_w_up[j], ff_w_down[j], ln_g[i, 2], ln_b[i, 2], riders)
        else:
            h = moe_ln(h, moe_w_router[j], *moe_bf16, ln_g[i, 2], ln_b[i, 2])
    return h.reshape(batch, seq, D)
```

```python
import functools
import math

import numpy as np
import jax
import jax.numpy as jnp
from jax import lax
from jax.experimental import pallas as pl
from jax.experimental.pallas import tpu as pltpu
from jax.experimental.pallas import tpu_sc as plsc

D_MODEL = 1024
DEPTH = 2
HEAD_DIM = 64
N_SLOTS = D_MODEL // HEAD_DIM
A_HEADS = 8
A_LATENT = 128
IDX_HEADS = 8
IDX_DIM = 64
A_TOPK_MAX = 256
B_HEADS = 8
B_KV_HEADS = 2
B_WINDOW = 128
C_HEADS = 16
C_KV_HEADS = 2
CMP_BLOCK = 32
CMP_STRIDE = 16
CMP_HIDDEN = 128
SLC_BLOCK = 64
SLC_COUNT = 8
C_WINDOW = 512
X_HEADS = 4
X_HEAD_DIM = 128
D_FF = 2816
N_EXPERTS = 8
TOP_K = 2
D_FF_EXPERT = 3584
REL_BUCKETS = 32
REL_MAX_DIST = 128
LN_EPS = 1e-5
NEG_INF = -1e30
FORCE_BONUS = 1e6
ALPHA = (2 * DEPTH) ** 0.25
MASK_BIG = 2.0 ** 100
LOG2E = math.log2(math.e)
ONES_ROWS = 16

LANES = 128
QB = 128
VMEM_LIMIT_BYTES = 56 * 1024 * 1024
SC_GATHER_WINDOW = 128

F32 = jnp.float32
BF16 = jnp.bfloat16


def _cparams(*sem):
    return pltpu.CompilerParams(dimension_semantics=sem, vmem_limit_bytes=VMEM_LIMIT_BYTES)


def _dot(a, b):
    return jnp.dot(a, b, preferred_element_type=F32)


def _dot_nt(a, b):
    return lax.dot_general(a, b, (((1,), (1,)), ((), ())), preferred_element_type=F32)


def _layer_norm_rows(y, g, b):
    mu = jnp.mean(y, axis=-1, keepdims=True)
    yc = y - mu
    var = jnp.mean(yc * yc, axis=-1, keepdims=True)
    return yc * lax.rsqrt(var + LN_EPS) * g + b


def _mm_kernel(x_ref, w_ref, o_ref):
    o_ref[...] = _dot(x_ref[...].astype(BF16), w_ref[...]).astype(o_ref.dtype)


def matmul(x, w, out_dtype, tm=512):
    M, K = x.shape
    N = w.shape[1]
    tm = min(tm, M)
    return pl.pallas_call(
        _mm_kernel,
        out_shape=jax.ShapeDtypeStruct((M, N), out_dtype),
        grid=(M // tm,),
        in_specs=[pl.BlockSpec((tm, K), lambda i: (i, 0)),
                  pl.BlockSpec((K, N), lambda i: (0, 0))],
        out_specs=pl.BlockSpec((tm, N), lambda i: (i, 0)),
        compiler_params=_cparams("parallel"),
        name="matmul",
    )(x, w)


def _mm_pair_kernel(x_ref, w_ref, waux_ref, o_ref, oaux_ref):
    xb = x_ref[...].astype(BF16)
    o_ref[...] = _dot(xb, w_ref[...]).astype(o_ref.dtype)
    oaux_ref[...] = _dot(xb, waux_ref[...])


def matmul_pair(x, w, w_aux, tm=512):
    M, K = x.shape
    N, NA = w.shape[1], w_aux.shape[1]
    row = lambda i: (i, 0)
    fixed = lambda i: (0, 0)
    return pl.pallas_call(
        _mm_pair_kernel,
        out_shape=(jax.ShapeDtypeStruct((M, N), BF16), jax.ShapeDtypeStruct((M, NA), F32)),
        grid=(M // tm,),
        in_specs=[pl.BlockSpec((tm, K), row), pl.BlockSpec((K, N), fixed), pl.BlockSpec((K, NA), fixed)],
        out_specs=(pl.BlockSpec((tm, N), row), pl.BlockSpec((tm, NA), row)),
        compiler_params=_cparams("parallel"),
        name="matmul_pair",
    )(x, w, w_aux)


def _cast_kernel(*refs):
    o_ref = refs[-1]
    cw = refs[0].shape[1]
    for j, x_ref in enumerate(refs[:-1]):
        o_ref[:, j * cw:(j + 1) * cw] = x_ref[...].astype(o_ref.dtype)


def cast_bf16(w, block_bytes=8 * 1024 * 1024):
    shape = w.shape
    C = shape[-1]
    M = int(np.prod(shape[:-1]))
    tm = M
    while tm * C * 4 > block_bytes and tm % 32 == 0:
        tm //= 2
    n_slab = next(k for k in (4, 2, 1) if C % (k * LANES) == 0)
    cw = C // n_slab
    w2 = w.reshape(M, C)
    out = pl.pallas_call(
        _cast_kernel,
        out_shape=jax.ShapeDtypeStruct((M, C), BF16),
        grid=(M // tm,),
        in_specs=[pl.BlockSpec((tm, cw), lambda i, j=j: (i, j)) for j in range(n_slab)],
        out_specs=pl.BlockSpec((tm, C), lambda i: (i, 0)),
        compiler_params=_cparams("parallel"),
        name="cast_bf16",
    )(*([w2] * n_slab))
    return out.reshape(shape)


def _mm_res_ln_kernel(*refs, n_in, concat):
    a_refs = refs[:n_in]
    w_ref, h_ref, g_ref, b_ref, o_ref = refs[n_in:]
    if concat:
        k0 = 0
        mix = None
        for r in a_refs:
            part = _dot(r[...], w_ref[k0:k0 + r.shape[1], :])
            mix = part if mix is None else mix + part
            k0 += r.shape[1]
    else:
        a = a_refs[0][...]
        if n_in > 1:
            a = a.astype(F32)
            for r in a_refs[1:]:
                a = a + r[...].astype(F32)
        mix = _dot(a.astype(BF16), w_ref[...])
    y = ALPHA * h_ref[...] + mix
    o_ref[...] = _layer_norm_rows(y, g_ref[...], b_ref[...])


def matmul_residual_ln(a_list, w, h, g, b, concat=False, tm=512):
    M = a_list[0].shape[0]
    K, N = w.shape
    n_in = len(a_list)
    row = lambda i: (i, 0)
    fixed = lambda i: (0, 0)
    return pl.pallas_call(
        functools.partial(_mm_res_ln_kernel, n_in=n_in, concat=concat),
        out_shape=jax.ShapeDtypeStruct((M, N), F32),
        grid=(M // tm,),
        in_specs=[pl.BlockSpec((tm, a.shape[1]), row) for a in a_list] + [
            pl.BlockSpec((K, N), fixed), pl.BlockSpec((tm, N), row),
            pl.BlockSpec((1, N), fixed), pl.BlockSpec((1, N), fixed)],
        out_specs=pl.BlockSpec((tm, N), row),
        compiler_params=_cparams("parallel"),
        name="matmul_residual_ln",
    )(*a_list, w, h, g.reshape(1, N), b.reshape(1, N))


def _rel_bucket(dist):
    n = jnp.maximum(dist, 0)
    max_exact = REL_BUCKETS // 2
    nf = jnp.maximum(n, 1).astype(F32)
    log_b = max_exact + (jnp.log(nf / max_exact) / math.log(REL_MAX_DIST / max_exact)
                         * (REL_BUCKETS - max_exact)).astype(jnp.int32)
    return jnp.where(n < max_exact, n, jnp.minimum(log_b, REL_BUCKETS - 1))


def _bias_lookup(tbl, dist):
    onehot = (_rel_bucket(dist)[..., None] == jnp.arange(REL_BUCKETS, dtype=jnp.int32)).astype(F32)
    return jnp.einsum("...r,rh->...h", onehot, tbl.astype(F32), precision=lax.Precision.HIGHEST)


def _tile_bias_masked_t(tbl, n_tiles, window=None):
    j = np.arange(QB)[:, None]
    i = np.arange(QB)[None, :]
    dist = np.stack([d * QB + i - j for d in range(n_tiles)])
    valid = dist >= 0
    if window is not None:
        valid &= dist < window
    rel = _bias_lookup(tbl, jnp.asarray(dist)) - _far_bias(tbl)
    return jnp.where(jnp.asarray(valid)[..., None], rel * LOG2E, NEG_INF).transpose(3, 0, 1, 2)


def _far_bias(tbl):
    return tbl.astype(F32)[REL_BUCKETS - 1]


def _flash_tile(s, add, m_ref, l_ref, acc_ref, vt_ones, hpg):
    dv = acc_ref.shape[0]
    scale, probs = [], []
    for hi in range(hpg):
        sl = slice(hi * QB, (hi + 1) * QB)
        sh = s[:, sl] if add is None else s[:, sl] + add(hi)
        m_old = m_ref[:, sl]
        m_new = jnp.maximum(m_old, jnp.max(sh, axis=0, keepdims=True))
        scale.append(jnp.exp2(m_old - m_new))
        probs.append(jnp.exp2(sh - m_new).astype(BF16))
        m_ref[:, sl] = m_new
    pv = _dot(vt_ones, jnp.concatenate(probs, axis=1))
    for hi in range(hpg):
        sl = slice(hi * QB, (hi + 1) * QB)
        acc_ref[:, sl] = scale[hi] * acc_ref[:, sl] + pv[:dv, sl]
        l_ref[:, sl] = scale[hi] * l_ref[:, sl] + pv[dv:dv + 1, sl]


def _value_tile_t(v_tile, groups):
    vt = v_tile.astype(F32).T.astype(BF16)
    ones = jnp.ones((ONES_ROWS, v_tile.shape[0]), BF16)
    parts = []
    for g in range(groups):
        parts += [vt[g * HEAD_DIM:(g + 1) * HEAD_DIM, :], ones]
    return jnp.concatenate(parts, axis=0)


def _for_tiles_pairwise(n_tiles, start, finish):
    def pair(i, carry):
        first = start(2 * i)
        second = start(2 * i + 1)
        finish(2 * i, first)
        finish(2 * i + 1, second)
        return carry

    lax.fori_loop(0, lax.shift_right_logical(n_tiles, 1), pair, 0)

    @pl.when((n_tiles & 1) == 1)
    def _():
        finish(n_tiles - 1, start(n_tiles - 1))


def _stack_heads(q, first, count, scale):
    return jnp.concatenate([q[:, (first + hi) * HEAD_DIM:(first + hi + 1) * HEAD_DIM] * scale
                            for hi in range(count)], axis=0)


def _store_heads_t(o_ref, acc_ref, l_ref, first, count, gate_t):
    for hi in range(count):
        h = first + hi
        sl = slice(hi * QB, (hi + 1) * QB)
        o = acc_ref[:, sl] / l_ref[:, sl]
        if gate_t is not None:
            o = o * (1.0 / (1.0 + jnp.exp(-gate_t[h:h + 1, :])))
        o_ref[:, h * HEAD_DIM:(h + 1) * HEAD_DIM] = o.T.astype(o_ref.dtype)


def _band_kernel(*refs, n_tiles, kvh, g, has_sinks, has_gate):
    it = iter(refs)
    q_ref, k_ref, v_ref, bias_ref = next(it), next(it), next(it), next(it)
    sink_ref = next(it) if has_sinks else None
    gate_ref = next(it) if has_gate else None
    o_ref, vt_s, m_s, l_s, acc_s = next(it), next(it), next(it), next(it), next(it)
    n = pl.program_id(1)
    nq = vt_s.shape[0]

    @pl.when(n == 0)
    def _():
        for kt in range(nq):
            vt_s[kt] = _value_tile_t(v_ref[kt * QB:(kt + 1) * QB, :], kvh)

    q = q_ref[...]
    qst = [_stack_heads(q, kv * g, g, HEAD_DIM ** -0.5 * LOG2E) for kv in range(kvh)]
    if has_sinks:
        m_s[...] = sink_ref[...] * LOG2E
        l_s[...] = jnp.ones(l_s.shape, F32)
    else:
        m_s[...] = jnp.full(m_s.shape, NEG_INF, F32)
        l_s[...] = jnp.zeros(l_s.shape, F32)
    acc_s[...] = jnp.zeros(acc_s.shape, F32)
    vrows = HEAD_DIM + ONES_ROWS

    def logits(d):
        k_t = k_ref[pl.ds(pl.multiple_of((n - d) * QB, QB), QB), :]
        return [_dot_nt(k_t[:, kv * HEAD_DIM:(kv + 1) * HEAD_DIM], qst[kv]) for kv in range(kvh)]

    def update(d, s):
        vt_t = vt_s[n - d]
        unbiased = 2 <= d <= n_tiles - 2
        for kv in range(kvh):
            _flash_tile(s[kv], None if unbiased else (lambda hi, kv=kv: bias_ref[kv * g + hi, d]),
                        m_s.at[kv], l_s.at[kv], acc_s.at[kv], vt_t[kv * vrows:(kv + 1) * vrows, :], g)

    @pl.when(n >= n_tiles - 1)
    def _():
        s_next = logits(0)
        for d in range(n_tiles):
            s_cur = s_next
            if d + 1 < n_tiles:
                s_next = logits(d + 1)
            update(d, s_cur)

    @pl.when(n < n_tiles - 1)
    def _():
        for d in range(n_tiles - 1):
            @pl.when(n >= d)
            def _(d=d):
                update(d, logits(d))

    gate_t = gate_ref[...].T if has_gate else None
    for kv in range(kvh):
        _store_heads_t(o_ref, acc_s.at[kv], l_s.at[kv], kv * g, g, gate_t)


def banded_attention(proj, q_col, k_col, v_col, n_heads, kvh, tbl, window, batch, seq,
                     sinks=None, gates=None, gate_col=0):
    g = n_heads // kvh
    kw = kvh * HEAD_DIM
    qw = n_heads * HEAD_DIM
    nq = seq // QB
    n_tiles = window // QB + 1
    args = [proj, proj, proj, _tile_bias_masked_t(tbl, n_tiles, window)]
    in_specs = [pl.BlockSpec((QB, qw), lambda b, n: (b * nq + n, q_col)),
                pl.BlockSpec((seq, kw), lambda b, n: (b, k_col)),
                pl.BlockSpec((seq, kw), lambda b, n: (b, v_col)),
                pl.BlockSpec((n_heads, n_tiles, QB, QB), lambda b, n: (0, 0, 0, 0))]
    if sinks is not None:
        args.append(jnp.repeat((sinks.astype(F32) - _far_bias(tbl)).reshape(kvh, 1, g), QB, axis=2))
        in_specs.append(pl.BlockSpec((kvh, 1, g * QB), lambda b, n: (0, 0, 0)))
    if gates is not None:
        args.append(gates)
        in_specs.append(pl.BlockSpec((QB, LANES), lambda b, n: (b * nq + n, gate_col)))
    return pl.pallas_call(
        functools.partial(_band_kernel, n_tiles=n_tiles, kvh=kvh, g=g,
                          has_sinks=sinks is not None, has_gate=gates is not None),
        out_shape=jax.ShapeDtypeStruct((batch * seq, qw), BF16),
        grid=(batch, nq),
        in_specs=in_specs,
        out_specs=pl.BlockSpec((QB, qw), lambda b, n: (b * nq + n, 0)),
        scratch_shapes=[pltpu.VMEM((nq, kvh * (HEAD_DIM + ONES_ROWS), QB), BF16),
                        pltpu.VMEM((kvh, 1, g * QB), F32), pltpu.VMEM((kvh, 1, g * QB), F32),
                        pltpu.VMEM((kvh, HEAD_DIM, g * QB), F32)],
        compiler_params=_cparams("parallel", "arbitrary"),
        name="banded_attention",
    )(*args)


def _sort_key(x):
    bits = lax.bitcast_convert_type(x, jnp.int32)
    return bits ^ ((bits >> 31) & jnp.int32(0x7FFFFFFF))


def _dsa_kernel(qa_ref, qi_ref, ki_ref, ckv_ref, wi_ref, gain_ref, wuk_ref, wuv_ref, bias_ref,
                o_ref, c_s, ct_s, key_s, m_s, l_s, acc_s, *, topk):
    n = pl.program_id(1)
    nt = n + 1
    nq = c_s.shape[0]

    @pl.when(n == 0)
    def _():
        for kt in range(nq):
            c = ckv_ref[kt * QB:(kt + 1) * QB, :]
            y = c * lax.rsqrt(jnp.mean(c * c, axis=-1, keepdims=True) + LN_EPS) * gain_ref[...]
            c_s[kt] = y.astype(BF16)
            ct_s[kt] = jnp.concatenate([y.T.astype(BF16), jnp.ones((ONES_ROWS, QB), BF16)], axis=0)

    qa = qa_ref[...]
    qi = qi_ref[...]
    qlat = jnp.concatenate(
        [(_dot(qa[:, h * HEAD_DIM:(h + 1) * HEAD_DIM], wuk_ref[h]) * (HEAD_DIM ** -0.5 * LOG2E)).astype(BF16)
         for h in range(A_HEADS)], axis=0)
    qidx = jnp.concatenate(
        [qi[:, h * IDX_DIM:(h + 1) * IDX_DIM] * IDX_DIM ** -0.5 for h in range(IDX_HEADS)], axis=0)
    wi_t = wi_ref[...].T
    wi_row = jnp.concatenate([wi_t[h:h + 1, :] for h in range(IDX_HEADS)], axis=1) * IDX_HEADS ** -0.5

    kpos0 = lax.broadcasted_iota(jnp.int32, (QB, QB), 0)
    qpos = n * QB + lax.broadcasted_iota(jnp.int32, (QB, QB), 1)

    def index_logits(kt):
        k_t = ki_ref[pl.ds(pl.multiple_of(kt * QB, QB), QB), :][:, :IDX_DIM]
        return _dot_nt(k_t, qidx)

    def score_tile(kt, lg):
        act = jnp.maximum(lg, 0.0) * wi_row
        sc = act[:, :QB]
        for h in range(1, IDX_HEADS):
            sc = sc + act[:, h * QB:(h + 1) * QB]
        sc = jnp.where(kt * QB + kpos0 <= qpos, sc, NEG_INF)
        sc = jnp.where(sc == 0.0, 0.0, sc)
        key_s[kt] = _sort_key(sc)

    _for_tiles_pairwise(nt, index_logits, score_tile)

    def count(pred):
        def hits(kt):
            return jnp.where(pred(kt, key_s[kt]), 1.0, 0.0)

        def pair(i, acc):
            return acc + (hits(2 * i) + hits(2 * i + 1))
        acc = lax.fori_loop(0, lax.shift_right_logical(nt, 1), pair, jnp.zeros((QB, QB), F32))
        acc = acc + jnp.where((nt & 1) == 1, hits(nt - 1), 0.0)
        return jnp.sum(acc, axis=0, keepdims=True)

    def thr_step(i, lo):
        cand = lo + (jnp.int32(1) << (31 - i))
        cnt = count(lambda kt, key: key >= cand)
        return jnp.where(cnt >= topk, cand, lo)

    thr = lax.fori_loop(0, 32, thr_step, jnp.full((1, QB), -2 ** 31, jnp.int32))
    need = topk - count(lambda kt, key: key > thr)

    def tie_step(i, j0):
        cand = j0 + (jnp.int32(1) << (10 - i))
        cnt = count(lambda kt, key: (key == thr) & (kt * QB + kpos0 < cand))
        return jnp.where(cnt < need, cand, j0)

    causal_ties = count(lambda kt, key: (key == thr) & (kt * QB + kpos0 <= qpos))
    must_search = jnp.max(jnp.where(causal_ties > need, 1.0, 0.0)) > 0.5
    j0 = lax.cond(must_search,
                  lambda: lax.fori_loop(0, 11, tie_step, jnp.zeros((1, QB), jnp.int32)),
                  lambda: jnp.full((1, QB), nq * QB, jnp.int32))

    m_s[...] = jnp.full(m_s.shape, NEG_INF, F32)
    l_s[...] = jnp.zeros(l_s.shape, F32)
    acc_s[...] = jnp.zeros(acc_s.shape, F32)

    def attend_logits(kt):
        return _dot_nt(c_s[kt], qlat)

    def attend_update(kt, s, d):
        key = key_s[kt]
        kpos = kt * QB + kpos0
        ok = ((key > thr) | ((key == thr) & (kpos <= j0))) & (kpos <= qpos)
        scale, probs = [], []
        for h in range(A_HEADS):
            sl = slice(h * QB, (h + 1) * QB)
            sh = jnp.where(ok, s[:, sl] if d is None else s[:, sl] + bias_ref[h, d], NEG_INF)
            m_old = m_s[:, sl]
            m_new = jnp.maximum(m_old, jnp.max(sh, axis=0, keepdims=True))
            scale.append(jnp.exp2(m_old - m_new))
            probs.append(jnp.where(ok, jnp.exp2(sh - m_new), 0.0).astype(BF16))
            m_s[:, sl] = m_new
        pv = _dot(ct_s[kt], jnp.concatenate(probs, axis=1))
        for h in range(A_HEADS):
            sl = slice(h * QB, (h + 1) * QB)
            acc_s[:, sl] = scale[h] * acc_s[:, sl] + pv[:A_LATENT, sl]
            l_s[:, sl] = scale[h] * l_s[:, sl] + pv[A_LATENT:A_LATENT + 1, sl]

    _for_tiles_pairwise(jnp.maximum(n - 1, 0), attend_logits, lambda kt, s: attend_update(kt, s, None))

    @pl.when(n >= 1)
    def _():
        s_prev = attend_logits(n - 1)
        s_diag = attend_logits(n)
        attend_update(n - 1, s_prev, 1)
        attend_update(n, s_diag, 0)

    @pl.when(n == 0)
    def _():
        attend_update(n, attend_logits(n), 0)

    for h in range(A_HEADS):
        sl = slice(h * QB, (h + 1) * QB)
        o_lat = (acc_s[:, sl] / l_s[:, sl]).T
        o_ref[:, h * HEAD_DIM:(h + 1) * HEAD_DIM] = _dot(o_lat.astype(BF16), wuv_ref[h]).astype(o_ref.dtype)


def dsa_attention(proj, aux, gain, w_uk, w_uv, tbl, batch, seq):
    nq = seq // QB
    topk = min(A_TOPK_MAX, seq // 4)
    qw = A_HEADS * HEAD_DIM
    return pl.pallas_call(
        functools.partial(_dsa_kernel, topk=topk),
        out_shape=jax.ShapeDtypeStruct((batch * seq, qw), BF16),
        grid=(batch, nq),
        in_specs=[
            pl.BlockSpec((QB, qw), lambda b, n: (b * nq + n, 0)),
            pl.BlockSpec((QB, qw), lambda b, n: (b * nq + n, 1)),
            pl.BlockSpec((seq, LANES), lambda b, n: (b, 12)),
            pl.BlockSpec((seq, A_LATENT), lambda b, n: (b, 0)),
            pl.BlockSpec((QB, LANES), lambda b, n: (b * nq + n, 1)),
            pl.BlockSpec((1, A_LATENT), lambda b, n: (0, 0)),
            pl.BlockSpec((A_HEADS, HEAD_DIM, A_LATENT), lambda b, n: (0, 0, 0)),
            pl.BlockSpec((A_HEADS, A_LATENT, HEAD_DIM), lambda b, n: (0, 0, 0)),
            pl.BlockSpec((A_HEADS, 2, QB, QB), lambda b, n: (0, 0, 0, 0)),
        ],
        out_specs=pl.BlockSpec((QB, qw), lambda b, n: (b * nq + n, 0)),
        scratch_shapes=[
            pltpu.VMEM((nq, QB, A_LATENT), BF16),
            pltpu.VMEM((nq, A_LATENT + ONES_ROWS, QB), BF16),
            pltpu.VMEM((nq, QB, QB), jnp.int32),
            pltpu.VMEM((1, A_HEADS * QB), F32),
            pltpu.VMEM((1, A_HEADS * QB), F32),
            pltpu.VMEM((A_LATENT, A_HEADS * QB), F32),
        ],
        compiler_params=_cparams("parallel", "arbitrary"),
        name="dsa_attention",
    )(proj, proj, proj, aux, aux, gain.reshape(1, A_LATENT).astype(F32),
      w_uk.astype(BF16), w_uv.astype(BF16), _tile_bias_masked_t(tbl, 2))


def _compress_kernel(kv_ref, pe_ref, w1_ref, w2_ref, o_ref, x_s):
    G = o_ref.shape[0]
    nch = o_ref.shape[1]
    half = CMP_STRIDE * HEAD_DIM
    x_s[...] = kv_ref[...].astype(F32)
    w1 = w1_ref[...]
    pe_term = _dot(pe_ref[...], w1)[:1]
    u = [jnp.zeros((nch, CMP_HIDDEN), F32) for _ in range(G)]
    v = [jnp.zeros((nch, CMP_HIDDEN), F32) for _ in range(G)]
    for l in range(CMP_STRIDE):
        rows = x_s[pl.ds(l, nch, stride=CMP_STRIDE), :].astype(BF16)
        for g in range(G):
            rows_g = rows[:, g * HEAD_DIM:(g + 1) * HEAD_DIM]
            u[g] = u[g] + _dot(rows_g, w1[l * HEAD_DIM:(l + 1) * HEAD_DIM])
            v[g] = v[g] + _dot(rows_g, w1[half + l * HEAD_DIM:half + (l + 1) * HEAD_DIM])
    for g in range(G):
        pre = u[g] + pltpu.roll(v[g], nch - 1, 0) + pe_term
        h = 0.5 * pre * (1.0 + jnp.tanh(math.sqrt(2.0 / math.pi) * (pre + 0.044715 * pre * pre * pre)))
        o_ref[g] = _dot(h.astype(BF16), w2_ref[...]).astype(o_ref.dtype)


def compress_blocks(proj, col, pe, w1, w2, batch, seq):
    G = C_KV_HEADS
    nch = seq // CMP_STRIDE
    half = CMP_STRIDE * HEAD_DIM
    pe_flat = jnp.broadcast_to(pe.reshape(1, CMP_BLOCK * HEAD_DIM), (8, CMP_BLOCK * HEAD_DIM))
    return pl.pallas_call(
        _compress_kernel,
        out_shape=jax.ShapeDtypeStruct((batch * G, nch, HEAD_DIM), BF16),
        grid=(batch,),
        in_specs=[pl.BlockSpec((seq, G * HEAD_DIM), lambda i: (i, col)),
                  pl.BlockSpec((8, 2 * half), lambda i: (0, 0)),
                  pl.BlockSpec((2 * half, CMP_HIDDEN), lambda i: (0, 0)),
                  pl.BlockSpec((CMP_HIDDEN, HEAD_DIM), lambda i: (0, 0))],
        out_specs=pl.BlockSpec((G, nch, HEAD_DIM), lambda i: (i, 0, 0)),
        scratch_shapes=[pltpu.VMEM((seq, G * HEAD_DIM), F32)],
        compiler_params=_cparams("parallel"),
        name="compress_blocks",
    )(proj, pe_flat.astype(BF16), w1.astype(BF16), w2.astype(BF16))


def _cmp_kernel(q_ref, kc_ref, vc_ref, bias_ref, gate_ref, ovt_ref, o_ref, pick_ref, *, n_cmp, n_slc, n_sel):
    n = pl.program_id(1)
    G = C_KV_HEADS
    hpg = C_HEADS // G
    nch = kc_ref.shape[1]
    c = lax.broadcasted_iota(jnp.int32, (nch, QB), 0)
    t = n * QB + lax.broadcasted_iota(jnp.int32, (nch, QB), 1)
    valid = (t - (c * CMP_STRIDE + CMP_BLOCK - 1) >= 0) & (c < n_cmp)
    jj = lax.broadcasted_iota(jnp.int32, (n_slc, QB), 0)
    ts = n * QB + lax.broadcasted_iota(jnp.int32, (n_slc, QB), 1)
    blk = ts // SLC_BLOCK
    bonus = FORCE_BONUS * jnp.where((jj == 0) | (jj == blk) | (jj == blk - 1), 1.0, 0.0)
    admissible = jj * SLC_BLOCK <= ts
    q = q_ref[...]
    gate_t = gate_ref[...].T
    for g in range(G):
        s = _dot_nt(kc_ref[g], _stack_heads(q, g * hpg, hpg, HEAD_DIM ** -0.5))
        vct = vc_ref[g].astype(F32).T.astype(BF16)
        psum = jnp.zeros((nch, QB), F32)
        probs = []
        for hi in range(hpg):
            sl = slice(hi * QB, (hi + 1) * QB)
            sh = jnp.where(valid, s[:, sl] + bias_ref[g * hpg + hi], NEG_INF)
            e = jnp.where(valid, jnp.exp(sh - jnp.max(sh, axis=0, keepdims=True)), 0.0)
            den = jnp.sum(e, axis=0, keepdims=True)
            p = e * (1.0 / jnp.where(den > 0.0, den, 1.0))
            psum = psum + p
            probs.append(p.astype(BF16))
        pv = _dot(vct, jnp.concatenate(probs, axis=1))
        for hi in range(hpg):
            h = g * hpg + hi
            o = pv[:, hi * QB:(hi + 1) * QB] * (1.0 / (1.0 + jnp.exp(-gate_t[h:h + 1, :])))
            o_ref[:, h * HEAD_DIM:(h + 1) * HEAD_DIM] = o.T.astype(o_ref.dtype)
        p_hi = psum.astype(BF16)
        p_lo = (psum - p_hi.astype(F32)).astype(BF16)
        p_slc = _dot(ovt_ref[...], p_hi) + _dot(ovt_ref[...], p_lo)
        score = jnp.where(admissible, p_slc + bonus, NEG_INF)
        rank = jnp.zeros((n_slc, QB), F32)
        for i in range(n_slc):
            si = score[i:i + 1, :]
            rank = rank + jnp.where(si > score, 1.0, 0.0) + jnp.where((si == score) & (i < jj), 1.0, 0.0)
        pick_ref[0, g] = jnp.where(rank < n_sel, 0.0, -MASK_BIG).astype(pick_ref.dtype)


def cmp_attention(proj, kc, vc, gates, tbl, batch, seq):
    nq = seq // QB
    G = C_KV_HEADS
    nch = seq // CMP_STRIDE
    n_cmp = (seq - CMP_BLOCK) // CMP_STRIDE + 1
    n_slc = seq // SLC_BLOCK
    n_sel = min(SLC_COUNT, n_slc)
    qw = C_HEADS * HEAD_DIM
    cmp_end = np.arange(nch) * CMP_STRIDE + CMP_BLOCK - 1
    dist_c = jnp.asarray(np.arange(seq)[None, :] - cmp_end[:, None])
    bias_c = _bias_lookup(tbl, dist_c).transpose(2, 0, 1)
    cs = np.arange(nch)[None, :] * CMP_STRIDE
    ss = np.arange(n_slc)[:, None] * SLC_BLOCK
    overlap_t = ((cs < ss + SLC_BLOCK) & (cs + CMP_BLOCK > ss) & (np.arange(nch)[None, :] < n_cmp))
    return pl.pallas_call(
        functools.partial(_cmp_kernel, n_cmp=n_cmp, n_slc=n_slc, n_sel=n_sel),
        out_shape=(jax.ShapeDtypeStruct((batch * seq, qw), BF16),
                   jax.ShapeDtypeStruct((batch, G, n_slc, seq), BF16)),
        grid=(batch, nq),
        in_specs=[
            pl.BlockSpec((QB, qw), lambda b, n: (b * nq + n, 0)),
            pl.BlockSpec((G, nch, HEAD_DIM), lambda b, n: (b, 0, 0)),
            pl.BlockSpec((G, nch, HEAD_DIM), lambda b, n: (b, 0, 0)),
            pl.BlockSpec((C_HEADS, nch, QB), lambda b, n: (0, 0, n)),
            pl.BlockSpec((QB, LANES), lambda b, n: (b * nq + n, 0)),
            pl.BlockSpec((n_slc, nch), lambda b, n: (0, 0)),
        ],
        out_specs=(pl.BlockSpec((QB, qw), lambda b, n: (b * nq + n, 0)),
                   pl.BlockSpec((1, G, n_slc, QB), lambda b, n: (b, 0, 0, n))),
        compiler_params=_cparams("parallel", "parallel"),
        name="cmp_attention",
    )(proj, kc, vc, bias_c, gates, jnp.asarray(overlap_t.astype(np.float32)).astype(BF16))


def _slc_kernel(q_ref, k_ref, v_ref, pick_ref, bias_ref, gate_ref, o_ref, vt_s, m_s, l_s, acc_s, *, n_slc):
    n = pl.program_id(1)
    G = C_KV_HEADS
    hpg = C_HEADS // G
    nq = vt_s.shape[0]

    @pl.when(n == 0)
    def _():
        for kt in range(nq):
            vt_s[kt] = _value_tile_t(v_ref[kt * QB:(kt + 1) * QB, :], G)

    q = q_ref[...].astype(F32)
    rhs = []
    for g in range(G):
        qt = jnp.concatenate(
            [(q[:, (g * hpg + hi) * HEAD_DIM:(g * hpg + hi + 1) * HEAD_DIM].T * (HEAD_DIM ** -0.5 * LOG2E)).astype(BF16)
             for hi in range(hpg)], axis=1)
        rhs.append(jnp.concatenate([qt, jnp.concatenate([pick_ref[0, g]] * hpg, axis=1)], axis=0))
    ej = lax.broadcasted_iota(jnp.int32, (QB, n_slc), 0)
    eb = lax.broadcasted_iota(jnp.int32, (QB, n_slc), 1)
    m_s[...] = jnp.full(m_s.shape, NEG_INF, F32)
    l_s[...] = jnp.zeros(l_s.shape, F32)
    acc_s[...] = jnp.zeros(acc_s.shape, F32)
    vrows = HEAD_DIM + ONES_ROWS

    def logits(kt):
        k_t = k_ref[pl.ds(pl.multiple_of(kt * QB, QB), QB), :]
        expand = jnp.where(eb == (kt * QB + ej) // SLC_BLOCK, 1.0, 0.0).astype(BF16)
        return [_dot(jnp.concatenate([k_t[:, g * HEAD_DIM:(g + 1) * HEAD_DIM], expand], axis=1), rhs[g])
                for g in range(G)]

    def update(kt, s, d):
        vt_t = vt_s[kt]
        for g in range(G):
            _flash_tile(s[g], None if d is None else (lambda hi, g=g: bias_ref[g * hpg + hi, d]),
                        m_s.at[g], l_s.at[g], acc_s.at[g], vt_t[g * vrows:(g + 1) * vrows, :], hpg)

    _for_tiles_pairwise(jnp.maximum(n - 1, 0), logits, lambda kt, s: update(kt, s, None))

    @pl.when(n >= 1)
    def _():
        s_prev = logits(n - 1)
        s_diag = logits(n)
        update(n - 1, s_prev, 1)
        update(n, s_diag, 0)

    @pl.when(n == 0)
    def _():
        update(n, logits(n), 0)

    gate_t = gate_ref[...].T
    for g in range(G):
        _store_heads_t(o_ref, acc_s.at[g], l_s.at[g], g * hpg, hpg, gate_t)


def slc_attention(proj, pick, gates, tbl, batch, seq):
    nq = seq // QB
    G = C_KV_HEADS
    n_slc = seq // SLC_BLOCK
    qw = C_HEADS * HEAD_DIM
    return pl.pallas_call(
        functools.partial(_slc_kernel, n_slc=n_slc),
        out_shape=jax.ShapeDtypeStruct((batch * seq, qw), BF16),
        grid=(batch, nq),
        in_specs=[
            pl.BlockSpec((QB, qw), lambda b, n: (b * nq + n, 0)),
            pl.BlockSpec((seq, LANES), lambda b, n: (b, 10)),
            pl.BlockSpec((seq, LANES), lambda b, n: (b, 11)),
            pl.BlockSpec((1, G, n_slc, QB), lambda b, n: (b, 0, 0, n)),
            pl.BlockSpec((C_HEADS, 2, QB, QB), lambda b, n: (0, 0, 0, 0)),
            pl.BlockSpec((QB, LANES), lambda b, n: (b * nq + n, 1)),
        ],
        out_specs=pl.BlockSpec((QB, qw), lambda b, n: (b * nq + n, 0)),
        scratch_shapes=[pltpu.VMEM((nq, G * (HEAD_DIM + ONES_ROWS), QB), BF16),
                        pltpu.VMEM((G, 1, C_HEADS // G * QB), F32),
                        pltpu.VMEM((G, 1, C_HEADS // G * QB), F32),
                        pltpu.VMEM((G, HEAD_DIM, C_HEADS // G * QB), F32)],
        compiler_params=_cparams("parallel", "arbitrary"),
        name="slc_attention",
    )(proj, proj, proj, pick, _tile_bias_masked_t(tbl, 2), gates)


def _xattn_kernel(h_ref, wq_ref, k_ref, v_ref, wo_ref, g_ref, b_ref, o_ref):
    x = h_ref[...]
    q = _dot(x.astype(BF16), wq_ref[...]).astype(BF16)
    k = k_ref[...]
    v = v_ref[...]
    outs = []
    for hd in range(X_HEADS):
        sl = slice(hd * X_HEAD_DIM, (hd + 1) * X_HEAD_DIM)
        lg = _dot_nt(q[:, sl], k[:, sl]) * X_HEAD_DIM ** -0.5
        e = jnp.exp(lg - jnp.max(lg, axis=-1, keepdims=True))
        den = jnp.sum(e, axis=-1, keepdims=True)
        outs.append((_dot(e.astype(BF16), v[:, sl]) / den).astype(BF16))
    o = jnp.concatenate(outs, axis=-1)
    y = ALPHA * x + _dot(o, wo_ref[...])
    o_ref[...] = _layer_norm_rows(y, g_ref[...], b_ref[...])


def cross_attention_ln(h, mem, w_q, w_k, w_v, w_o, g, b, batch, seq, tm=256):
    M = mem.shape[0] // batch
    XW = X_HEADS * X_HEAD_DIM
    D = D_MODEL
    kv = matmul(mem, jnp.concatenate([w_k, w_v], axis=1).astype(BF16), BF16, tm=512)
    nt = seq // tm
    fixed = lambda bb, i: (0, 0)
    return pl.pallas_call(
        _xattn_kernel,
        out_shape=jax.ShapeDtypeStruct((batch * seq, D), F32),
        grid=(batch, nt),
        in_specs=[pl.BlockSpec((tm, D), lambda bb, i: (bb * nt + i, 0)),
                  pl.BlockSpec((D, XW), fixed),
                  pl.BlockSpec((M, XW), lambda bb, i: (bb, 0)),
                  pl.BlockSpec((M, XW), lambda bb, i: (bb, 1)),
                  pl.BlockSpec((XW, D), fixed),
                  pl.BlockSpec((1, D), fixed), pl.BlockSpec((1, D), fixed)],
        out_specs=pl.BlockSpec((tm, D), lambda bb, i: (bb * nt + i, 0)),
        compiler_params=_cparams("parallel", "parallel"),
        name="cross_attention_ln",
    )(h, w_q.astype(BF16), kv, kv, w_o.astype(BF16), g.reshape(1, D), b.reshape(1, D))


def _silu(a):
    return a * (1.0 / (1.0 + jnp.exp(-a)))


def _swiglu_kernel(*refs, n_cast):
    x_ref, wg_ref, wu_ref, wd_ref, g_ref, b_ref = refs[:6]
    cast_in = refs[6:6 + n_cast]
    o_ref = refs[6 + n_cast]
    cast_out = refs[7 + n_cast:]
    x = x_ref[...]
    xb = x.astype(BF16)
    hmid = (_silu(_dot(xb, wg_ref[...])) * _dot(xb, wu_ref[...])).astype(BF16)
    y = ALPHA * x + _dot(hmid, wd_ref[...])
    o_ref[...] = _layer_norm_rows(y, g_ref[...], b_ref[...])
    for src, dst in zip(cast_in, cast_out):
        dst[...] = src[...].astype(dst.dtype)


def swiglu_ln(h, w_gate, w_up, w_down, g, b, cast_also=(), tm=256):
    T, D = h.shape
    F = w_gate.shape[1]
    steps = T // tm
    fixed = lambda i: (0, 0)
    row = lambda i: (i, 0)
    once = pl.Buffered(1)
    flat = [w.reshape(-1, w.shape[-1]) for w in cast_also]
    slabs = [pl.BlockSpec((w.shape[0] // steps, w.shape[1]), row) for w in flat]
    outs = pl.pallas_call(
        functools.partial(_swiglu_kernel, n_cast=len(flat)),
        out_shape=[jax.ShapeDtypeStruct((T, D), F32)] + [jax.ShapeDtypeStruct(w.shape, BF16) for w in flat],
        grid=(steps,),
        in_specs=[pl.BlockSpec((tm, D), row),
                  pl.BlockSpec((D, F), fixed, pipeline_mode=once),
                  pl.BlockSpec((D, F), fixed, pipeline_mode=once),
                  pl.BlockSpec((F, D), fixed, pipeline_mode=once),
                  pl.BlockSpec((1, D), fixed), pl.BlockSpec((1, D), fixed)] + slabs,
        out_specs=[pl.BlockSpec((tm, D), row)] + slabs,
        compiler_params=_cparams("parallel"),
        name="swiglu_ln",
    )(h, cast_bf16(w_gate), cast_bf16(w_up), cast_bf16(w_down), g.reshape(1, D), b.reshape(1, D), *flat)
    return outs[0], [o.reshape(w.shape) for o, w in zip(outs[1:], cast_also)]


def _router_kernel(x_ref, w_ref, idx_ref, wgt_ref, xlo_ref, xhi_ref):
    x = x_ref[...]
    w = w_ref[...]
    x_hi = x.astype(BF16)
    x_lo = (x - x_hi.astype(F32)).astype(BF16)
    w_hi = w.astype(BF16)
    w_lo = (w - w_hi.astype(F32)).astype(BF16)
    lg = _dot(x_hi, w_hi) + (_dot(x_hi, w_lo) + _dot(x_lo, w_hi))
    e_iota = lax.broadcasted_iota(jnp.int32, lg.shape, 1).astype(F32)
    m1 = jnp.max(lg, axis=-1, keepdims=True)
    i1 = jnp.min(jnp.where(lg == m1, e_iota, float(N_EXPERTS)), axis=-1, keepdims=True)
    first = e_iota == i1
    rest = jnp.where(first, -jnp.inf, lg)
    m2 = jnp.max(rest, axis=-1, keepdims=True)
    i2 = jnp.min(jnp.where(rest == m2, e_iota, float(N_EXPERTS)), axis=-1, keepdims=True)
    w2 = jnp.exp(m2 - m1)
    den = 1.0 + w2
    k_iota = lax.broadcasted_iota(jnp.int32, idx_ref.shape, 1)
    idx_ref[...] = jnp.where(k_iota == 0, i1, i2).astype(jnp.int32)
    wgt_ref[...] = jnp.where(k_iota == 0, 1.0 / den, w2 / den)
    half = x.shape[1] // 2
    xlo_ref[...] = _pack_bf16_pairs(x[:, :half])
    xhi_ref[...] = _pack_bf16_pairs(x[:, half:])


def _pack_bf16_pairs(x):
    w = x.shape[1] // 2
    lo = lax.bitcast_convert_type(x[:, :w].astype(BF16).astype(F32), jnp.uint32)
    hi = lax.bitcast_convert_type(x[:, w:].astype(BF16).astype(F32), jnp.uint32)
    return (lo >> 16) | (hi & jnp.uint32(0xFFFF0000))


def _unpack_bf16_pairs(words):
    lo = lax.bitcast_convert_type(words << 16, F32).astype(BF16)
    hi = lax.bitcast_convert_type(words & jnp.uint32(0xFFFF0000), F32).astype(BF16)
    return jnp.concatenate([lo, hi], axis=1)


def moe_route(h, w_router, tm=1024):
    T, D = h.shape
    E = w_router.shape[1]
    row = lambda i: (i, 0)
    return pl.pallas_call(
        _router_kernel,
        out_shape=(jax.ShapeDtypeStruct((T, TOP_K), jnp.int32), jax.ShapeDtypeStruct((T, TOP_K), F32),
                   jax.ShapeDtypeStruct((T, D // 4), jnp.uint32), jax.ShapeDtypeStruct((T, D // 4), jnp.uint32)),
        grid=(T // tm,),
        in_specs=[pl.BlockSpec((tm, D), row), pl.BlockSpec((D, E), lambda i: (0, 0))],
        out_specs=(pl.BlockSpec((tm, TOP_K), row), pl.BlockSpec((tm, TOP_K), row),
                   pl.BlockSpec((tm, D // 4), row), pl.BlockSpec((tm, D // 4), row)),
        compiler_params=_cparams("parallel"),
        name="moe_route",
    )(h, w_router)


def sc_gather_rows(xs, idx, window):
    n_idx = idx.shape[0]
    n_arr = len(xs)
    mesh = plsc.VectorSubcoreMesh(core_axis_name="core", subcore_axis_name="subcore")

    @pl.kernel(out_type=[jax.ShapeDtypeStruct((n_idx, x.shape[1]), x.dtype) for x in xs], mesh=mesh,
               scratch_types=[], name="sc_gather_rows")
    def gather(*refs):
        x_hbms, i_hbm, o_hbms = refs[:n_arr], refs[n_arr], refs[n_arr + 1:]
        for x_hbm, o_hbm in zip(x_hbms, o_hbms):
            def body(i_vmem, o_vmem, x_hbm=x_hbm):
                pltpu.sync_copy(x_hbm.at[i_vmem.at[0]], o_vmem)

            pltpu.emit_pipeline(
                body,
                grid=(n_idx // window,),
                in_specs=[pl.BlockSpec((1, window), lambda i: (0, i))],
                out_specs=[pl.BlockSpec((window, x_hbm.shape[1]), lambda i: (i, 0))],
                core_axis_name=("core", "subcore"),
                dimension_semantics=(pltpu.PARALLEL,),
            )(i_hbm, o_hbm)

    return gather(*xs, idx.reshape(1, n_idx))


def _moe_plan(top_idx, tm):
    T = top_idx.shape[0]
    n_asg = T * TOP_K
    n_tiles = n_asg // tm + N_EXPERTS
    flat_e = top_idx.reshape(n_asg)
    onehot = (flat_e[:, None] == jnp.arange(N_EXPERTS, dtype=jnp.int32)[None, :]).astype(jnp.int32)
    incl = jnp.cumsum(onehot, axis=0)
    counts = incl[-1]
    rank = jnp.sum((incl - onehot) * onehot, axis=1)
    tiles_e = (counts + tm - 1) // tm
    tile_end = jnp.cumsum(tiles_e)
    row_start_p = (tile_end - tiles_e) * tm
    row_start = jnp.cumsum(counts) - counts
    pos = (jnp.sum(onehot * row_start_p[None, :], axis=1) + rank).reshape(T, TOP_K).T.reshape(n_asg)
    tile_ids = jnp.arange(n_tiles, dtype=jnp.int32)
    tile_expert = jnp.minimum(jnp.searchsorted(tile_end, tile_ids, side="right"), N_EXPERTS - 1).astype(jnp.int32)
    n_rows = n_tiles * tm
    sorted_tok = jnp.argsort(flat_e, stable=True).astype(jnp.int32) // TOP_K
    window_src = jnp.concatenate([jnp.zeros((n_rows,), jnp.int32), sorted_tok, jnp.zeros((n_rows,), jnp.int32)])
    r = jnp.arange(n_rows, dtype=jnp.int32)
    src_tok = r % T
    for e in range(N_EXPERTS):
        shifted = lax.dynamic_slice(window_src, (n_rows - (row_start_p[e] - row_start[e]),), (n_rows,))
        in_run = (r >= row_start_p[e]) & (r < row_start_p[e] + counts[e])
        src_tok = jnp.where(in_run, shifted, src_tok)
    return src_tok, pos.astype(jnp.int32), tile_expert, tile_end[-1:].astype(jnp.int32)


def _moe_ffn_kernel(te_ref, nu_ref, xlo_ref, xhi_ref, wg_ref, wu_ref, wd_ref, olo_ref, ohi_ref, acc_s):
    i = pl.program_id(0)
    c = pl.program_id(1)
    last = pl.num_programs(1) - 1
    used = i < nu_ref[0]
    half = acc_s.shape[1] // 2

    @pl.when(used & (c == 0))
    def _():
        acc_s[...] = jnp.zeros(acc_s.shape, F32)

    @pl.when(used)
    def _():
        x = jnp.concatenate([_unpack_bf16_pairs(xlo_ref[...]), _unpack_bf16_pairs(xhi_ref[...])], axis=1)
        hmid = _silu(_dot(x, wg_ref[0])) * _dot(x, wu_ref[0])
        acc_s[...] += _dot(hmid.astype(BF16), wd_ref[0])

    @pl.when(used & (c == last))
    def _():
        olo_ref[...] = _pack_bf16_pairs(acc_s[:, :half])
        ohi_ref[...] = _pack_bf16_pairs(acc_s[:, half:])

    @pl.when(jnp.logical_not(used) & (c == last))
    def _():
        olo_ref[...] = jnp.zeros(olo_ref.shape, olo_ref.dtype)
        ohi_ref[...] = jnp.zeros(ohi_ref.shape, ohi_ref.dtype)


def moe_expert_ffn(xs_lo, xs_hi, tile_expert, n_used, w_gate, w_up, w_down, tm, fc=1792):
    R, half = xs_lo.shape
    D = 4 * half
    E, _, F = w_gate.shape
    rows = lambda i, c, te, nu: (i, 0)
    return pl.pallas_call(
        _moe_ffn_kernel,
        out_shape=(jax.ShapeDtypeStruct((R, half), jnp.uint32), jax.ShapeDtypeStruct((R, half), jnp.uint32)),
        grid_spec=pltpu.PrefetchScalarGridSpec(
            num_scalar_prefetch=2,
            grid=(R // tm, F // fc),
            in_specs=[pl.BlockSpec((tm, half), rows), pl.BlockSpec((tm, half), rows),
                      pl.BlockSpec((1, D, fc), lambda i, c, te, nu: (te[i], 0, c)),
                      pl.BlockSpec((1, D, fc), lambda i, c, te, nu: (te[i], 0, c)),
                      pl.BlockSpec((1, fc, D), lambda i, c, te, nu: (te[i], c, 0))],
            out_specs=(pl.BlockSpec((tm, half), rows), pl.BlockSpec((tm, half), rows)),
            scratch_shapes=[pltpu.VMEM((tm, D), F32)]),
        compiler_params=_cparams("parallel", "arbitrary"),
        name="moe_expert_ffn",
    )(tile_expert, n_used, xs_lo, xs_hi, w_gate, w_up, w_down)


def _moe_combine_kernel(h_ref, lo0_ref, hi0_ref, lo1_ref, hi1_ref, w_ref, g_ref, b_ref, o_ref):
    w = w_ref[...]
    y0 = jnp.concatenate([_unpack_bf16_pairs(lo0_ref[...]), _unpack_bf16_pairs(hi0_ref[...])], axis=1).astype(F32)
    y1 = jnp.concatenate([_unpack_bf16_pairs(lo1_ref[...]), _unpack_bf16_pairs(hi1_ref[...])], axis=1).astype(F32)
    ff = w[:, 0:1] * y0 + w[:, 1:2] * y1
    o_ref[...] = _layer_norm_rows(ALPHA * h_ref[...] + ff, g_ref[...], b_ref[...])


def moe_combine_ln(h, y_lo, y_hi, top_w, g, b, tm=512):
    T, D = h.shape
    half = D // 4
    nt = T // tm
    row = lambda i: (i, 0)
    second = lambda i: (nt + i, 0)
    fixed = lambda i: (0, 0)
    return pl.pallas_call(
        _moe_combine_kernel,
        out_shape=jax.ShapeDtypeStruct((T, D), F32),
        grid=(nt,),
        in_specs=[pl.BlockSpec((tm, D), row),
                  pl.BlockSpec((tm, half), row), pl.BlockSpec((tm, half), row),
                  pl.BlockSpec((tm, half), second), pl.BlockSpec((tm, half), second),
                  pl.BlockSpec((tm, TOP_K), row),
                  pl.BlockSpec((1, D), fixed), pl.BlockSpec((1, D), fixed)],
        out_specs=pl.BlockSpec((tm, D), row),
        compiler_params=_cparams("parallel"),
        name="moe_combine_ln",
    )(h, y_lo, y_hi, y_lo, y_hi, top_w, g.reshape(1, D), b.reshape(1, D))


def moe_ln(h, w_router, w_gate, w_up, w_down, g, b, tm=512):
    top_idx, top_w, hb_lo, hb_hi = moe_route(h, w_router)
    src_tok, pos, tile_expert, n_used = _moe_plan(top_idx, tm)
    xs_lo, xs_hi = sc_gather_rows([hb_lo, hb_hi], src_tok, SC_GATHER_WINDOW)
    ys_lo, ys_hi = moe_expert_ffn(xs_lo, xs_hi, tile_expert, n_used, w_gate, w_up, w_down, tm)
    y_lo, y_hi = sc_gather_rows([ys_lo, ys_hi], pos, SC_GATHER_WINDOW)
    return moe_combine_ln(h, y_lo, y_hi, top_w, g, b)


def _pad_cols(w, width):
    return jnp.pad(w, ((0, 0), (0, width - w.shape[1])))


def even_layer_mixer(h, w_in, ckv_gain, w_uk, w_uv, sinks, w_out, rel_table, ln_g, ln_b, batch, seq):
    cuts = np.cumsum((A_HEADS * HEAD_DIM, A_LATENT, IDX_HEADS * IDX_DIM, IDX_DIM, IDX_HEADS,
                      B_HEADS * HEAD_DIM, B_KV_HEADS * HEAD_DIM))
    w_qa, w_ckv, w_qi, w_ki, w_wi, w_qb, w_kb, w_vb = jnp.split(w_in, [int(c) for c in cuts], axis=1)
    w_main = jnp.concatenate([w_qa, w_qi, w_qb, _pad_cols(w_ki, LANES), w_kb, w_vb], axis=1).astype(BF16)
    w_aux = jnp.concatenate([w_ckv, _pad_cols(w_wi, LANES)], axis=1).astype(BF16)
    proj, aux = matmul_pair(h, w_main, w_aux)
    o_a = dsa_attention(proj, aux, ckv_gain, w_uk, w_uv, rel_table[:, :A_HEADS], batch, seq)
    o_b = banded_attention(proj, 2, 13, 14, B_HEADS, B_KV_HEADS,
                           rel_table[:, A_HEADS:A_HEADS + B_HEADS], B_WINDOW,
                           batch, seq, sinks=sinks)
    return matmul_residual_ln([o_a, o_b], w_out.astype(BF16), h, ln_g, ln_b, concat=True)


def odd_layer_mixer(h, w_in, pe_k, w1_k, w2_k, pe_v, w1_v, w2_v, w_out, rel_table, ln_g, ln_b, batch, seq):
    kvw = C_KV_HEADS * HEAD_DIM
    qw = C_HEADS * HEAD_DIM
    w_main = w_in[:, :qw + 6 * kvw].astype(BF16)
    w_gl = w_in[:, qw + 6 * kvw:].reshape(D_MODEL, C_HEADS, 3)
    w_gate = jnp.concatenate([_pad_cols(w_gl[:, :, r], LANES) for r in range(3)], axis=1).astype(BF16)
    proj, gates = matmul_pair(h, w_main, w_gate)
    tbl = rel_table[:, :C_HEADS]
    kc = compress_blocks(proj, qw // kvw, pe_k, w1_k, w2_k, batch, seq)
    vc = compress_blocks(proj, qw // kvw + 1, pe_v, w1_v, w2_v, batch, seq)
    o_cmp, pick = cmp_attention(proj, kc, vc, gates, tbl, batch, seq)
    o_slc = slc_attention(proj, pick, gates, tbl, batch, seq)
    o_win = banded_attention(proj, 0, 12, 13, C_HEADS, C_KV_HEADS, tbl, C_WINDOW,
                             batch, seq, gates=gates, gate_col=2)
    return matmul_residual_ln([o_cmp, o_slc, o_win], w_out.astype(BF16), h, ln_g, ln_b)


def kernel(x, mem, rel_table, ev_w_in, ev_ckv_gain, ev_w_uk, ev_w_uv, ev_sinks, ev_w_out, od_w_in, od_pe_k, od_w1_k, od_w2_k, od_pe_v, od_w1_v, od_w2_v, od_w_out, xa_w_q, xa_w_k, xa_w_v, xa_w_o, ff_w_gate, ff_w_up, ff_w_down, moe_w_router, moe_w_gate, moe_w_up, moe_w_down, ln_g, ln_b):
    batch, seq, D = x.shape
    h = x.reshape(batch * seq, D)
    mem2 = mem.reshape(-1, D)
    for i in range(DEPTH):
        j = i // 2
        if i % 2 == 0:
            h = even_layer_mixer(h, ev_w_in[j], ev_ckv_gain[j], ev_w_uk[j], ev_w_uv[j], ev_sinks[j],
                                 ev_w_out[j], rel_table, ln_g[i, 0], ln_b[i, 0], batch, seq)
        else:
            h = odd_layer_mixer(h, od_w_in[j], od_pe_k[j], od_w1_k[j], od_w2_k[j], od_pe_v[j],
                                od_w1_v[j], od_w2_v[j], od_w_out[j], rel_table, ln_g[i, 0], ln_b[i, 0],
                                batch, seq)
        h = cross_attention_ln(h, mem2, xa_w_q[i], xa_w_k[i], xa_w_v[i], xa_w_o[i],
                               ln_g[i, 1], ln_b[i, 1], batch, seq)
        if i % 2 == 0:
            riders = (moe_w_gate[j], moe_w_up[j], moe_w_down[j]) if i + 1 < DEPTH else ()
            h, moe_bf16 = swiglu_ln(h, ff_w_gate[j], ff_w_up[j], ff_w_down[j], ln_g[i, 2], ln_b[i, 2], riders)
        else:
            h = moe_ln(h, moe_w_router[j], *moe_bf16, ln_g[i, 2], ln_b[i, 2])
    return h.reshape(batch, seq, D)
```
